```python
import jax, jax.numpy as jnp
from jax import lax
import numpy as np

D_MODEL = 2048
BATCH = 16
SEQ = 2048
DEPTH = 1

RET_HEADS = 8
RET_DK = 128
RET_DV = 256
RET_CHUNK = 128
ROPE_BASE = 10000.0
SWA_Q_HEADS = 16
SWA_KV_HEADS = 4
SWA_HEAD_DIM = 64
SWA_WINDOW = 128
SWA_BLOCK = 128
D_FF = 5632
CONV_WIDTH = 3
RMS_EPS = 1e-6

RET_QK_W = RET_HEADS * RET_DK
RET_V_W = RET_HEADS * RET_DV
SWA_Q_W = SWA_Q_HEADS * SWA_HEAD_DIM
SWA_KV_W = SWA_KV_HEADS * SWA_HEAD_DIM
IN_WIDTHS = (RET_QK_W, RET_QK_W, RET_V_W, RET_V_W, SWA_Q_W, SWA_KV_W, SWA_KV_W, D_MODEL, D_MODEL)
IN_WIDTH = sum(IN_WIDTHS)
IN_SPLITS = tuple(int(s) for s in np.cumsum(IN_WIDTHS)[:-1])

kernel_name = "hybrid_retention_swa_sink_convffn"


def rmsnorm(x, g):
    xf = x.astype(jnp.float32)
    y = xf * lax.rsqrt(jnp.mean(xf * xf, axis=-1, keepdims=True) + RMS_EPS)
    return (y * g.astype(jnp.float32)).astype(x.dtype)


def rotary(x, pos):
    d = x.shape[-1]
    inv = 1.0 / (ROPE_BASE ** (jnp.arange(0, d, 2, dtype=jnp.float32) / d))
    ang = pos.astype(jnp.float32)[:, None] * inv[None, :]
    cos = jnp.cos(ang)[None, :, None, :]
    sin = jnp.sin(ang)[None, :, None, :]
    x1, x2 = x[..., : d // 2], x[..., d // 2:]
    return jnp.concatenate([x1 * cos - x2 * sin, x1 * sin + x2 * cos], axis=-1)


def retention_chunkwise(q, k, v):
    B, S, H, dk = q.shape
    dv = v.shape[-1]
    C = RET_CHUNK
    N = S // C
    log_gamma = jnp.log(1.0 - 2.0 ** (-5.0 - jnp.arange(H, dtype=jnp.float32)))
    idx = jnp.arange(C, dtype=jnp.float32)
    rel = idx[:, None] - idx[None, :]
    inner_decay = jnp.where(rel[None] >= 0,
                            jnp.exp(log_gamma[:, None, None] * jnp.maximum(rel, 0.0)[None]), 0.0)
    xi = jnp.exp(log_gamma[:, None] * (idx + 1.0))[None, :, :, None]
    zeta = jnp.exp(log_gamma[:, None] * (C - 1.0 - idx))[None, :, :, None]
    chunk_decay = jnp.exp(log_gamma * C)[None, :, None, None]
    k = k * (dk ** -0.5)

    def to_chunks(t):
        return t.reshape(B, N, C, H, t.shape[-1]).transpose(1, 0, 3, 2, 4)

    def step(state, inp):
        qi, ki, vi = inp
        scores = jnp.einsum('bhqd,bhkd->bhqk', qi, ki) * inner_decay[None]
        inner = jnp.einsum('bhqk,bhkv->bhqv', scores, vi)
        cross = jnp.einsum('bhqd,bhdv->bhqv', qi, state) * xi
        new_state = state * chunk_decay + jnp.einsum('bhkd,bhkv->bhdv', ki * zeta, vi)
        return new_state, inner + cross

    init = jnp.zeros((B, H, dk, dv), jnp.float32)
    _, out = lax.scan(step, init, (to_chunks(q), to_chunks(k), to_chunks(v)))
    return out.transpose(1, 0, 3, 2, 4).reshape(B, S, H, dv)


def swa_attention_sinks(q, k, v, sinks):
    B, S, Hq, dh = q.shape
    Hkv = k.shape[2]
    G = Hq // Hkv
    C = SWA_BLOCK
    N = S // C
    qb = q.astype(jnp.float32).reshape(B, N, C, Hkv, G, dh)

    def with_prev(t):
        tb = t.astype(jnp.float32).reshape(B, N, C, Hkv, dh)
        prev = jnp.pad(tb[:, :-1], ((0, 0), (1, 0), (0, 0), (0, 0), (0, 0)))
        return jnp.concatenate([prev, tb], axis=2)

    kw = with_prev(k)
    vw = with_prev(v)
    scores = jnp.einsum('bnqhgd,bnkhd->bnhgqk', qb, kw) * (dh ** -0.5)
    qpos = jnp.arange(C)[:, None] + C
    kpos = jnp.arange(2 * C)[None, :]
    relp = qpos - kpos
    band = (relp >= 0) & (relp < SWA_WINDOW)
    blk = jnp.arange(N)
    valid = band[None] & ((blk[:, None, None] > 0) | (kpos[None] >= C))
    scores = jnp.where(valid[None, :, None, None], scores, -jnp.inf)
    sink = sinks.astype(jnp.float32).reshape(Hkv, G)[None, None, :, :, None, None]
    m = jnp.maximum(jnp.max(scores, axis=-1, keepdims=True), sink)
    e = jnp.exp(scores - m)
    p = e / (jnp.sum(e, axis=-1, keepdims=True) + jnp.exp(sink - m))
    out = jnp.einsum('bnhgqk,bnkhd->bnqhgd', p, vw)
    return out.reshape(B, S, Hq * dh)


def causal_depthwise_conv(u, w, b):
    S = u.shape[1]
    up = jnp.pad(u, ((0, 0), (CONV_WIDTH - 1, 0), (0, 0)))
    y = b
    for kk in range(CONV_WIDTH):
        y = y + up[:, kk:kk + S] * w[kk]
    return y


def setup_inputs(seed: int = 0) -> dict:
    key = jax.random.key(seed)
    ks = jax.random.split(key, 16)
    f = jnp.float32
    L = DEPTH

    def nrm(k, shape, scale):
        return jax.random.normal(k, shape, f) * scale

    return {
        "x": jax.random.normal(ks[0], (BATCH, SEQ, D_MODEL), f),
        "g_pre_mix": 1.0 + nrm(ks[1], (L, D_MODEL), 0.02),
        "w_in": nrm(ks[2], (L, D_MODEL, IN_WIDTH), D_MODEL ** -0.5),
        "w_ret_o": nrm(ks[3], (L, RET_V_W, D_MODEL), RET_V_W ** -0.5),
        "w_swa_o": nrm(ks[4], (L, SWA_Q_W, D_MODEL), SWA_Q_W ** -0.5),
        "w_out": nrm(ks[5], (L, D_MODEL, D_MODEL), D_MODEL ** -0.5),
        "swa_sinks": nrm(ks[6], (L, SWA_Q_HEADS), 0.5),
        "g_post_mix": 1.0 + nrm(ks[7], (L, D_MODEL), 0.02),
        "g_pre_ffn": 1.0 + nrm(ks[8], (L, D_MODEL), 0.02),
        "w_up": nrm(ks[9], (L, D_MODEL, 2 * D_FF), D_MODEL ** -0.5),
        "conv_w": nrm(ks[10], (L, CONV_WIDTH, 2 * D_FF), CONV_WIDTH ** -0.5),
        "conv_b": nrm(ks[11], (L, 2 * D_FF), 0.01),
        "w_down": nrm(ks[12], (L, D_FF, D_MODEL), D_FF ** -0.5),
        "g_post_ffn": 1.0 + nrm(ks[13], (L, D_MODEL), 0.02),
    }


def reference(x, g_pre_mix, w_in, w_ret_o, w_swa_o, w_out, swa_sinks, g_post_mix,
              g_pre_ffn, w_up, conv_w, conv_b, w_down, g_post_ffn):
    B, S, _ = x.shape
    pos = jnp.arange(S, dtype=jnp.int32)
    for l in range(DEPTH):
        h = rmsnorm(x, g_pre_mix[l])
        proj = h @ w_in[l]
        rq, rk, rv, rg, sq, sk, sv, gr, gs = jnp.split(proj, IN_SPLITS, axis=-1)

        rq = rotary(rq.astype(jnp.float32).reshape(B, S, RET_HEADS, RET_DK), pos)
        rk = rotary(rk.astype(jnp.float32).reshape(B, S, RET_HEADS, RET_DK), pos)
        rv = rv.astype(jnp.float32).reshape(B, S, RET_HEADS, RET_DV)
        ret = retention_chunkwise(rq, rk, rv)
        ret = ret * lax.rsqrt(jnp.mean(ret * ret, axis=-1, keepdims=True) + RMS_EPS)
        ret = (jax.nn.silu(rg.astype(jnp.float32)) * ret.reshape(B, S, RET_V_W)).astype(x.dtype)
        ret_out = ret @ w_ret_o[l]

        swa = swa_attention_sinks(sq.reshape(B, S, SWA_Q_HEADS, SWA_HEAD_DIM),
                                  sk.reshape(B, S, SWA_KV_HEADS, SWA_HEAD_DIM),
                                  sv.reshape(B, S, SWA_KV_HEADS, SWA_HEAD_DIM),
                                  swa_sinks[l]).astype(x.dtype)
        swa_out = swa @ w_swa_o[l]

        mixed = jax.nn.sigmoid(gr) * ret_out + jax.nn.sigmoid(gs) * swa_out
        x = x + rmsnorm(mixed @ w_out[l], g_post_mix[l])

        h = rmsnorm(x, g_pre_ffn[l])
        u = causal_depthwise_conv(h @ w_up[l], conv_w[l], conv_b[l])
        val, gate = jnp.split(u, 2, axis=-1)
        y = (jax.nn.gelu(gate, approximate=True) * val) @ w_down[l]
        x = x + rmsnorm(y, g_post_ffn[l])
    return x
```

```python
import functools

import jax
import jax.numpy as jnp
from jax import lax
from jax.experimental import pallas as pl
from jax.experimental.pallas import tpu as pltpu

D_MODEL = 2048
RET_HEADS = 8
RET_DK = 128
RET_DV = 256
RET_CHUNK = 128
ROPE_BASE = 10000.0
SWA_Q_HEADS = 16
SWA_KV_HEADS = 4
SWA_HEAD_DIM = 64
SWA_BLOCK = 128
D_FF = 5632
CONV_WIDTH = 3
RMS_EPS = 1e-6

RET_QK_W = RET_HEADS * RET_DK
RET_V_W = RET_HEADS * RET_DV
SWA_Q_W = SWA_Q_HEADS * SWA_HEAD_DIM
SWA_KV_W = SWA_KV_HEADS * SWA_HEAD_DIM

COL_GR = 0
COL_GS = COL_GR + D_MODEL
COL_RV = COL_GS + D_MODEL
COL_RG = COL_RV + RET_V_W
COL_RQ = COL_RG + RET_V_W
COL_RK = COL_RQ + RET_QK_W
COL_SQ = COL_RK + RET_QK_W
COL_SK = COL_SQ + SWA_Q_W
COL_SV = COL_SK + SWA_KV_W
IN_WIDTH = COL_SV + SWA_KV_W

V7X_VMEM_LIMIT_BYTES = 56 * 1024 * 1024
BF16_SUBLANES = 16
NORM_ROWS = 16

INPROJ_TM = 1024
INPROJ_TN = 512
MIX_TM = 512
MIX_TN = 512
FFN_TM = 512
FFN_TN = 512
FFN_HALO = BF16_SUBLANES

f32 = jnp.float32
bf16 = jnp.bfloat16


def _params(semantics):
    return pltpu.CompilerParams(dimension_semantics=semantics, vmem_limit_bytes=V7X_VMEM_LIMIT_BYTES)


def _rmsnorm(x, g):
    ms = jnp.mean(x * x, axis=-1, keepdims=True)
    return x * lax.rsqrt(ms + RMS_EPS) * g


def _norm_rows_to(x_ref, g_ref, dst_ref, dst_off, rows):
    g = g_ref[...]

    def step(c, carry):
        r = pl.multiple_of(c * NORM_ROWS, NORM_ROWS)
        dst_ref[pl.ds(dst_off + r, NORM_ROWS), :] = _rmsnorm(x_ref[pl.ds(r, NORM_ROWS), :], g).astype(dst_ref.dtype)
        return carry

    lax.fori_loop(0, rows // NORM_ROWS, step, 0)


_T_SIG_END = COL_RV // INPROJ_TN
_T_RV_END = COL_RG // INPROJ_TN
_T_RG_END = COL_RQ // INPROJ_TN
_T_RQ_END = COL_RK // INPROJ_TN
_T_RK_END = COL_SQ // INPROJ_TN


def _inproj_kernel(x_ref, g_ref, w_ref, rot_ref, o_ref, h_scr):
    j = pl.program_id(1)
    tm = x_ref.shape[0]

    @pl.when(j == 0)
    def _():
        _norm_rows_to(x_ref, g_ref, h_scr, 0, tm)

    acc = jnp.dot(h_scr[...], w_ref[...], preferred_element_type=f32)

    @pl.when(j < _T_SIG_END)
    def _():
        o_ref[...] = jax.nn.sigmoid(acc).astype(o_ref.dtype)

    @pl.when(((j >= _T_SIG_END) & (j < _T_RV_END)) | (j >= _T_RK_END))
    def _():
        o_ref[...] = acc.astype(o_ref.dtype)

    @pl.when((j >= _T_RV_END) & (j < _T_RG_END))
    def _():
        o_ref[...] = (acc * jax.nn.sigmoid(acc)).astype(o_ref.dtype)

    @pl.when((j >= _T_RG_END) & (j < _T_RK_END))
    def _():
        cos = rot_ref[:, 0:RET_DK]
        sin = rot_ref[:, RET_DK:2 * RET_DK]
        for hh in range(INPROJ_TN // RET_DK):
            sl = slice(hh * RET_DK, (hh + 1) * RET_DK)
            xh = acc[:, sl]
            o_ref[:, sl] = (xh * cos + pltpu.roll(xh, RET_DK // 2, axis=1) * sin).astype(o_ref.dtype)


def _inproj(x2, g, w_in_r, rot, seq):
    T = x2.shape[0]
    tm, tn = INPROJ_TM, INPROJ_TN
    tm = min(tm, seq)
    spt = seq // tm
    grid = (T // tm, IN_WIDTH // tn)
    return pl.pallas_call(
        _inproj_kernel,
        grid=grid,
        in_specs=[
            pl.BlockSpec((tm, D_MODEL), lambda i, j: (i, 0)),
            pl.BlockSpec((1, D_MODEL), lambda i, j: (0, 0)),
            pl.BlockSpec((D_MODEL, tn), lambda i, j: (0, j)),
            pl.BlockSpec((None, tm, 2 * RET_DK), lambda i, j: (jnp.where(j >= _T_RQ_END, 1, 0), i % spt, 0)),
        ],
        out_specs=pl.BlockSpec((tm, tn), lambda i, j: (i, j)),
        out_shape=jax.ShapeDtypeStruct((T, IN_WIDTH), bf16),
        scratch_shapes=[pltpu.VMEM((tm, D_MODEL), bf16)],
        compiler_params=_params(("parallel", "arbitrary")),
        name="inproj",
    )(x2, g, w_in_r, rot)


def _retention_kernel(q_ref, k_ref, v_ref, gate_ref, dmat_ref, xi_ref, zeta_ref, cdec_ref, o_ref, kv_scr, st_scr):
    S = q_ref.shape[0]
    C = RET_CHUNK
    N = S // C
    q3 = q_ref[...].reshape(N, C, RET_DK)
    k3 = k_ref[...].reshape(N, C, RET_DK)
    v3 = v_ref[...].reshape(N, C, RET_DV)

    kz = (k3.astype(f32) * zeta_ref[...][None]).astype(bf16)
    kzt = jnp.swapaxes(kz, 1, 2)
    kv_scr[...] = jnp.einsum("ndk,nkv->ndv", kzt, v3, preferred_element_type=f32)

    cdec = cdec_ref[...]

    def scan(n, state):
        st_scr[n] = state.astype(bf16)
        return state * cdec + kv_scr[n]

    lax.fori_loop(0, N, scan, jnp.zeros((RET_DK, RET_DV), f32))

    sc = jnp.einsum("nqd,nkd->nqk", q3, k3, preferred_element_type=f32) * dmat_ref[...][None]
    inner = jnp.einsum("nqk,nkv->nqv", sc.astype(bf16), v3, preferred_element_type=f32)
    cross = jnp.einsum("nqd,ndv->nqv", q3, st_scr[...], preferred_element_type=f32) * xi_ref[...][None]
    o = inner + cross
    o = o * lax.rsqrt(jnp.mean(o * o, axis=-1, keepdims=True) + RMS_EPS)
    o = gate_ref[...].astype(f32).reshape(N, C, RET_DV) * o
    o_ref[...] = o.reshape(S, RET_DV).astype(o_ref.dtype)


def _retention_tables():
    C = RET_CHUNK
    log_gamma = jnp.log(1.0 - 2.0 ** (-5.0 - jnp.arange(RET_HEADS, dtype=f32)))
    idx = jnp.arange(C, dtype=f32)
    rel = idx[:, None] - idx[None, :]
    dmat = jnp.where(rel[None] >= 0, jnp.exp(log_gamma[:, None, None] * jnp.maximum(rel, 0.0)[None]), 0.0)
    xi = jnp.exp(log_gamma[:, None] * (idx + 1.0))
    zeta = jnp.exp(log_gamma[:, None] * (C - 1.0 - idx))
    cdec = jnp.exp(log_gamma * C)
    xi_b = jnp.broadcast_to(xi[:, :, None], (RET_HEADS, C, RET_DV))
    zeta_b = jnp.broadcast_to(zeta[:, :, None], (RET_HEADS, C, RET_DK))
    cdec_b = jnp.broadcast_to(cdec[:, None, None], (RET_HEADS, 1, RET_DV))
    return dmat.astype(f32), xi_b.astype(f32), zeta_b.astype(f32), cdec_b.astype(f32)


def _retention(proj, batch, seq):
    T = proj.shape[0]
    C = RET_CHUNK
    N = seq // C
    dmat, xi_b, zeta_b, cdec_b = _retention_tables()
    return pl.pallas_call(
        _retention_kernel,
        grid=(batch, RET_HEADS),
        in_specs=[
            pl.BlockSpec((seq, RET_DK), lambda b, h: (b, COL_RQ // RET_DK + h)),
            pl.BlockSpec((seq, RET_DK), lambda b, h: (b, COL_RK // RET_DK + h)),
            pl.BlockSpec((seq, RET_DV), lambda b, h: (b, COL_RV // RET_DV + h)),
            pl.BlockSpec((seq, RET_DV), lambda b, h: (b, COL_RG // RET_DV + h)),
            pl.BlockSpec((None, C, C), lambda b, h: (h, 0, 0)),
            pl.BlockSpec((None, C, RET_DV), lambda b, h: (h, 0, 0)),
            pl.BlockSpec((None, C, RET_DK), lambda b, h: (h, 0, 0)),
            pl.BlockSpec((None, 1, RET_DV), lambda b, h: (h, 0, 0)),
        ],
        out_specs=pl.BlockSpec((seq, RET_DV), lambda b, h: (b, h)),
        out_shape=jax.ShapeDtypeStruct((T, RET_V_W), bf16),
        scratch_shapes=[pltpu.VMEM((N, RET_DK, RET_DV), f32), pltpu.VMEM((N, RET_DK, RET_DV), bf16)],
        compiler_params=_params(("parallel", "parallel")),
        name="retention",
    )(proj, proj, proj, proj, dmat, xi_b, zeta_b, cdec_b)


KV_PER_STEP = 2
Q_PER_KV = SWA_Q_HEADS // SWA_KV_HEADS
SWA_Q_STEP_W = KV_PER_STEP * Q_PER_KV * SWA_HEAD_DIM
LANES = 128


def _swa_kernel(sink_ref, q_ref, k_ref, v_ref, o_ref, klo, khi, vlo, vhi):
    S = q_ref.shape[0]
    C = SWA_BLOCK
    gp = pl.program_id(1)
    half = SWA_HEAD_DIM

    lane_s = lax.broadcasted_iota(jnp.int32, (S, LANES), 1)
    low_s = lane_s < half

    def prep(src_ref, lo_scr, hi_scr):
        w = src_ref[...].astype(f32)
        r = pltpu.roll(w, half, axis=1)
        zeros = jnp.zeros((C, LANES), bf16)
        for t in range(KV_PER_STEP):
            lo_scr[t, 0:C, :] = zeros
            hi_scr[t, 0:C, :] = zeros
        lo_scr[0, C:C + S, :] = jnp.where(low_s, w, 0.0).astype(bf16)
        hi_scr[0, C:C + S, :] = jnp.where(low_s, 0.0, r).astype(bf16)
        lo_scr[1, C:C + S, :] = jnp.where(low_s, r, 0.0).astype(bf16)
        hi_scr[1, C:C + S, :] = jnp.where(low_s, 0.0, w).astype(bf16)

    prep(k_ref, klo, khi)
    prep(v_ref, vlo, vhi)

    rows = 2 * C
    win = 2 * C
    row_i = lax.broadcasted_iota(jnp.int32, (rows, win), 0) & (C - 1)
    col_j = lax.broadcasted_iota(jnp.int32, (rows, win), 1)
    band = (col_j > row_i) & (col_j <= row_i + C)
    cur = col_j >= C
    first_pair = lax.broadcasted_iota(jnp.int32, (rows, 1), 0) < C
    low_o = lax.broadcasted_iota(jnp.int32, (rows, LANES), 1) < half
    nt = (((1,), (1,)), ((), ()))

    def softmax_parts(s, sink, valid):
        s = jnp.where(valid, s * (SWA_HEAD_DIM ** -0.5), -jnp.inf)
        m = jnp.maximum(jnp.max(s, axis=-1, keepdims=True), sink)
        e = jnp.exp(s - m)
        den = jnp.sum(e, axis=-1, keepdims=True) + jnp.exp(sink - m)
        return e.astype(bf16), 1.0 / den

    def block(n, carry):
        r0 = pl.multiple_of(n * C, C)
        valid = band & (cur | (n > 0))
        for t in range(KV_PER_STEP):
            c0 = t * Q_PER_KV * SWA_HEAD_DIM
            hbase = gp * (KV_PER_STEP * Q_PER_KV) + t * Q_PER_KV
            qs = jnp.concatenate([q_ref[pl.ds(r0, C), c0:c0 + LANES],
                                  q_ref[pl.ds(r0, C), c0 + LANES:c0 + 2 * LANES]], axis=0)
            s_e = lax.dot_general(qs, klo[t, pl.ds(r0, win), :], nt, preferred_element_type=f32)
            s_o = lax.dot_general(qs, khi[t, pl.ds(r0, win), :], nt, preferred_element_type=f32)
            sink_e = jnp.where(first_pair, sink_ref[hbase + 0], sink_ref[hbase + 2])
            sink_o = jnp.where(first_pair, sink_ref[hbase + 1], sink_ref[hbase + 3])
            p_e, r_e = softmax_parts(s_e, sink_e, valid)
            p_o, r_o = softmax_parts(s_o, sink_o, valid)
            o = (jnp.dot(p_e, vlo[t, pl.ds(r0, win), :], preferred_element_type=f32)
                 + jnp.dot(p_o, vhi[t, pl.ds(r0, win), :], preferred_element_type=f32))
            o = o * jnp.where(low_o, r_e, r_o)
            o_ref[pl.ds(r0, C), c0:c0 + LANES] = o[0:C].astype(o_ref.dtype)
            o_ref[pl.ds(r0, C), c0 + LANES:c0 + 2 * LANES] = o[C:2 * C].astype(o_ref.dtype)
        return carry

    lax.fori_loop(0, S // C, block, 0)


def _swa(proj, sinks, batch, seq):
    T = proj.shape[0]
    steps = SWA_KV_HEADS // KV_PER_STEP
    scr = pltpu.VMEM((KV_PER_STEP, seq + SWA_BLOCK, LANES), bf16)
    return pl.pallas_call(
        _swa_kernel,
        grid=(batch, steps),
        in_specs=[
            pl.BlockSpec(memory_space=pltpu.SMEM),
            pl.BlockSpec((seq, SWA_Q_STEP_W), lambda b, g: (b, COL_SQ // SWA_Q_STEP_W + g)),
            pl.BlockSpec((seq, LANES), lambda b, g: (b, COL_SK // LANES + g)),
            pl.BlockSpec((seq, LANES), lambda b, g: (b, COL_SV // LANES + g)),
        ],
        out_specs=pl.BlockSpec((seq, SWA_Q_STEP_W), lambda b, g: (b, g)),
        out_shape=jax.ShapeDtypeStruct((T, SWA_Q_W), bf16),
        scratch_shapes=[scr, scr, scr, scr],
        compiler_params=_params(("parallel", "parallel")),
        name="swa",
    )(sinks, proj, proj, proj)


def _mix_kernel(ret_ref, swa_ref, gr_ref, gs_ref, wr_ref, ws_ref, wo_ref, x_ref, g_ref, o_ref, acc_scr):
    j = pl.program_id(1)
    ret_out = jnp.dot(ret_ref[...], wr_ref[...], preferred_element_type=f32)
    swa_out = jnp.dot(swa_ref[...], ws_ref[...], preferred_element_type=f32)
    mixed = gr_ref[...].astype(f32) * ret_out + gs_ref[...].astype(f32) * swa_out
    part = jnp.dot(mixed.astype(bf16), wo_ref[...], preferred_element_type=f32)

    @pl.when(j == 0)
    def _():
        acc_scr[...] = part

    @pl.when(j > 0)
    def _():
        acc_scr[...] += part

    @pl.when(j == pl.num_programs(1) - 1)
    def _():
        g = g_ref[...]

        def step(c, carry):
            r = pl.multiple_of(c * NORM_ROWS, NORM_ROWS)
            rows = pl.ds(r, NORM_ROWS)
            o_ref[rows, :] = x_ref[rows, :] + _rmsnorm(acc_scr[rows, :], g)
            return carry

        lax.fori_loop(0, o_ref.shape[0] // NORM_ROWS, step, 0)


def _mix(ret, swa, proj, w_ret_o, w_swa_o, w_out, x2, g):
    T = x2.shape[0]
    tm, tn = min(MIX_TM, T), MIX_TN
    nj = D_MODEL // tn
    return pl.pallas_call(
        _mix_kernel,
        grid=(T // tm, nj),
        in_specs=[
            pl.BlockSpec((tm, RET_V_W), lambda i, j: (i, 0)),
            pl.BlockSpec((tm, SWA_Q_W), lambda i, j: (i, 0)),
            pl.BlockSpec((tm, tn), lambda i, j: (i, COL_GR // tn + j)),
            pl.BlockSpec((tm, tn), lambda i, j: (i, COL_GS // tn + j)),
            pl.BlockSpec((RET_V_W, tn), lambda i, j: (0, j)),
            pl.BlockSpec((SWA_Q_W, tn), lambda i, j: (0, j)),
            pl.BlockSpec((tn, D_MODEL), lambda i, j: (j, 0)),
            pl.BlockSpec((tm, D_MODEL), lambda i, j: (i, 0)),
            pl.BlockSpec((1, D_MODEL), lambda i, j: (0, 0)),
        ],
        out_specs=pl.BlockSpec((tm, D_MODEL), lambda i, j: (i, 0)),
        out_shape=jax.ShapeDtypeStruct((T, D_MODEL), f32),
        scratch_shapes=[pltpu.VMEM((tm, D_MODEL), f32)],
        compiler_params=_params(("parallel", "arbitrary")),
        name="mix",
    )(ret, swa, proj, proj, w_ret_o, w_swa_o, w_out, x2, g)


def _gelu_tanh(x):
    return x * (0.5 * (1.0 + jnp.tanh(0.7978845608028654 * (x + 0.044715 * (x * x * x)))))


def _ffn_kernel(x_ref, halo_ref, gpre_ref, wv_ref, wg_ref, cwv_ref, cwg_ref, cbv_ref, cbg_ref, wd_ref, gpost_ref,
                o_ref, h_scr, u_scr, acc_scr, *, tiles_per_seq):
    i = pl.program_id(0)
    j = pl.program_id(1)
    tm = x_ref.shape[0]
    tn = wv_ref.shape[1]
    H = FFN_HALO

    @pl.when(j == 0)
    def _():
        keep = jnp.where(i % tiles_per_seq == 0, 0.0, 1.0)
        h_scr[0:H, :] = (_rmsnorm(halo_ref[...], gpre_ref[...]) * keep).astype(bf16)
        _norm_rows_to(x_ref, gpre_ref, h_scr, H, tm)

    h = h_scr[...]
    u_scr[:, 0:tn] = jnp.dot(h, wv_ref[...], preferred_element_type=f32)
    u_scr[:, tn:2 * tn] = jnp.dot(h, wg_ref[...], preferred_element_type=f32)

    def conv(cols, cw_ref, cb_ref):
        y = cb_ref[...]
        for kk in range(CONV_WIDTH):
            y = y + u_scr[pl.ds(H - (CONV_WIDTH - 1) + kk, tm), cols] * cw_ref[kk:kk + 1, :]
        return y

    val = conv(slice(0, tn), cwv_ref, cbv_ref)
    gate = conv(slice(tn, 2 * tn), cwg_ref, cbg_ref)
    a = (_gelu_tanh(gate) * val).astype(bf16)
    part = jnp.dot(a, wd_ref[...], preferred_element_type=f32)

    @pl.when(j == 0)
    def _():
        acc_scr[...] = part

    @pl.when(j > 0)
    def _():
        acc_scr[...] += part

    @pl.when(j == pl.num_programs(1) - 1)
    def _():
        g = gpost_ref[...]

        def step(c, carry):
            r = pl.multiple_of(c * NORM_ROWS, NORM_ROWS)
            rows = pl.ds(r, NORM_ROWS)
            o_ref[rows, :] = x_ref[rows, :] + _rmsnorm(acc_scr[rows, :], g)
            return carry

        lax.fori_loop(0, tm // NORM_ROWS, step, 0)


def _ffn(x1, g_pre, w_up, conv_w, conv_b, w_down, g_post, seq):
    T = x1.shape[0]
    tm, tn = min(FFN_TM, seq), FFN_TN
    nj = D_FF // tn
    hb = tm // FFN_HALO
    kern = functools.partial(_ffn_kernel, tiles_per_seq=seq // tm)
    return pl.pallas_call(
        kern,
        grid=(T // tm, nj),
        in_specs=[
            pl.BlockSpec((tm, D_MODEL), lambda i, j: (i, 0)),
            pl.BlockSpec((FFN_HALO, D_MODEL), lambda i, j: (jnp.maximum(i * hb - 1, 0), 0)),
            pl.BlockSpec((1, D_MODEL), lambda i, j: (0, 0)),
            pl.BlockSpec((D_MODEL, tn), lambda i, j: (0, j)),
            pl.BlockSpec((D_MODEL, tn), lambda i, j: (0, nj + j)),
            pl.BlockSpec((CONV_WIDTH, tn), lambda i, j: (0, j)),
            pl.BlockSpec((CONV_WIDTH, tn), lambda i, j: (0, nj + j)),
            pl.BlockSpec((1, tn), lambda i, j: (0, j)),
            pl.BlockSpec((1, tn), lambda i, j: (0, nj + j)),
            pl.BlockSpec((tn, D_MODEL), lambda i, j: (j, 0)),
            pl.BlockSpec((1, D_MODEL), lambda i, j: (0, 0)),
        ],
        out_specs=pl.BlockSpec((tm, D_MODEL), lambda i, j: (i, 0)),
        out_shape=jax.ShapeDtypeStruct((T, D_MODEL), f32),
        scratch_shapes=[
            pltpu.VMEM((tm + FFN_HALO, D_MODEL), bf16),
            pltpu.VMEM((tm + FFN_HALO, 2 * tn), f32),
            pltpu.VMEM((tm, D_MODEL), f32),
        ],
        compiler_params=_params(("parallel", "arbitrary")),
        name="ffn",
    )(x1, x1, g_pre, w_up, w_up, conv_w, conv_w, conv_b, conv_b, w_down, g_post)


def _rotary_tables(seq):
    d = RET_DK
    inv = 1.0 / (ROPE_BASE ** (jnp.arange(0, d, 2, dtype=f32) / d))
    ang = jnp.arange(seq, dtype=jnp.int32).astype(f32)[:, None] * inv[None, :]
    cos, sin = jnp.cos(ang), jnp.sin(ang)
    tab = jnp.concatenate([cos, cos, -sin, sin], axis=-1)
    return jnp.stack([tab, tab * (d ** -0.5)])


def _reorder_in_weight(w):
    widths = (RET_QK_W, RET_QK_W, RET_V_W, RET_V_W, SWA_Q_W, SWA_KV_W, SWA_KV_W, D_MODEL, D_MODEL)
    offs = [0]
    for wd in widths:
        offs.append(offs[-1] + wd)
    rq, rk, rv, rg, sq, sk, sv, gr, gs = [w[:, offs[t]:offs[t + 1]] for t in range(len(widths))]
    return jnp.concatenate([gr, gs, rv, rg, rq, rk, sq, sk, sv], axis=1).astype(bf16)


def kernel(x, g_pre_mix, w_in, w_ret_o, w_swa_o, w_out, swa_sinks, g_post_mix, g_pre_ffn, w_up, conv_w, conv_b,
           w_down, g_post_ffn):
    B, S, D = x.shape
    depth = w_in.shape[0]
    rot = _rotary_tables(S)
    x2 = x.reshape(B * S, D)
    for l in range(depth):
        proj = _inproj(x2, g_pre_mix[l][None], _reorder_in_weight(w_in[l]), rot, S)
        ret = _retention(proj, B, S)
        swa = _swa(proj, swa_sinks[l].astype(f32), B, S)
        x2 = _mix(ret, swa, proj, w_ret_o[l].astype(bf16), w_swa_o[l].astype(bf16), w_out[l].astype(bf16), x2,
                  g_post_mix[l][None])
        x2 = _ffn(x2, g_pre_ffn[l][None], w_up[l].astype(bf16), conv_w[l], conv_b[l][None], w_down[l].astype(bf16),
                  g_post_ffn[l][None], S)
    return x2.reshape(B, S, D)
```

```python
import functools

import jax
import jax.numpy as jnp
from jax import lax
from jax.experimental import pallas as pl
from jax.experimental.pallas import tpu as pltpu

D_MODEL = 2048
RET_HEADS = 8
RET_DK = 128
RET_DV = 256
RET_CHUNK = 128
ROPE_BASE = 10000.0
SWA_Q_HEADS = 16
SWA_KV_HEADS = 4
SWA_HEAD_DIM = 64
SWA_BLOCK = 128
D_FF = 5632
CONV_WIDTH = 3
RMS_EPS = 1e-6

RET_QK_W = RET_HEADS * RET_DK
RET_V_W = RET_HEADS * RET_DV
SWA_Q_W = SWA_Q_HEADS * SWA_HEAD_DIM
SWA_KV_W = SWA_KV_HEADS * SWA_HEAD_DIM

COL_GR = 0
COL_GS = COL_GR + D_MODEL
COL_RV = COL_GS + D_MODEL
COL_RG = COL_RV + RET_V_W
COL_RQ = COL_RG + RET_V_W
COL_RK = COL_RQ + RET_QK_W
COL_SQ = COL_RK + RET_QK_W
COL_SK = COL_SQ + SWA_Q_W
COL_SV = COL_SK + SWA_KV_W
IN_WIDTH = COL_SV + SWA_KV_W

V7X_VMEM_LIMIT_BYTES = 56 * 1024 * 1024
BF16_SUBLANES = 16
MXU_COLS = 256
NORM_ROWS = 32
NORM_UNROLL = 2

INPROJ_TM = 1024
INPROJ_TN = 512
MIX_TM = 512
MIX_TN = 512
FFN_TM = 512
FFN_TN = 512
FFN_HALO = BF16_SUBLANES

f32 = jnp.float32
bf16 = jnp.bfloat16


def _params(semantics):
    return pltpu.CompilerParams(dimension_semantics=semantics, vmem_limit_bytes=V7X_VMEM_LIMIT_BYTES)


def _rmsnorm(x, g):
    ms = jnp.mean(x * x, axis=-1, keepdims=True)
    return x * lax.rsqrt(ms + RMS_EPS) * g


def _norm_rows_to(x_ref, g_ref, dst_ref, dst_off, rows):
    g = g_ref[...]

    def step(c, carry):
        r = pl.multiple_of(c * NORM_ROWS, NORM_ROWS)
        dst_ref[pl.ds(dst_off + r, NORM_ROWS), :] = _rmsnorm(x_ref[pl.ds(r, NORM_ROWS), :], g).astype(dst_ref.dtype)
        return carry

    lax.fori_loop(0, rows // NORM_ROWS, step, 0, unroll=NORM_UNROLL)


def _residual_norm_rows(x_ref, y_ref, g_ref, o_ref, rows):
    g = g_ref[...]

    def step(c, carry):
        r = pl.ds(pl.multiple_of(c * NORM_ROWS, NORM_ROWS), NORM_ROWS)
        o_ref[r, :] = x_ref[r, :] + _rmsnorm(y_ref[r, :], g)
        return carry

    lax.fori_loop(0, rows // NORM_ROWS, step, 0, unroll=NORM_UNROLL)


_T_SIG_END = COL_RV // INPROJ_TN
_T_RV_END = COL_RG // INPROJ_TN
_T_RG_END = COL_RQ // INPROJ_TN
_T_RQ_END = COL_RK // INPROJ_TN
_T_RK_END = COL_SQ // INPROJ_TN


def _inproj_kernel(x_ref, g_ref, w_ref, rot_ref, o_ref, h_scr):
    j = pl.program_id(1)
    tm = x_ref.shape[0]

    @pl.when(j == 0)
    def _():
        _norm_rows_to(x_ref, g_ref, h_scr, 0, tm)

    def rotary(acc):
        cos = rot_ref[:, 0:RET_DK]
        sin = rot_ref[:, RET_DK:2 * RET_DK]
        heads = []
        for hh in range(acc.shape[1] // RET_DK):
            xh = acc[:, hh * RET_DK:(hh + 1) * RET_DK]
            heads.append(xh * cos + pltpu.roll(xh, RET_DK // 2, axis=1) * sin)
        return jnp.concatenate(heads, axis=1)

    def tile(epilogue):
        h = h_scr[...]
        for c in range(INPROJ_TN // MXU_COLS):
            cs = slice(c * MXU_COLS, (c + 1) * MXU_COLS)
            acc = jnp.dot(h, w_ref[:, cs], preferred_element_type=f32)
            o_ref[:, cs] = epilogue(acc).astype(o_ref.dtype)

    @pl.when(j < _T_SIG_END)
    def _():
        tile(jax.nn.sigmoid)

    @pl.when(((j >= _T_SIG_END) & (j < _T_RV_END)) | (j >= _T_RK_END))
    def _():
        tile(lambda acc: acc)

    @pl.when((j >= _T_RV_END) & (j < _T_RG_END))
    def _():
        tile(lambda acc: acc * jax.nn.sigmoid(acc))

    @pl.when((j >= _T_RG_END) & (j < _T_RK_END))
    def _():
        tile(rotary)


def _inproj(x2, g, w_in_r, rot, seq):
    T = x2.shape[0]
    tm, tn = INPROJ_TM, INPROJ_TN
    tm = min(tm, seq)
    spt = seq // tm
    grid = (T // tm, IN_WIDTH // tn)
    return pl.pallas_call(
        _inproj_kernel,
        grid=grid,
        in_specs=[
            pl.BlockSpec((tm, D_MODEL), lambda i, j: (i, 0)),
            pl.BlockSpec((1, D_MODEL), lambda i, j: (0, 0)),
            pl.BlockSpec((D_MODEL, tn), lambda i, j: (0, j)),
            pl.BlockSpec((None, tm, 2 * RET_DK), lambda i, j: (jnp.where(j >= _T_RQ_END, 1, 0), i % spt, 0)),
        ],
        out_specs=pl.BlockSpec((tm, tn), lambda i, j: (i, j)),
        out_shape=jax.ShapeDtypeStruct((T, IN_WIDTH), bf16),
        scratch_shapes=[pltpu.VMEM((tm, D_MODEL), bf16)],
        compiler_params=_params(("parallel", "arbitrary")),
        name="inproj",
    )(x2, g, w_in_r, rot)


def _retention_kernel(q_ref, k_ref, v_ref, gate_ref, dmat_ref, xi_ref, zeta_ref, cdec_ref, o_ref, kv_scr, st_scr):
    S = q_ref.shape[0]
    C = RET_CHUNK
    N = S // C
    q3 = q_ref[...].reshape(N, C, RET_DK)
    k3 = k_ref[...].reshape(N, C, RET_DK)
    v3 = v_ref[...].reshape(N, C, RET_DV)

    kz = (k3.astype(f32) * zeta_ref[...][None]).astype(bf16)
    kzt = jnp.swapaxes(kz, 1, 2)
    kv_scr[...] = jnp.einsum("ndk,nkv->ndv", kzt, v3, preferred_element_type=f32)

    cdec = cdec_ref[...]

    def scan(n, state):
        st_scr[n] = state.astype(bf16)
        return state * cdec + kv_scr[n]

    lax.fori_loop(0, N, scan, jnp.zeros((RET_DK, RET_DV), f32))

    sc = jnp.einsum("nqd,nkd->nqk", q3, k3, preferred_element_type=f32) * dmat_ref[...][None]
    inner = jnp.einsum("nqk,nkv->nqv", sc.astype(bf16), v3, preferred_element_type=f32)
    cross = jnp.einsum("nqd,ndv->nqv", q3, st_scr[...], preferred_element_type=f32) * xi_ref[...][None]
    o = inner + cross
    o = o * lax.rsqrt(jnp.mean(o * o, axis=-1, keepdims=True) + RMS_EPS)
    o = gate_ref[...].astype(f32).reshape(N, C, RET_DV) * o
    o_ref[...] = o.reshape(S, RET_DV).astype(o_ref.dtype)


def _retention_tables():
    C = RET_CHUNK
    log_gamma = jnp.log(1.0 - 2.0 ** (-5.0 - jnp.arange(RET_HEADS, dtype=f32)))
    idx = jnp.arange(C, dtype=f32)
    rel = idx[:, None] - idx[None, :]
    dmat = jnp.where(rel[None] >= 0, jnp.exp(log_gamma[:, None, None] * jnp.maximum(rel, 0.0)[None]), 0.0)
    xi = jnp.exp(log_gamma[:, None] * (idx + 1.0))
    zeta = jnp.exp(log_gamma[:, None] * (C - 1.0 - idx))
    cdec = jnp.exp(log_gamma * C)
    xi_b = jnp.broadcast_to(xi[:, :, None], (RET_HEADS, C, RET_DV))
    zeta_b = jnp.broadcast_to(zeta[:, :, None], (RET_HEADS, C, RET_DK))
    cdec_b = jnp.broadcast_to(cdec[:, None, None], (RET_HEADS, 1, RET_DV))
    return dmat.astype(f32), xi_b.astype(f32), zeta_b.astype(f32), cdec_b.astype(f32)


def _retention(proj, batch, seq):
    T = proj.shape[0]
    C = RET_CHUNK
    N = seq // C
    dmat, xi_b, zeta_b, cdec_b = _retention_tables()
    return pl.pallas_call(
        _retention_kernel,
        grid=(batch, RET_HEADS),
        in_specs=[
            pl.BlockSpec((seq, RET_DK), lambda b, h: (b, COL_RQ // RET_DK + h)),
            pl.BlockSpec((seq, RET_DK), lambda b, h: (b, COL_RK // RET_DK + h)),
            pl.BlockSpec((seq, RET_DV), lambda b, h: (b, COL_RV // RET_DV + h)),
            pl.BlockSpec((seq, RET_DV), lambda b, h: (b, COL_RG // RET_DV + h)),
            pl.BlockSpec((None, C, C), lambda b, h: (h, 0, 0)),
            pl.BlockSpec((None, C, RET_DV), lambda b, h: (h, 0, 0)),
            pl.BlockSpec((None, C, RET_DK), lambda b, h: (h, 0, 0)),
            pl.BlockSpec((None, 1, RET_DV), lambda b, h: (h, 0, 0)),
        ],
        out_specs=pl.BlockSpec((seq, RET_DV), lambda b, h: (b, h)),
        out_shape=jax.ShapeDtypeStruct((T, RET_V_W), bf16),
        scratch_shapes=[pltpu.VMEM((N, RET_DK, RET_DV), f32), pltpu.VMEM((N, RET_DK, RET_DV), bf16)],
        compiler_params=_params(("parallel", "parallel")),
        name="retention",
    )(proj, proj, proj, proj, dmat, xi_b, zeta_b, cdec_b)


KV_PER_STEP = 2
Q_PER_KV = SWA_Q_HEADS // SWA_KV_HEADS
SWA_Q_STEP_W = KV_PER_STEP * Q_PER_KV * SWA_HEAD_DIM
LANES = 128
SWA_UNROLL = 4


def _swa_kernel(sink_ref, bias_ref, q_ref, k_ref, v_ref, o_ref, klo, khi, vlo, vhi):
    S = q_ref.shape[0]
    C = SWA_BLOCK
    gp = pl.program_id(1)
    half = SWA_HEAD_DIM

    lane_s = lax.broadcasted_iota(jnp.int32, (S, LANES), 1)
    low_s = lane_s < half

    def prep(src_ref, lo_scr, hi_scr, scale, with_ones):
        w = src_ref[...].astype(f32) * scale
        r = pltpu.roll(w, half, axis=1)
        zeros = jnp.zeros((C, lo_scr.shape[2]), bf16)
        for t in range(KV_PER_STEP):
            lo_scr[t, 0:C, :] = zeros
            hi_scr[t, 0:C, :] = zeros
        lo_scr[0, C:C + S, 0:LANES] = jnp.where(low_s, w, 0.0).astype(bf16)
        hi_scr[0, C:C + S, 0:LANES] = jnp.where(low_s, 0.0, r).astype(bf16)
        lo_scr[1, C:C + S, 0:LANES] = jnp.where(low_s, r, 0.0).astype(bf16)
        hi_scr[1, C:C + S, 0:LANES] = jnp.where(low_s, 0.0, w).astype(bf16)
        if with_ones:
            for t in range(KV_PER_STEP):
                lo_scr[t, C:C + S, LANES:2 * LANES] = jnp.where(low_s, 1.0, 0.0).astype(bf16)
                hi_scr[t, C:C + S, LANES:2 * LANES] = jnp.where(low_s, 0.0, 1.0).astype(bf16)

    prep(k_ref, klo, khi, SWA_HEAD_DIM ** -0.5, False)
    prep(v_ref, vlo, vhi, 1.0, True)

    rows = 2 * C
    win = 2 * C
    first_pair = lax.broadcasted_iota(jnp.int32, (rows, 1), 0) < C
    low_o = lax.broadcasted_iota(jnp.int32, (rows, LANES), 1) < half
    nt = (((1,), (1,)), ((), ()))

    def softmax_parts(s, sink, bias):
        s = s + bias
        m = jnp.maximum(jnp.max(s, axis=-1, keepdims=True), sink)
        return jnp.exp(s - m).astype(bf16), jnp.exp(sink - m)

    def block(n, carry):
        r0 = pl.multiple_of(n * C, C)
        bias = bias_ref[jnp.minimum(n, 1)]
        for t in range(KV_PER_STEP):
            c0 = t * Q_PER_KV * SWA_HEAD_DIM
            hbase = gp * (KV_PER_STEP * Q_PER_KV) + t * Q_PER_KV
            qs = jnp.concatenate([q_ref[pl.ds(r0, C), c0:c0 + LANES],
                                  q_ref[pl.ds(r0, C), c0 + LANES:c0 + 2 * LANES]], axis=0)
            s_e = lax.dot_general(qs, klo[t, pl.ds(r0, win), :], nt, preferred_element_type=f32)
            s_o = lax.dot_general(qs, khi[t, pl.ds(r0, win), :], nt, preferred_element_type=f32)
            sink_e = jnp.where(first_pair, sink_ref[hbase + 0], sink_ref[hbase + 2])
            sink_o = jnp.where(first_pair, sink_ref[hbase + 1], sink_ref[hbase + 3])
            p_e, z_e = softmax_parts(s_e, sink_e, bias)
            p_o, z_o = softmax_parts(s_o, sink_o, bias)
            pv = (jnp.dot(p_e, vlo[t, pl.ds(r0, win), :], preferred_element_type=f32)
                  + jnp.dot(p_o, vhi[t, pl.ds(r0, win), :], preferred_element_type=f32))
            den = pv[:, LANES:2 * LANES] + jnp.where(low_o, z_e, z_o)
            o = pv[:, 0:LANES] / den
            o_ref[pl.ds(r0, C), c0:c0 + LANES] = o[0:C].astype(o_ref.dtype)
            o_ref[pl.ds(r0, C), c0 + LANES:c0 + 2 * LANES] = o[C:2 * C].astype(o_ref.dtype)
        return carry

    lax.fori_loop(0, S // C, block, 0, unroll=SWA_UNROLL)


def _swa_bias():
    C = SWA_BLOCK
    row_i = jnp.arange(2 * C)[:, None] % C
    col_j = jnp.arange(2 * C)[None, :]
    band = (col_j > row_i) & (col_j <= row_i + C)
    first = band & (col_j >= C)
    return jnp.where(jnp.stack([first, band]), 0.0, -jnp.inf).astype(f32)


def _swa(proj, sinks, batch, seq):
    T = proj.shape[0]
    steps = SWA_KV_HEADS // KV_PER_STEP
    k_scr = pltpu.VMEM((KV_PER_STEP, seq + SWA_BLOCK, LANES), bf16)
    v_scr = pltpu.VMEM((KV_PER_STEP, seq + SWA_BLOCK, 2 * LANES), bf16)
    bias = _swa_bias()
    return pl.pallas_call(
        _swa_kernel,
        grid=(batch, steps),
        in_specs=[
            pl.BlockSpec(memory_space=pltpu.SMEM),
            pl.BlockSpec(bias.shape, lambda b, g: (0, 0, 0)),
            pl.BlockSpec((seq, SWA_Q_STEP_W), lambda b, g: (b, COL_SQ // SWA_Q_STEP_W + g)),
            pl.BlockSpec((seq, LANES), lambda b, g: (b, COL_SK // LANES + g)),
            pl.BlockSpec((seq, LANES), lambda b, g: (b, COL_SV // LANES + g)),
        ],
        out_specs=pl.BlockSpec((seq, SWA_Q_STEP_W), lambda b, g: (b, g)),
        out_shape=jax.ShapeDtypeStruct((T, SWA_Q_W), bf16),
        scratch_shapes=[k_scr, k_scr, v_scr, v_scr],
        compiler_params=_params(("parallel", "parallel")),
        name="swa",
    )(sinks, bias, proj, proj, proj)


def _mix_kernel(ret_ref, swa_ref, gr_ref, gs_ref, wr_ref, ws_ref, wo_ref, x_ref, g_ref, o_ref, acc_scr):
    j = pl.program_id(1)

    @pl.when(j == 0)
    def _():
        acc_scr[...] = jnp.zeros_like(acc_scr)

    ret_out = jnp.dot(ret_ref[...], wr_ref[...], preferred_element_type=f32)
    swa_out = jnp.dot(swa_ref[...], ws_ref[...], preferred_element_type=f32)
    mixed = gr_ref[...].astype(f32) * ret_out + gs_ref[...].astype(f32) * swa_out
    acc_scr[...] += jnp.dot(mixed.astype(bf16), wo_ref[...], preferred_element_type=f32)

    @pl.when(j == pl.num_programs(1) - 1)
    def _():
        _residual_norm_rows(x_ref, acc_scr, g_ref, o_ref, o_ref.shape[0])


def _mix(ret, swa, proj, w_ret_o, w_swa_o, w_out, x2, g):
    T = x2.shape[0]
    tm, tn = min(MIX_TM, T), MIX_TN
    nj = D_MODEL // tn
    return pl.pallas_call(
        _mix_kernel,
        grid=(T // tm, nj),
        in_specs=[
            pl.BlockSpec((tm, RET_V_W), lambda i, j: (i, 0)),
            pl.BlockSpec((tm, SWA_Q_W), lambda i, j: (i, 0)),
            pl.BlockSpec((tm, tn), lambda i, j: (i, COL_GR // tn + j)),
            pl.BlockSpec((tm, tn), lambda i, j: (i, COL_GS // tn + j)),
            pl.BlockSpec((RET_V_W, tn), lambda i, j: (0, j)),
            pl.BlockSpec((SWA_Q_W, tn), lambda i, j: (0, j)),
            pl.BlockSpec((tn, D_MODEL), lambda i, j: (j, 0)),
            pl.BlockSpec((tm, D_MODEL), lambda i, j: (i, 0)),
            pl.BlockSpec((1, D_MODEL), lambda i, j: (0, 0)),
        ],
        out_specs=pl.BlockSpec((tm, D_MODEL), lambda i, j: (i, 0)),
        out_shape=jax.ShapeDtypeStruct((T, D_MODEL), f32),
        scratch_shapes=[pltpu.VMEM((tm, D_MODEL), f32)],
        compiler_params=_params(("parallel", "arbitrary")),
        name="mix",
    )(ret, swa, proj, proj, w_ret_o, w_swa_o, w_out, x2, g)


def _gelu_tanh(x):
    return x * (0.5 * (1.0 + jnp.tanh(0.7978845608028654 * (x + 0.044715 * (x * x * x)))))


def _ffn_kernel(x_ref, halo_ref, gpre_ref, wv_ref, wg_ref, cwv_ref, cwg_ref, cbv_ref, cbg_ref, wd_ref, gpost_ref,
                o_ref, h_scr, u_scr, acc_scr, *, tiles_per_seq):
    i = pl.program_id(0)
    j = pl.program_id(1)
    tm = x_ref.shape[0]
    tn = wv_ref.shape[1]
    H = FFN_HALO

    @pl.when(j == 0)
    def _():
        keep = jnp.where(i % tiles_per_seq == 0, 0.0, 1.0)
        h_scr[0:H, :] = (_rmsnorm(halo_ref[...], gpre_ref[...]) * keep).astype(bf16)
        _norm_rows_to(x_ref, gpre_ref, h_scr, H, tm)
        acc_scr[...] = jnp.zeros_like(acc_scr)

    def conv(u_cols, w_cols, cw_ref, cb_ref):
        y = cb_ref[:, w_cols]
        for kk in range(CONV_WIDTH):
            y = y + u_scr[pl.ds(H - (CONV_WIDTH - 1) + kk, tm), u_cols] * cw_ref[kk:kk + 1, w_cols]
        return y

    h = h_scr[...]
    part = None
    for c in range(tn // MXU_COLS):
        cs = slice(c * MXU_COLS, (c + 1) * MXU_COLS)
        uv = slice(2 * c * MXU_COLS, (2 * c + 1) * MXU_COLS)
        ug = slice((2 * c + 1) * MXU_COLS, (2 * c + 2) * MXU_COLS)
        u_scr[:, uv] = jnp.dot(h, wv_ref[:, cs], preferred_element_type=f32)
        u_scr[:, ug] = jnp.dot(h, wg_ref[:, cs], preferred_element_type=f32)
        val = conv(uv, cs, cwv_ref, cbv_ref)
        gate = conv(ug, cs, cwg_ref, cbg_ref)
        a = (_gelu_tanh(gate) * val).astype(bf16)
        d = jnp.dot(a, wd_ref[cs, :], preferred_element_type=f32)
        part = d if part is None else part + d
    acc_scr[...] += part

    @pl.when(j == pl.num_programs(1) - 1)
    def _():
        _residual_norm_rows(x_ref, acc_scr, gpost_ref, o_ref, tm)


def _ffn(x1, g_pre, w_up, conv_w, conv_b, w_down, g_post, seq):
    T = x1.shape[0]
    tm, tn = min(FFN_TM, seq), FFN_TN
    nj = D_FF // tn
    hb = tm // FFN_HALO
    kern = functools.partial(_ffn_kernel, tiles_per_seq=seq // tm)
    return pl.pallas_call(
        kern,
        grid=(T // tm, nj),
        in_specs=[
            pl.BlockSpec((tm, D_MODEL), lambda i, j: (i, 0)),
            pl.BlockSpec((FFN_HALO, D_MODEL), lambda i, j: (jnp.maximum(i * hb - 1, 0), 0)),
            pl.BlockSpec((1, D_MODEL), lambda i, j: (0, 0)),
            pl.BlockSpec((D_MODEL, tn), lambda i, j: (0, j)),
            pl.BlockSpec((D_MODEL, tn), lambda i, j: (0, nj + j)),
            pl.BlockSpec((CONV_WIDTH, tn), lambda i, j: (0, j)),
            pl.BlockSpec((CONV_WIDTH, tn), lambda i, j: (0, nj + j)),
            pl.BlockSpec((1, tn), lambda i, j: (0, j)),
            pl.BlockSpec((1, tn), lambda i, j: (0, nj + j)),
            pl.BlockSpec((tn, D_MODEL), lambda i, j: (j, 0)),
            pl.BlockSpec((1, D_MODEL), lambda i, j: (0, 0)),
        ],
        out_specs=pl.BlockSpec((tm, D_MODEL), lambda i, j: (i, 0)),
        out_shape=jax.ShapeDtypeStruct((T, D_MODEL), f32),
        scratch_shapes=[
            pltpu.VMEM((tm + FFN_HALO, D_MODEL), bf16),
            pltpu.VMEM((tm + FFN_HALO, 2 * tn), f32),
            pltpu.VMEM((tm, D_MODEL), f32),
        ],
        compiler_params=_params(("parallel", "arbitrary")),
        name="ffn",
    )(x1, x1, g_pre, w_up, w_up, conv_w, conv_w, conv_b, conv_b, w_down, g_post)


def _rotary_tables(seq):
    d = RET_DK
    inv = 1.0 / (ROPE_BASE ** (jnp.arange(0, d, 2, dtype=f32) / d))
    ang = jnp.arange(seq, dtype=jnp.int32).astype(f32)[:, None] * inv[None, :]
    cos, sin = jnp.cos(ang), jnp.sin(ang)
    tab = jnp.concatenate([cos, cos, -sin, sin], axis=-1)
    return jnp.stack([tab, tab * (d ** -0.5)])


def _reorder_in_weight(w):
    widths = (RET_QK_W, RET_QK_W, RET_V_W, RET_V_W, SWA_Q_W, SWA_KV_W, SWA_KV_W, D_MODEL, D_MODEL)
    offs = [0]
    for wd in widths:
        offs.append(offs[-1] + wd)
    rq, rk, rv, rg, sq, sk, sv, gr, gs = [w[:, offs[t]:offs[t + 1]] for t in range(len(widths))]
    return jnp.concatenate([gr, gs, rv, rg, rq, rk, sq, sk, sv], axis=1).astype(bf16)


def kernel(x, g_pre_mix, w_in, w_ret_o, w_swa_o, w_out, swa_sinks, g_post_mix, g_pre_ffn, w_up, conv_w, conv_b,
           w_down, g_post_ffn):
    B, S, D = x.shape
    depth = w_in.shape[0]
    rot = _rotary_tables(S)
    x2 = x.reshape(B * S, D)
    for l in range(depth):
        proj = _inproj(x2, g_pre_mix[l][None], _reorder_in_weight(w_in[l]), rot, S)
        ret = _retention(proj, B, S)
        swa = _swa(proj, swa_sinks[l].astype(f32), B, S)
        x2 = _mix(ret, swa, proj, w_ret_o[l].astype(bf16), w_swa_o[l].astype(bf16), w_out[l].astype(bf16), x2,
                  g_post_mix[l][None])
        x2 = _ffn(x2, g_pre_ffn[l][None], w_up[l].astype(bf16), conv_w[l], conv_b[l][None], w_down[l].astype(bf16),
                  g_post_ffn[l][None], S)
    return x2.reshape(B, S, D)
```

```python
import functools

import jax
import jax.numpy as jnp
from jax import lax
from jax.experimental import pallas as pl
from jax.experimental.pallas import tpu as pltpu

D_MODEL = 2048
RET_HEADS = 8
RET_DK = 128
RET_DV = 256
RET_CHUNK = 128
ROPE_BASE = 10000.0
SWA_Q_HEADS = 16
SWA_KV_HEADS = 4
SWA_HEAD_DIM = 64
SWA_BLOCK = 128
D_FF = 5632
CONV_WIDTH = 3
RMS_EPS = 1e-6

RET_QK_W = RET_HEADS * RET_DK
RET_V_W = RET_HEADS * RET_DV
SWA_Q_W = SWA_Q_HEADS * SWA_HEAD_DIM
SWA_KV_W = SWA_KV_HEADS * SWA_HEAD_DIM

COL_RQ = 0
COL_RK = COL_RQ + RET_QK_W
COL_RV = COL_RK + RET_QK_W
COL_RG = COL_RV + RET_V_W
COL_SQ = COL_RG + RET_V_W
COL_SK = COL_SQ + SWA_Q_W
COL_SV = COL_SK + SWA_KV_W
COL_GR = COL_SV + SWA_KV_W
COL_GS = COL_GR + D_MODEL
IN_WIDTH = COL_GS + D_MODEL

V7X_VMEM_LIMIT_BYTES = 60 * 1024 * 1024
BF16_SUBLANES = 16
MXU_COLS = 256
NORM_ROWS = 32
NORM_UNROLL = 4

INPROJ_TM = 1024
INPROJ_TN = 512
MIX_TM = 512
MIX_TN = 512
FFN_TM = 1024
FFN_TN = 512
FFN_HALO = BF16_SUBLANES
FFN_ROW_SPLIT = 2
INPROJ_ROW_SPLIT = 2

f32 = jnp.float32
bf16 = jnp.bfloat16

def _params(semantics):
    return pltpu.CompilerParams(dimension_semantics=semantics, vmem_limit_bytes=V7X_VMEM_LIMIT_BYTES)


def _rmsnorm(x, g):
    ms = jnp.mean(x * x, axis=-1, keepdims=True)
    return x * lax.rsqrt(ms + RMS_EPS) * g


def _norm_rows_to(x_ref, g_ref, dst_ref, dst_off, rows):
    g = g_ref[...]
    group = NORM_ROWS * NORM_UNROLL

    def step(c, carry):
        base = pl.multiple_of(c * group, group)
        for k in range(NORM_UNROLL):
            r = base + k * NORM_ROWS
            dst_ref[pl.ds(dst_off + r, NORM_ROWS), :] = (
                _rmsnorm(x_ref[pl.ds(r, NORM_ROWS), :], g).astype(dst_ref.dtype))
        return carry

    lax.fori_loop(0, rows // group, step, 0)


def _residual_norm_rows(x_ref, y_ref, g_ref, o_ref, rows):
    g = g_ref[...]
    group = NORM_ROWS * NORM_UNROLL

    def step(c, carry):
        base = pl.multiple_of(c * group, group)
        rows_k = [pl.ds(base + k * NORM_ROWS, NORM_ROWS) for k in range(NORM_UNROLL)]
        scales = []
        for r in rows_k:
            y = y_ref[r, :]
            scales.append(lax.rsqrt(jnp.mean(y * y, axis=-1, keepdims=True) + RMS_EPS))
        for r, s in zip(rows_k, scales):
            o_ref[r, :] = x_ref[r, :] + y_ref[r, :] * s * g
        return carry

    lax.fori_loop(0, rows // group, step, 0)


_T_RK = COL_RK // INPROJ_TN
_T_RV = COL_RV // INPROJ_TN
_T_RG = COL_RG // INPROJ_TN
_T_SQ = COL_SQ // INPROJ_TN
_T_GR = COL_GR // INPROJ_TN


def _inproj_kernel(x_ref, g_ref, w_ref, rot_ref, o_ref, h_scr):
    j = pl.program_id(1)
    tm = x_ref.shape[0]

    @pl.when(j == 0)
    def _():
        _norm_rows_to(x_ref, g_ref, h_scr, 0, tm)

    def rotary(acc, rows):
        cos = rot_ref[rows, 0:RET_DK]
        sin = rot_ref[rows, RET_DK:2 * RET_DK]
        heads = []
        for hh in range(acc.shape[1] // RET_DK):
            xh = acc[:, hh * RET_DK:(hh + 1) * RET_DK]
            heads.append(xh * cos + pltpu.roll(xh, RET_DK // 2, axis=1) * sin)
        return jnp.concatenate(heads, axis=1)

    def tile(epilogue):
        hm = tm // INPROJ_ROW_SPLIT
        for c in range(INPROJ_TN // MXU_COLS):
            cs = slice(c * MXU_COLS, (c + 1) * MXU_COLS)
            w = w_ref[:, cs].astype(bf16)
            for m in range(INPROJ_ROW_SPLIT):
                rows = slice(m * hm, (m + 1) * hm)
                acc = jnp.dot(h_scr[rows, :], w, preferred_element_type=f32)
                o_ref[rows, cs] = epilogue(acc, rows).astype(o_ref.dtype)

    @pl.when(j < _T_RV)
    def _():
        tile(rotary)

    @pl.when(((j >= _T_RV) & (j < _T_RG)) | ((j >= _T_SQ) & (j < _T_GR)))
    def _():
        tile(lambda acc, rows: acc)

    @pl.when((j >= _T_RG) & (j < _T_SQ))
    def _():
        tile(lambda acc, rows: acc * jax.nn.sigmoid(acc))

    @pl.when(j >= _T_GR)
    def _():
        tile(lambda acc, rows: jax.nn.sigmoid(acc))


def _inproj(x2, g, w_in, rot, seq):
    T = x2.shape[0]
    tm, tn = INPROJ_TM, INPROJ_TN
    tm = min(tm, seq)
    spt = seq // tm
    grid = (T // tm, IN_WIDTH // tn)
    return pl.pallas_call(
        _inproj_kernel,
        grid=grid,
        in_specs=[
            pl.BlockSpec((tm, D_MODEL), lambda i, j: (i, 0)),
            pl.BlockSpec((1, D_MODEL), lambda i, j: (0, 0)),
            pl.BlockSpec((D_MODEL, tn), lambda i, j: (0, j)),
            pl.BlockSpec((None, tm, 2 * RET_DK), lambda i, j: (jnp.where(j >= _T_RK, 1, 0), i % spt, 0)),
        ],
        out_specs=pl.BlockSpec((tm, tn), lambda i, j: (i, j)),
        out_shape=jax.ShapeDtypeStruct((T, IN_WIDTH), bf16),
        scratch_shapes=[pltpu.VMEM((tm, D_MODEL), bf16)],
        compiler_params=_params(("parallel", "arbitrary")),
        name="inproj",
    )(x2, g, w_in, rot)


def _retention_kernel(q_ref, k_ref, v_ref, gate_ref, dmat_ref, xi_ref, zeta_ref, cdec_ref, o_ref, kv_scr, st_scr):
    S = q_ref.shape[0]
    C = RET_CHUNK
    N = S // C
    q3 = q_ref[...].reshape(N, C, RET_DK)
    k3 = k_ref[...].reshape(N, C, RET_DK)
    v3 = v_ref[...].reshape(N, C, RET_DV)

    kz = (k3.astype(f32) * zeta_ref[...][None]).astype(bf16)
    kzt = jnp.swapaxes(kz, 1, 2)
    kv_scr[...] = jnp.einsum("ndk,nkv->ndv", kzt, v3, preferred_element_type=f32)

    cdec = cdec_ref[...]

    def scan(n, state):
        st_scr[n] = state.astype(bf16)
        return state * cdec + kv_scr[n]

    lax.fori_loop(0, N, scan, jnp.zeros((RET_DK, RET_DV), f32))

    sc = jnp.einsum("nqd,nkd->nqk", q3, k3, preferred_element_type=f32) * dmat_ref[...][None]
    inner = jnp.einsum("nqk,nkv->nqv", sc.astype(bf16), v3, preferred_element_type=f32)
    cross = jnp.einsum("nqd,ndv->nqv", q3, st_scr[...], preferred_element_type=f32) * xi_ref[...][None]
    o = inner + cross
    o = o * lax.rsqrt(jnp.mean(o * o, axis=-1, keepdims=True) + RMS_EPS)
    o = gate_ref[...].astype(f32).reshape(N, C, RET_DV) * o
    o_ref[...] = o.reshape(S, RET_DV).astype(o_ref.dtype)


def _retention_tables():
    C = RET_CHUNK
    log_gamma = jnp.log(1.0 - 2.0 ** (-5.0 - jnp.arange(RET_HEADS, dtype=f32)))
    idx = jnp.arange(C, dtype=f32)
    rel = idx[:, None] - idx[None, :]
    dmat = jnp.where(rel[None] >= 0, jnp.exp(log_gamma[:, None, None] * jnp.maximum(rel, 0.0)[None]), 0.0)
    xi = jnp.exp(log_gamma[:, None] * (idx + 1.0))
    zeta = jnp.exp(log_gamma[:, None] * (C - 1.0 - idx))
    cdec = jnp.exp(log_gamma * C)
    xi_b = jnp.broadcast_to(xi[:, :, None], (RET_HEADS, C, RET_DV))
    zeta_b = jnp.broadcast_to(zeta[:, :, None], (RET_HEADS, C, RET_DK))
    cdec_b = jnp.broadcast_to(cdec[:, None, None], (RET_HEADS, 1, RET_DV))
    return dmat.astype(f32), xi_b.astype(f32), zeta_b.astype(f32), cdec_b.astype(f32)


def _retention(proj, batch, seq):
    T = proj.shape[0]
    C = RET_CHUNK
    N = seq // C
    dmat, xi_b, zeta_b, cdec_b = _retention_tables()
    return pl.pallas_call(
        _retention_kernel,
        grid=(batch, RET_HEADS),
        in_specs=[
            pl.BlockSpec((seq, RET_DK), lambda b, h: (b, COL_RQ // RET_DK + h)),
            pl.BlockSpec((seq, RET_DK), lambda b, h: (b, COL_RK // RET_DK + h)),
            pl.BlockSpec((seq, RET_DV), lambda b, h: (b, COL_RV // RET_DV + h)),
            pl.BlockSpec((seq, RET_DV), lambda b, h: (b, COL_RG // RET_DV + h)),
            pl.BlockSpec((None, C, C), lambda b, h: (h, 0, 0)),
            pl.BlockSpec((None, C, RET_DV), lambda b, h: (h, 0, 0)),
            pl.BlockSpec((None, C, RET_DK), lambda b, h: (h, 0, 0)),
            pl.BlockSpec((None, 1, RET_DV), lambda b, h: (h, 0, 0)),
        ],
        out_specs=pl.BlockSpec((seq, RET_DV), lambda b, h: (b, h)),
        out_shape=jax.ShapeDtypeStruct((T, RET_V_W), bf16),
        scratch_shapes=[pltpu.VMEM((N, RET_DK, RET_DV), f32), pltpu.VMEM((N, RET_DK, RET_DV), bf16)],
        compiler_params=_params(("parallel", "parallel")),
        name="retention",
    )(proj, proj, proj, proj, dmat, xi_b, zeta_b, cdec_b)


KV_PER_STEP = 2
Q_PER_KV = SWA_Q_HEADS // SWA_KV_HEADS
SWA_Q_STEP_W = KV_PER_STEP * Q_PER_KV * SWA_HEAD_DIM
LANES = 128
SWA_UNROLL = 4


def _swa_kernel(sink_ref, bias_ref, q_ref, k_ref, v_ref, o_ref, klo, khi, vlo, vhi):
    S = q_ref.shape[0]
    C = SWA_BLOCK
    gp = pl.program_id(1)
    half = SWA_HEAD_DIM

    lane_s = lax.broadcasted_iota(jnp.int32, (S, LANES), 1)
    low_s = lane_s < half

    def prep(src_ref, lo_scr, hi_scr, scale, with_ones):
        w = src_ref[...].astype(f32) * scale
        r = pltpu.roll(w, half, axis=1)
        zeros = jnp.zeros((C, lo_scr.shape[2]), bf16)
        for t in range(KV_PER_STEP):
            lo_scr[t, 0:C, :] = zeros
            hi_scr[t, 0:C, :] = zeros
        lo_scr[0, C:C + S, 0:LANES] = jnp.where(low_s, w, 0.0).astype(bf16)
        hi_scr[0, C:C + S, 0:LANES] = jnp.where(low_s, 0.0, r).astype(bf16)
        lo_scr[1, C:C + S, 0:LANES] = jnp.where(low_s, r, 0.0).astype(bf16)
        hi_scr[1, C:C + S, 0:LANES] = jnp.where(low_s, 0.0, w).astype(bf16)
        if with_ones:
            for t in range(KV_PER_STEP):
                lo_scr[t, C:C + S, LANES:2 * LANES] = jnp.where(low_s, 1.0, 0.0).astype(bf16)
                hi_scr[t, C:C + S, LANES:2 * LANES] = jnp.where(low_s, 0.0, 1.0).astype(bf16)

    prep(k_ref, klo, khi, SWA_HEAD_DIM ** -0.5, False)
    prep(v_ref, vlo, vhi, 1.0, True)

    rows = 2 * C
    win = 2 * C
    first_pair = lax.broadcasted_iota(jnp.int32, (rows, 1), 0) < C
    low_o = lax.broadcasted_iota(jnp.int32, (rows, LANES), 1) < half
    nt = (((1,), (1,)), ((), ()))

    def softmax_parts(s, sink, bias):
        s = s + bias
        m = jnp.maximum(jnp.max(s, axis=-1, keepdims=True), sink)
        return jnp.exp(s - m).astype(bf16), jnp.exp(sink - m)

    def block(n, carry):
        r0 = pl.multiple_of(n * C, C)
        bias = bias_ref[jnp.minimum(n, 1)]
        for t in range(KV_PER_STEP):
            c0 = t * Q_PER_KV * SWA_HEAD_DIM
            hbase = gp * (KV_PER_STEP * Q_PER_KV) + t * Q_PER_KV
            qs = jnp.concatenate([q_ref[pl.ds(r0, C), c0:c0 + LANES],
                                  q_ref[pl.ds(r0, C), c0 + LANES:c0 + 2 * LANES]], axis=0)
            s_e = lax.dot_general(qs, klo[t, pl.ds(r0, win), :], nt, preferred_element_type=f32)
            s_o = lax.dot_general(qs, khi[t, pl.ds(r0, win), :], nt, preferred_element_type=f32)
            sink_e = jnp.where(first_pair, sink_ref[hbase + 0], sink_ref[hbase + 2])
            sink_o = jnp.where(first_pair, sink_ref[hbase + 1], sink_ref[hbase + 3])
            p_e, z_e = softmax_parts(s_e, sink_e, bias)
            p_o, z_o = softmax_parts(s_o, sink_o, bias)
            pv = (jnp.dot(p_e, vlo[t, pl.ds(r0, win), :], preferred_element_type=f32)
                  + jnp.dot(p_o, vhi[t, pl.ds(r0, win), :], preferred_element_type=f32))
            den = pv[:, LANES:2 * LANES] + jnp.where(low_o, z_e, z_o)
            o = pv[:, 0:LANES] / den
            o_ref[pl.ds(r0, C), c0:c0 + LANES] = o[0:C].astype(o_ref.dtype)
            o_ref[pl.ds(r0, C), c0 + LANES:c0 + 2 * LANES] = o[C:2 * C].astype(o_ref.dtype)
        return carry

    lax.fori_loop(0, S // C, block, 0, unroll=SWA_UNROLL)


def _swa_bias():
    C = SWA_BLOCK
    row_i = jnp.arange(2 * C)[:, None] % C
    col_j = jnp.arange(2 * C)[None, :]
    band = (col_j > row_i) & (col_j <= row_i + C)
    first = band & (col_j >= C)
    return jnp.where(jnp.stack([first, band]), 0.0, -jnp.inf).astype(f32)


def _swa(proj, sinks, batch, seq):
    T = proj.shape[0]
    steps = SWA_KV_HEADS // KV_PER_STEP
    k_scr = pltpu.VMEM((KV_PER_STEP, seq + SWA_BLOCK, LANES), bf16)
    v_scr = pltpu.VMEM((KV_PER_STEP, seq + SWA_BLOCK, 2 * LANES), bf16)
    bias = _swa_bias()
    return pl.pallas_call(
        _swa_kernel,
        grid=(batch, steps),
        in_specs=[
            pl.BlockSpec(memory_space=pltpu.SMEM),
            pl.BlockSpec(bias.shape, lambda b, g: (0, 0, 0)),
            pl.BlockSpec((seq, SWA_Q_STEP_W), lambda b, g: (b, COL_SQ // SWA_Q_STEP_W + g)),
            pl.BlockSpec((seq, LANES), lambda b, g: (b, COL_SK // LANES + g)),
            pl.BlockSpec((seq, LANES), lambda b, g: (b, COL_SV // LANES + g)),
        ],
        out_specs=pl.BlockSpec((seq, SWA_Q_STEP_W), lambda b, g: (b, g)),
        out_shape=jax.ShapeDtypeStruct((T, SWA_Q_W), bf16),
        scratch_shapes=[k_scr, k_scr, v_scr, v_scr],
        compiler_params=_params(("parallel", "parallel")),
        name="swa",
    )(sinks, bias, proj, proj, proj)


def _mix_kernel(ret_ref, swa_ref, gr_ref, gs_ref, wr_ref, ws_ref, wo_ref, x_ref, g_ref, o_ref, acc_scr):
    j = pl.program_id(1)

    @pl.when(j == 0)
    def _():
        acc_scr[...] = jnp.zeros_like(acc_scr)

    ret_out = jnp.dot(ret_ref[...], wr_ref[...], preferred_element_type=f32)
    swa_out = jnp.dot(swa_ref[...], ws_ref[...], preferred_element_type=f32)
    mixed = gr_ref[...].astype(f32) * ret_out + gs_ref[...].astype(f32) * swa_out
    acc_scr[...] += jnp.dot(mixed.astype(bf16), wo_ref[...], preferred_element_type=f32)

    @pl.when(j == pl.num_programs(1) - 1)
    def _():
        _residual_norm_rows(x_ref, acc_scr, g_ref, o_ref, o_ref.shape[0])


def _mix(ret, swa, proj, w_ret_o, w_swa_o, w_out, x2, g):
    T = x2.shape[0]
    tm, tn = min(MIX_TM, T), MIX_TN
    nj = D_MODEL // tn
    return pl.pallas_call(
        _mix_kernel,
        grid=(T // tm, nj),
        in_specs=[
            pl.BlockSpec((tm, RET_V_W), lambda i, j: (i, 0)),
            pl.BlockSpec((tm, SWA_Q_W), lambda i, j: (i, 0)),
            pl.BlockSpec((tm, tn), lambda i, j: (i, COL_GR // tn + j)),
            pl.BlockSpec((tm, tn), lambda i, j: (i, COL_GS // tn + j)),
            pl.BlockSpec((RET_V_W, tn), lambda i, j: (0, j)),
            pl.BlockSpec((SWA_Q_W, tn), lambda i, j: (0, j)),
            pl.BlockSpec((tn, D_MODEL), lambda i, j: (j, 0)),
            pl.BlockSpec((tm, D_MODEL), lambda i, j: (i, 0)),
            pl.BlockSpec((1, D_MODEL), lambda i, j: (0, 0)),
        ],
        out_specs=pl.BlockSpec((tm, D_MODEL), lambda i, j: (i, 0)),
        out_shape=jax.ShapeDtypeStruct((T, D_MODEL), f32),
        scratch_shapes=[pltpu.VMEM((tm, D_MODEL), f32)],
        compiler_params=_params(("parallel", "arbitrary")),
        name="mix",
    )(ret, swa, proj, proj, w_ret_o, w_swa_o, w_out, x2, g)


def _gelu_tanh(x):
    return x * (0.5 * (1.0 + jnp.tanh(0.7978845608028654 * (x + 0.044715 * (x * x * x)))))


def _ffn_kernel(x_ref, halo_ref, gpre_ref, wv_ref, wg_ref, cwv_ref, cwg_ref, cbv_ref, cbg_ref, wd_ref, gpost_ref,
                o_ref, h_scr, *u_scrs, tiles_per_seq):
    i = pl.program_id(0)
    j = pl.program_id(1)
    tm = x_ref.shape[0]
    tn = wv_ref.shape[1]
    H = FFN_HALO

    @pl.when(j == 0)
    def _():
        keep = jnp.where(i % tiles_per_seq == 0, 0.0, 1.0)
        h_scr[0:H, :] = (_rmsnorm(halo_ref[...], gpre_ref[...]) * keep).astype(bf16)
        _norm_rows_to(x_ref, gpre_ref, h_scr, H, tm)
        o_ref[...] = jnp.zeros_like(o_ref)

    def conv(u_scr, row0, nrows, u_cols, w_cols, cw_ref, cb_ref):
        y = cb_ref[:, w_cols]
        for kk in range(CONV_WIDTH):
            y = y + u_scr[pl.ds(H - (CONV_WIDTH - 1) + kk + row0, nrows), u_cols] * cw_ref[kk:kk + 1, w_cols]
        return y

    h = h_scr[...]
    hm = tm // FFN_ROW_SPLIT
    parts = [None] * FFN_ROW_SPLIT
    uv = slice(0, MXU_COLS)
    ug = slice(MXU_COLS, 2 * MXU_COLS)
    for c, u_scr in enumerate(u_scrs):
        cs = slice(c * MXU_COLS, (c + 1) * MXU_COLS)
        u_scr[:, uv] = jnp.dot(h, wv_ref[:, cs], preferred_element_type=f32)
        u_scr[:, ug] = jnp.dot(h, wg_ref[:, cs], preferred_element_type=f32)
        for m in range(FFN_ROW_SPLIT):
            val = conv(u_scr, m * hm, hm, uv, cs, cwv_ref, cbv_ref)
            gate = conv(u_scr, m * hm, hm, ug, cs, cwg_ref, cbg_ref)
            a = (_gelu_tanh(gate) * val).astype(bf16)
            d = jnp.dot(a, wd_ref[cs, :], preferred_element_type=f32)
            parts[m] = d if parts[m] is None else parts[m] + d
    for m in range(FFN_ROW_SPLIT):
        o_ref[m * hm:(m + 1) * hm, :] += parts[m]

    @pl.when(j == pl.num_programs(1) - 1)
    def _():
        _residual_norm_rows(x_ref, o_ref, gpost_ref, o_ref, tm)


def _ffn(x1, g_pre, w_up, conv_w, conv_b, w_down, g_post, seq):
    T = x1.shape[0]
    tm, tn = min(FFN_TM, seq), FFN_TN
    nj = D_FF // tn
    hb = tm // FFN_HALO
    kern = functools.partial(_ffn_kernel, tiles_per_seq=seq // tm)
    return pl.pallas_call(
        kern,
        grid=(T // tm, nj),
        in_specs=[
            pl.BlockSpec((tm, D_MODEL), lambda i, j: (i, 0)),
            pl.BlockSpec((FFN_HALO, D_MODEL), lambda i, j: (jnp.maximum(i * hb - 1, 0), 0)),
            pl.BlockSpec((1, D_MODEL), lambda i, j: (0, 0)),
            pl.BlockSpec((D_MODEL, tn), lambda i, j: (0, j)),
            pl.BlockSpec((D_MODEL, tn), lambda i, j: (0, nj + j)),
            pl.BlockSpec((CONV_WIDTH, tn), lambda i, j: (0, j)),
            pl.BlockSpec((CONV_WIDTH, tn), lambda i, j: (0, nj + j)),
            pl.BlockSpec((1, tn), lambda i, j: (0, j)),
            pl.BlockSpec((1, tn), lambda i, j: (0, nj + j)),
            pl.BlockSpec((tn, D_MODEL), lambda i, j: (j, 0)),
            pl.BlockSpec((1, D_MODEL), lambda i, j: (0, 0)),
        ],
        out_specs=pl.BlockSpec((tm, D_MODEL), lambda i, j: (i, 0)),
        out_shape=jax.ShapeDtypeStruct((T, D_MODEL), f32),
        scratch_shapes=[
            pltpu.VMEM((tm + FFN_HALO, D_MODEL), bf16),
        ] + [pltpu.VMEM((tm + FFN_HALO, 2 * MXU_COLS), f32)] * (tn // MXU_COLS),
        compiler_params=_params(("parallel", "arbitrary")),
        name="ffn",
    )(x1, x1, g_pre, w_up, w_up, conv_w, conv_w, conv_b, conv_b, w_down, g_post)


def _rotary_tables(seq):
    d = RET_DK
    inv = 1.0 / (ROPE_BASE ** (jnp.arange(0, d, 2, dtype=f32) / d))
    ang = jnp.arange(seq, dtype=jnp.int32).astype(f32)[:, None] * inv[None, :]
    cos, sin = jnp.cos(ang), jnp.sin(ang)
    tab = jnp.concatenate([cos, cos, -sin, sin], axis=-1)
    return jnp.stack([tab, tab * (d ** -0.5)])


def kernel(x, g_pre_mix, w_in, w_ret_o, w_swa_o, w_out, swa_sinks, g_post_mix, g_pre_ffn, w_up, conv_w, conv_b,
           w_down, g_post_ffn):
    B, S, D = x.shape
    depth = w_in.shape[0]
    rot = _rotary_tables(S)
    x2 = x.reshape(B * S, D)
    for l in range(depth):
        proj = _inproj(x2, g_pre_mix[l][None], w_in[l], rot, S)
        ret = _retention(proj, B, S)
        swa = _swa(proj, swa_sinks[l].astype(f32), B, S)
        x2 = _mix(ret, swa, proj, w_ret_o[l].astype(bf16), w_swa_o[l].astype(bf16), w_out[l].astype(bf16), x2,
                  g_post_mix[l][None])
        x2 = _ffn(x2, g_pre_ffn[l][None], w_up[l].astype(bf16), conv_w[l], conv_b[l][None], w_down[l].astype(bf16),
                  g_post_ffn[l][None], S)
    return x2.reshape(B, S, D)
```

```python
import functools

import jax
import jax.numpy as jnp
from jax import lax
from jax.experimental import pallas as pl
from jax.experimental.pallas import tpu as pltpu

D_MODEL = 2048
RET_HEADS = 8
RET_DK = 128
RET_DV = 256
RET_CHUNK = 128
ROPE_BASE = 10000.0
SWA_Q_HEADS = 16
SWA_KV_HEADS = 4
SWA_HEAD_DIM = 64
SWA_BLOCK = 128
D_FF = 5632
CONV_WIDTH = 3
RMS_EPS = 1e-6

RET_QK_W = RET_HEADS * RET_DK
RET_V_W = RET_HEADS * RET_DV
SWA_Q_W = SWA_Q_HEADS * SWA_HEAD_DIM
SWA_KV_W = SWA_KV_HEADS * SWA_HEAD_DIM

COL_RQ = 0
COL_RK = COL_RQ + RET_QK_W
COL_RV = COL_RK + RET_QK_W
COL_RG = COL_RV + RET_V_W
COL_SQ = COL_RG + RET_V_W
COL_SK = COL_SQ + SWA_Q_W
COL_SV = COL_SK + SWA_KV_W
COL_GR = COL_SV + SWA_KV_W
COL_GS = COL_GR + D_MODEL
IN_WIDTH = COL_GS + D_MODEL

V7X_VMEM_LIMIT_BYTES = 60 * 1024 * 1024
BF16_SUBLANES = 16
MXU_COLS = 256
NORM_ROWS = 32
NORM_UNROLL = 4

INPROJ_TM = 1024
INPROJ_TN = 1024
MIX_TM = 512
MIX_TN = 1024
MIX_GATE_W = 512
FFN_TM = 1024
FFN_TN = 512
FFN_HALO = BF16_SUBLANES
FFN_ROW_SPLIT = 2
INPROJ_ROW_SPLIT = 2

f32 = jnp.float32
bf16 = jnp.bfloat16

def _params(semantics):
    return pltpu.CompilerParams(dimension_semantics=semantics, vmem_limit_bytes=V7X_VMEM_LIMIT_BYTES)


def _rmsnorm(x, g):
    ms = jnp.mean(x * x, axis=-1, keepdims=True)
    return x * lax.rsqrt(ms + RMS_EPS) * g


def _norm_rows_to(x_ref, g_ref, dst_ref, dst_off, rows):
    g = g_ref[...]
    group = NORM_ROWS * NORM_UNROLL

    def step(c, carry):
        base = pl.multiple_of(c * group, group)
        for k in range(NORM_UNROLL):
            r = base + k * NORM_ROWS
            dst_ref[pl.ds(dst_off + r, NORM_ROWS), :] = (
                _rmsnorm(x_ref[pl.ds(r, NORM_ROWS), :], g).astype(dst_ref.dtype))
        return carry

    lax.fori_loop(0, rows // group, step, 0)


def _residual_norm_rows(x_ref, y_ref, g_ref, o_ref, rows):
    g = g_ref[...]
    group = NORM_ROWS * NORM_UNROLL

    def step(c, carry):
        base = pl.multiple_of(c * group, group)
        rows_k = [pl.ds(base + k * NORM_ROWS, NORM_ROWS) for k in range(NORM_UNROLL)]
        scales = []
        for r in rows_k:
            y = y_ref[r, :]
            scales.append(lax.rsqrt(jnp.mean(y * y, axis=-1, keepdims=True) + RMS_EPS))
        for r, s in zip(rows_k, scales):
            o_ref[r, :] = x_ref[r, :] + y_ref[r, :] * s * g
        return carry

    lax.fori_loop(0, rows // group, step, 0)


def _inproj_tile_kinds():
    starts = ((COL_RQ, "rot"), (COL_RK, "rot"), (COL_RV, "cast"), (COL_RG, "silu"), (COL_SQ, "cast"),
              (COL_GR, "sig"))
    chunk_kinds = [[k for c0, k in starts if c0 <= col][-1] for col in range(0, IN_WIDTH, MXU_COLS)]
    per_tile = INPROJ_TN // MXU_COLS
    return [tuple(chunk_kinds[t:t + per_tile]) for t in range(0, len(chunk_kinds), per_tile)]


_INPROJ_TILE_KINDS = _inproj_tile_kinds()
_T_RK = COL_RK // INPROJ_TN
assert COL_RK % INPROJ_TN == 0 and COL_RV % INPROJ_TN == 0


def _inproj_kernel(x_ref, g_ref, w_ref, rot_ref, o_ref, h_scr):
    j = pl.program_id(1)
    tm = x_ref.shape[0]

    @pl.when(j == 0)
    def _():
        _norm_rows_to(x_ref, g_ref, h_scr, 0, tm)

    def rotary(acc, rows):
        cos = rot_ref[rows, 0:RET_DK]
        sin = rot_ref[rows, RET_DK:2 * RET_DK]
        heads = []
        for hh in range(acc.shape[1] // RET_DK):
            xh = acc[:, hh * RET_DK:(hh + 1) * RET_DK]
            heads.append(xh * cos + pltpu.roll(xh, RET_DK // 2, axis=1) * sin)
        return jnp.concatenate(heads, axis=1)

    epilogues = {
        "rot": rotary,
        "cast": lambda acc, rows: acc,
        "silu": lambda acc, rows: acc * jax.nn.sigmoid(acc),
        "sig": lambda acc, rows: jax.nn.sigmoid(acc),
    }

    def tile(kinds):
        hm = tm // INPROJ_ROW_SPLIT
        for c, kind in enumerate(kinds):
            cs = slice(c * MXU_COLS, (c + 1) * MXU_COLS)
            w = w_ref[:, cs].astype(bf16)
            for m in range(INPROJ_ROW_SPLIT):
                rows = slice(m * hm, (m + 1) * hm)
                acc = jnp.dot(h_scr[rows, :], w, preferred_element_type=f32)
                o_ref[rows, cs] = epilogues[kind](acc, rows).astype(o_ref.dtype)

    for kinds in sorted(set(_INPROJ_TILE_KINDS)):
        tiles = [t for t, k in enumerate(_INPROJ_TILE_KINDS) if k == kinds]
        cond = functools.reduce(lambda a, b: a | b, [j == t for t in tiles])
        pl.when(cond)(functools.partial(tile, kinds))


def _inproj(x2, g, w_in, rot, seq):
    T = x2.shape[0]
    tm, tn = INPROJ_TM, INPROJ_TN
    tm = min(tm, seq)
    spt = seq // tm
    grid = (T // tm, len(_INPROJ_TILE_KINDS))
    return pl.pallas_call(
        _inproj_kernel,
        grid=grid,
        in_specs=[
            pl.BlockSpec((tm, D_MODEL), lambda i, j: (i, 0)),
            pl.BlockSpec((1, D_MODEL), lambda i, j: (0, 0)),
            pl.BlockSpec((D_MODEL, tn), lambda i, j: (0, j)),
            pl.BlockSpec((None, tm, 2 * RET_DK), lambda i, j: (jnp.where(j >= _T_RK, 1, 0), i % spt, 0)),
        ],
        out_specs=pl.BlockSpec((tm, tn), lambda i, j: (i, j)),
        out_shape=jax.ShapeDtypeStruct((T, IN_WIDTH), bf16),
        scratch_shapes=[pltpu.VMEM((tm, D_MODEL), bf16)],
        compiler_params=_params(("parallel", "arbitrary")),
        name="inproj",
    )(x2, g, w_in, rot)


def _retention_kernel(q_ref, k_ref, v_ref, gate_ref, dmat_ref, xi_ref, zeta_ref, cdec_ref, o_ref, kv_scr, st_scr):
    S = q_ref.shape[0]
    C = RET_CHUNK
    N = S // C
    q3 = q_ref[...].reshape(N, C, RET_DK)
    k3 = k_ref[...].reshape(N, C, RET_DK)
    v3 = v_ref[...].reshape(N, C, RET_DV)

    kz = (k3.astype(f32) * zeta_ref[...][None]).astype(bf16)
    kzt = jnp.swapaxes(kz, 1, 2)
    kv_scr[...] = jnp.einsum("ndk,nkv->ndv", kzt, v3, preferred_element_type=f32)

    cdec = cdec_ref[...]

    def scan(n, state):
        st_scr[n] = state.astype(bf16)
        return state * cdec + kv_scr[n]

    lax.fori_loop(0, N, scan, jnp.zeros((RET_DK, RET_DV), f32))

    sc = jnp.einsum("nqd,nkd->nqk", q3, k3, preferred_element_type=f32) * dmat_ref[...][None]
    inner = jnp.einsum("nqk,nkv->nqv", sc.astype(bf16), v3, preferred_element_type=f32)
    cross = jnp.einsum("nqd,ndv->nqv", q3, st_scr[...], preferred_element_type=f32) * xi_ref[...][None]
    o = inner + cross
    o = o * lax.rsqrt(jnp.mean(o * o, axis=-1, keepdims=True) + RMS_EPS)
    o = gate_ref[...].astype(f32).reshape(N, C, RET_DV) * o
    o_ref[...] = o.reshape(S, RET_DV).astype(o_ref.dtype)


def _retention_tables():
    C = RET_CHUNK
    log_gamma = jnp.log(1.0 - 2.0 ** (-5.0 - jnp.arange(RET_HEADS, dtype=f32)))
    idx = jnp.arange(C, dtype=f32)
    rel = idx[:, None] - idx[None, :]
    dmat = jnp.where(rel[None] >= 0, jnp.exp(log_gamma[:, None, None] * jnp.maximum(rel, 0.0)[None]), 0.0)
    xi = jnp.exp(log_gamma[:, None] * (idx + 1.0))
    zeta = jnp.exp(log_gamma[:, None] * (C - 1.0 - idx))
    cdec = jnp.exp(log_gamma * C)
    xi_b = jnp.broadcast_to(xi[:, :, None], (RET_HEADS, C, RET_DV))
    zeta_b = jnp.broadcast_to(zeta[:, :, None], (RET_HEADS, C, RET_DK))
    cdec_b = jnp.broadcast_to(cdec[:, None, None], (RET_HEADS, 1, RET_DV))
    return dmat.astype(f32), xi_b.astype(f32), zeta_b.astype(f32), cdec_b.astype(f32)


def _retention(proj, batch, seq):
    T = proj.shape[0]
    C = RET_CHUNK
    N = seq // C
    dmat, xi_b, zeta_b, cdec_b = _retention_tables()
    return pl.pallas_call(
        _retention_kernel,
        grid=(batch, RET_HEADS),
        in_specs=[
            pl.BlockSpec((seq, RET_DK), lambda b, h: (b, COL_RQ // RET_DK + h)),
            pl.BlockSpec((seq, RET_DK), lambda b, h: (b, COL_RK // RET_DK + h)),
            pl.BlockSpec((seq, RET_DV), lambda b, h: (b, COL_RV // RET_DV + h)),
            pl.BlockSpec((seq, RET_DV), lambda b, h: (b, COL_RG // RET_DV + h)),
            pl.BlockSpec((None, C, C), lambda b, h: (h, 0, 0)),
            pl.BlockSpec((None, C, RET_DV), lambda b, h: (h, 0, 0)),
            pl.BlockSpec((None, C, RET_DK), lambda b, h: (h, 0, 0)),
            pl.BlockSpec((None, 1, RET_DV), lambda b, h: (h, 0, 0)),
        ],
        out_specs=pl.BlockSpec((seq, RET_DV), lambda b, h: (b, h)),
        out_shape=jax.ShapeDtypeStruct((T, RET_V_W), bf16),
        scratch_shapes=[pltpu.VMEM((N, RET_DK, RET_DV), f32), pltpu.VMEM((N, RET_DK, RET_DV), bf16)],
        compiler_params=_params(("parallel", "parallel")),
        name="retention",
    )(proj, proj, proj, proj, dmat, xi_b, zeta_b, cdec_b)


KV_PER_STEP = 2
Q_PER_KV = SWA_Q_HEADS // SWA_KV_HEADS
SWA_Q_STEP_W = KV_PER_STEP * Q_PER_KV * SWA_HEAD_DIM
LANES = 128
SWA_UNROLL = 4


def _swa_kernel(sink_ref, bias_ref, q_ref, k_ref, v_ref, o_ref, klo, khi, vlo, vhi):
    S = q_ref.shape[0]
    C = SWA_BLOCK
    gp = pl.program_id(1)
    half = SWA_HEAD_DIM

    lane_s = lax.broadcasted_iota(jnp.int32, (S, LANES), 1)
    low_s = lane_s < half

    def prep(src_ref, lo_scr, hi_scr, scale, with_ones):
        w = src_ref[...].astype(f32) * scale
        r = pltpu.roll(w, half, axis=1)
        zeros = jnp.zeros((C, lo_scr.shape[2]), bf16)
        for t in range(KV_PER_STEP):
            lo_scr[t, 0:C, :] = zeros
            hi_scr[t, 0:C, :] = zeros
        lo_scr[0, C:C + S, 0:LANES] = jnp.where(low_s, w, 0.0).astype(bf16)
        hi_scr[0, C:C + S, 0:LANES] = jnp.where(low_s, 0.0, r).astype(bf16)
        lo_scr[1, C:C + S, 0:LANES] = jnp.where(low_s, r, 0.0).astype(bf16)
        hi_scr[1, C:C + S, 0:LANES] = jnp.where(low_s, 0.0, w).astype(bf16)
        if with_ones:
            for t in range(KV_PER_STEP):
                lo_scr[t, C:C + S, LANES:2 * LANES] = jnp.where(low_s, 1.0, 0.0).astype(bf16)
                hi_scr[t, C:C + S, LANES:2 * LANES] = jnp.where(low_s, 0.0, 1.0).astype(bf16)

    prep(k_ref, klo, khi, SWA_HEAD_DIM ** -0.5, False)
    prep(v_ref, vlo, vhi, 1.0, True)

    rows = 2 * C
    win = 2 * C
    first_pair = lax.broadcasted_iota(jnp.int32, (rows, 1), 0) < C
    low_o = lax.broadcasted_iota(jnp.int32, (rows, LANES), 1) < half
    nt = (((1,), (1,)), ((), ()))

    def softmax_parts(s, sink, bias):
        s = s + bias
        m = jnp.maximum(jnp.max(s, axis=-1, keepdims=True), sink)
        return jnp.exp(s - m).astype(bf16), jnp.exp(sink - m)

    def block(n, carry):
        r0 = pl.multiple_of(n * C, C)
        bias = bias_ref[jnp.minimum(n, 1)]
        for t in range(KV_PER_STEP):
            c0 = t * Q_PER_KV * SWA_HEAD_DIM
            hbase = gp * (KV_PER_STEP * Q_PER_KV) + t * Q_PER_KV
            qs = jnp.concatenate([q_ref[pl.ds(r0, C), c0:c0 + LANES],
                                  q_ref[pl.ds(r0, C), c0 + LANES:c0 + 2 * LANES]], axis=0)
            s_e = lax.dot_general(qs, klo[t, pl.ds(r0, win), :], nt, preferred_element_type=f32)
            s_o = lax.dot_general(qs, khi[t, pl.ds(r0, win), :], nt, preferred_element_type=f32)
            sink_e = jnp.where(first_pair, sink_ref[hbase + 0], sink_ref[hbase + 2])
            sink_o = jnp.where(first_pair, sink_ref[hbase + 1], sink_ref[hbase + 3])
            p_e, z_e = softmax_parts(s_e, sink_e, bias)
            p_o, z_o = softmax_parts(s_o, sink_o, bias)
            pv = (jnp.dot(p_e, vlo[t, pl.ds(r0, win), :], preferred_element_type=f32)
                  + jnp.dot(p_o, vhi[t, pl.ds(r0, win), :], preferred_element_type=f32))
            den = pv[:, LANES:2 * LANES] + jnp.where(low_o, z_e, z_o)
            o = pv[:, 0:LANES] / den
            o_ref[pl.ds(r0, C), c0:c0 + LANES] = o[0:C].astype(o_ref.dtype)
            o_ref[pl.ds(r0, C), c0 + LANES:c0 + 2 * LANES] = o[C:2 * C].astype(o_ref.dtype)
        return carry

    lax.fori_loop(0, S // C, block, 0, unroll=SWA_UNROLL)


def _swa_bias():
    C = SWA_BLOCK
    row_i = jnp.arange(2 * C)[:, None] % C
    col_j = jnp.arange(2 * C)[None, :]
    band = (col_j > row_i) & (col_j <= row_i + C)
    first = band & (col_j >= C)
    return jnp.where(jnp.stack([first, band]), 0.0, -jnp.inf).astype(f32)


def _swa(proj, sinks, batch, seq):
    T = proj.shape[0]
    steps = SWA_KV_HEADS // KV_PER_STEP
    k_scr = pltpu.VMEM((KV_PER_STEP, seq + SWA_BLOCK, LANES), bf16)
    v_scr = pltpu.VMEM((KV_PER_STEP, seq + SWA_BLOCK, 2 * LANES), bf16)
    bias = _swa_bias()
    return pl.pallas_call(
        _swa_kernel,
        grid=(batch, steps),
        in_specs=[
            pl.BlockSpec(memory_space=pltpu.SMEM),
            pl.BlockSpec(bias.shape, lambda b, g: (0, 0, 0)),
            pl.BlockSpec((seq, SWA_Q_STEP_W), lambda b, g: (b, COL_SQ // SWA_Q_STEP_W + g)),
            pl.BlockSpec((seq, LANES), lambda b, g: (b, COL_SK // LANES + g)),
            pl.BlockSpec((seq, LANES), lambda b, g: (b, COL_SV // LANES + g)),
        ],
        out_specs=pl.BlockSpec((seq, SWA_Q_STEP_W), lambda b, g: (b, g)),
        out_shape=jax.ShapeDtypeStruct((T, SWA_Q_W), bf16),
        scratch_shapes=[k_scr, k_scr, v_scr, v_scr],
        compiler_params=_params(("parallel", "parallel")),
        name="swa",
    )(sinks, bias, proj, proj, proj)


def _mix_kernel(ret_ref, swa_ref, *refs):
    n_gate = MIX_TN // MIX_GATE_W
    gr_refs, gs_refs = refs[:n_gate], refs[n_gate:2 * n_gate]
    wr_ref, ws_ref, wo_ref, x_ref, g_ref, o_ref = refs[2 * n_gate:]
    j = pl.program_id(1)

    @pl.when(j == 0)
    def _():
        o_ref[...] = jnp.zeros_like(o_ref)

    part = None
    for c in range(n_gate):
        cs = slice(c * MIX_GATE_W, (c + 1) * MIX_GATE_W)
        ret_out = jnp.dot(ret_ref[...], wr_ref[:, cs], preferred_element_type=f32)
        swa_out = jnp.dot(swa_ref[...], ws_ref[:, cs], preferred_element_type=f32)
        mixed = gr_refs[c][...].astype(f32) * ret_out + gs_refs[c][...].astype(f32) * swa_out
        d = jnp.dot(mixed.astype(bf16), wo_ref[cs, :], preferred_element_type=f32)
        part = d if part is None else part + d
    o_ref[...] += part

    @pl.when(j == pl.num_programs(1) - 1)
    def _():
        _residual_norm_rows(x_ref, o_ref, g_ref, o_ref, o_ref.shape[0])


def _mix(ret, swa, proj, w_ret_o, w_swa_o, w_out, x2, g):
    T = x2.shape[0]
    tm, tn = min(MIX_TM, T), MIX_TN
    nj = D_MODEL // tn
    gw = MIX_GATE_W
    n_gate = tn // gw

    def gate_specs(col0):
        return [pl.BlockSpec((tm, gw), functools.partial(lambda i, j, c: (i, col0 // gw + j * n_gate + c), c=c))
                for c in range(n_gate)]

    return pl.pallas_call(
        _mix_kernel,
        grid=(T // tm, nj),
        in_specs=[
            pl.BlockSpec((tm, RET_V_W), lambda i, j: (i, 0)),
            pl.BlockSpec((tm, SWA_Q_W), lambda i, j: (i, 0)),
            *gate_specs(COL_GR),
            *gate_specs(COL_GS),
            pl.BlockSpec((RET_V_W, tn), lambda i, j: (0, j)),
            pl.BlockSpec((SWA_Q_W, tn), lambda i, j: (0, j)),
            pl.BlockSpec((tn, D_MODEL), lambda i, j: (j, 0)),
            pl.BlockSpec((tm, D_MODEL), lambda i, j: (i, 0)),
            pl.BlockSpec((1, D_MODEL), lambda i, j: (0, 0)),
        ],
        out_specs=pl.BlockSpec((tm, D_MODEL), lambda i, j: (i, 0)),
        out_shape=jax.ShapeDtypeStruct((T, D_MODEL), f32),
        compiler_params=_params(("parallel", "arbitrary")),
        name="mix",
    )(ret, swa, *([proj] * (2 * n_gate)), w_ret_o, w_swa_o, w_out, x2, g)


def _gelu_tanh(x):
    return x * (0.5 * (1.0 + jnp.tanh(0.7978845608028654 * (x + 0.044715 * (x * x * x)))))


def _ffn_kernel(x_ref, halo_ref, gpre_ref, wv_ref, wg_ref, cwv_ref, cwg_ref, cbv_ref, cbg_ref, wd_ref, gpost_ref,
                o_ref, h_scr, *u_scrs, tiles_per_seq):
    i = pl.program_id(0)
    j = pl.program_id(1)
    tm = x_ref.shape[0]
    tn = wv_ref.shape[1]
    H = FFN_HALO

    @pl.when(j == 0)
    def _():
        keep = jnp.where(i % tiles_per_seq == 0, 0.0, 1.0)
        h_scr[0:H, :] = (_rmsnorm(halo_ref[...], gpre_ref[...]) * keep).astype(bf16)
        _norm_rows_to(x_ref, gpre_ref, h_scr, H, tm)
        o_ref[...] = jnp.zeros_like(o_ref)

    def conv(u_scr, row0, nrows, u_cols, w_cols, cw_ref, cb_ref):
        y = cb_ref[:, w_cols]
        for kk in range(CONV_WIDTH):
            y = y + u_scr[pl.ds(H - (CONV_WIDTH - 1) + kk + row0, nrows), u_cols] * cw_ref[kk:kk + 1, w_cols]
        return y

    h = h_scr[...]
    hm = tm // FFN_ROW_SPLIT
    parts = [None] * FFN_ROW_SPLIT
    uv = slice(0, MXU_COLS)
    ug = slice(MXU_COLS, 2 * MXU_COLS)
    for c, u_scr in enumerate(u_scrs):
        cs = slice(c * MXU_COLS, (c + 1) * MXU_COLS)
        u_scr[:, uv] = jnp.dot(h, wv_ref[:, cs], preferred_element_type=f32)
        u_scr[:, ug] = jnp.dot(h, wg_ref[:, cs], preferred_element_type=f32)
        for m in range(FFN_ROW_SPLIT):
            val = conv(u_scr, m * hm, hm, uv, cs, cwv_ref, cbv_ref)
            gate = conv(u_scr, m * hm, hm, ug, cs, cwg_ref, cbg_ref)
            a = (_gelu_tanh(gate) * val).astype(bf16)
            d = jnp.dot(a, wd_ref[cs, :], preferred_element_type=f32)
            parts[m] = d if parts[m] is None else parts[m] + d
    for m in range(FFN_ROW_SPLIT):
        o_ref[m * hm:(m + 1) * hm, :] += parts[m]

    @pl.when(j == pl.num_programs(1) - 1)
    def _():
        _residual_norm_rows(x_ref, o_ref, gpost_ref, o_ref, tm)


def _ffn(x1, g_pre, w_up, conv_w, conv_b, w_down, g_post, seq):
    T = x1.shape[0]
    tm, tn = min(FFN_TM, seq), FFN_TN
    nj = D_FF // tn
    hb = tm // FFN_HALO
    kern = functools.partial(_ffn_kernel, tiles_per_seq=seq // tm)
    return pl.pallas_call(
        kern,
        grid=(T // tm, nj),
        in_specs=[
            pl.BlockSpec((tm, D_MODEL), lambda i, j: (i, 0)),
            pl.BlockSpec((FFN_HALO, D_MODEL), lambda i, j: (jnp.maximum(i * hb - 1, 0), 0)),
            pl.BlockSpec((1, D_MODEL), lambda i, j: (0, 0)),
            pl.BlockSpec((D_MODEL, tn), lambda i, j: (0, j)),
            pl.BlockSpec((D_MODEL, tn), lambda i, j: (0, nj + j)),
            pl.BlockSpec((CONV_WIDTH, tn), lambda i, j: (0, j)),
            pl.BlockSpec((CONV_WIDTH, tn), lambda i, j: (0, nj + j)),
            pl.BlockSpec((1, tn), lambda i, j: (0, j)),
            pl.BlockSpec((1, tn), lambda i, j: (0, nj + j)),
            pl.BlockSpec((tn, D_MODEL), lambda i, j: (j, 0)),
            pl.BlockSpec((1, D_MODEL), lambda i, j: (0, 0)),
        ],
        out_specs=pl.BlockSpec((tm, D_MODEL), lambda i, j: (i, 0)),
        out_shape=jax.ShapeDtypeStruct((T, D_MODEL), f32),
        scratch_shapes=[
            pltpu.VMEM((tm + FFN_HALO, D_MODEL), bf16),
        ] + [pltpu.VMEM((tm + FFN_HALO, 2 * MXU_COLS), f32)] * (tn // MXU_COLS),
        compiler_params=_params(("parallel", "arbitrary")),
        name="ffn",
    )(x1, x1, g_pre, w_up, w_up, conv_w, conv_w, conv_b, conv_b, w_down, g_post)


def _rotary_tables(seq):
    d = RET_DK
    inv = 1.0 / (ROPE_BASE ** (jnp.arange(0, d, 2, dtype=f32) / d))
    ang = jnp.arange(seq, dtype=jnp.int32).astype(f32)[:, None] * inv[None, :]
    cos, sin = jnp.cos(ang), jnp.sin(ang)
    tab = jnp.concatenate([cos, cos, -sin, sin], axis=-1)
    return jnp.stack([tab, tab * (d ** -0.5)])


def kernel(x, g_pre_mix, w_in, w_ret_o, w_swa_o, w_out, swa_sinks, g_post_mix, g_pre_ffn, w_up, conv_w, conv_b,
           w_down, g_post_ffn):
    B, S, D = x.shape
    depth = w_in.shape[0]
    rot = _rotary_tables(S)
    x2 = x.reshape(B * S, D)
    for l in range(depth):
        proj = _inproj(x2, g_pre_mix[l][None], w_in[l], rot, S)
        ret = _retention(proj, B, S)
        swa = _swa(proj, swa_sinks[l].astype(f32), B, S)
        x2 = _mix(ret, swa, proj, w_ret_o[l].astype(bf16), w_swa_o[l].astype(bf16), w_out[l].astype(bf16), x2,
                  g_post_mix[l][None])
        x2 = _ffn(x2, g_pre_ffn[l][None], w_up[l].astype(bf16), conv_w[l], conv_b[l][None], w_down[l].astype(bf16),
                  g_post_ffn[l][None], S)
    return x2.reshape(B, S, D)
```

```python
import functools

import jax
import jax.numpy as jnp
from jax import lax
from jax.experimental import pallas as pl
from jax.experimental.pallas import tpu as pltpu

D_MODEL = 2048
RET_HEADS = 8
RET_DK = 128
RET_DV = 256
RET_CHUNK = 128
ROPE_BASE = 10000.0
SWA_Q_HEADS = 16
SWA_KV_HEADS = 4
SWA_HEAD_DIM = 64
SWA_BLOCK = 128
D_FF = 5632
CONV_WIDTH = 3
RMS_EPS = 1e-6

RET_QK_W = RET_HEADS * RET_DK
RET_V_W = RET_HEADS * RET_DV
SWA_Q_W = SWA_Q_HEADS * SWA_HEAD_DIM
SWA_KV_W = SWA_KV_HEADS * SWA_HEAD_DIM

COL_RQ = 0
COL_RK = COL_RQ + RET_QK_W
COL_RV = COL_RK + RET_QK_W
COL_RG = COL_RV + RET_V_W
COL_SQ = COL_RG + RET_V_W
COL_SK = COL_SQ + SWA_Q_W
COL_SV = COL_SK + SWA_KV_W
COL_GR = COL_SV + SWA_KV_W
COL_GS = COL_GR + D_MODEL
IN_WIDTH = COL_GS + D_MODEL

V7X_VMEM_LIMIT_BYTES = 60 * 1024 * 1024
BF16_SUBLANES = 16
MXU_COLS = 256
NORM_ROWS = 32
NORM_UNROLL = 4

INPROJ_TM = 1024
INPROJ_TN = 1024
MIX_TM = 512
MIX_TN = 1024
MIX_GATE_W = 512
FFN_TM = 1024
FFN_TN = 512
FFN_HALO = BF16_SUBLANES
FFN_ROW_SPLIT = 2
INPROJ_ROW_SPLIT = 2

f32 = jnp.float32
bf16 = jnp.bfloat16

def _params(semantics):
    return pltpu.CompilerParams(dimension_semantics=semantics, vmem_limit_bytes=V7X_VMEM_LIMIT_BYTES)


def _rmsnorm(x, g):
    ms = jnp.mean(x * x, axis=-1, keepdims=True)
    return x * lax.rsqrt(ms + RMS_EPS) * g


def _norm_rows_to(x_ref, g_ref, dst_ref, dst_off, rows):
    g = g_ref[...]
    group = NORM_ROWS * NORM_UNROLL

    def step(c, carry):
        base = pl.multiple_of(c * group, group)
        for k in range(NORM_UNROLL):
            r = base + k * NORM_ROWS
            dst_ref[pl.ds(dst_off + r, NORM_ROWS), :] = (
                _rmsnorm(x_ref[pl.ds(r, NORM_ROWS), :], g).astype(dst_ref.dtype))
        return carry

    lax.fori_loop(0, rows // group, step, 0)


def _residual_norm_rows(x_ref, y_ref, g_ref, o_ref, rows):
    g = g_ref[...]
    group = NORM_ROWS * NORM_UNROLL

    def step(c, carry):
        base = pl.multiple_of(c * group, group)
        rows_k = [pl.ds(base + k * NORM_ROWS, NORM_ROWS) for k in range(NORM_UNROLL)]
        scales = []
        for r in rows_k:
            y = y_ref[r, :]
            scales.append(lax.rsqrt(jnp.mean(y * y, axis=-1, keepdims=True) + RMS_EPS))
        for r, s in zip(rows_k, scales):
            o_ref[r, :] = x_ref[r, :] + y_ref[r, :] * s * g
        return carry

    lax.fori_loop(0, rows // group, step, 0)


def _inproj_tile_kinds():
    starts = ((COL_RQ, "rot"), (COL_RK, "rot"), (COL_RV, "cast"), (COL_RG, "silu"), (COL_SQ, "cast"),
              (COL_GR, "sig"))
    chunk_kinds = [[k for c0, k in starts if c0 <= col][-1] for col in range(0, IN_WIDTH, MXU_COLS)]
    per_tile = INPROJ_TN // MXU_COLS
    return [tuple(chunk_kinds[t:t + per_tile]) for t in range(0, len(chunk_kinds), per_tile)]


_INPROJ_TILE_KINDS = _inproj_tile_kinds()
_T_RK = COL_RK // INPROJ_TN
assert COL_RK % INPROJ_TN == 0 and COL_RV % INPROJ_TN == 0


def _inproj_kernel(x_ref, g_ref, w_ref, rot_ref, o_ref, h_scr):
    j = pl.program_id(1)
    tm = x_ref.shape[0]

    @pl.when(j == 0)
    def _():
        _norm_rows_to(x_ref, g_ref, h_scr, 0, tm)

    def rotary(acc, rows):
        cos = rot_ref[rows, 0:RET_DK]
        sin = rot_ref[rows, RET_DK:2 * RET_DK]
        heads = []
        for hh in range(acc.shape[1] // RET_DK):
            xh = acc[:, hh * RET_DK:(hh + 1) * RET_DK]
            heads.append(xh * cos + pltpu.roll(xh, RET_DK // 2, axis=1) * sin)
        return jnp.concatenate(heads, axis=1)

    epilogues = {
        "rot": rotary,
        "cast": lambda acc, rows: acc,
        "silu": lambda acc, rows: acc * jax.nn.sigmoid(acc),
        "sig": lambda acc, rows: jax.nn.sigmoid(acc),
    }

    def tile(kinds):
        hm = tm // INPROJ_ROW_SPLIT
        for c, kind in enumerate(kinds):
            cs = slice(c * MXU_COLS, (c + 1) * MXU_COLS)
            w = w_ref[:, cs]
            for m in range(INPROJ_ROW_SPLIT):
                rows = slice(m * hm, (m + 1) * hm)
                acc = jnp.dot(h_scr[rows, :], w, preferred_element_type=f32)
                o_ref[rows, cs] = epilogues[kind](acc, rows).astype(o_ref.dtype)

    for kinds in sorted(set(_INPROJ_TILE_KINDS)):
        tiles = [t for t, k in enumerate(_INPROJ_TILE_KINDS) if k == kinds]
        cond = functools.reduce(lambda a, b: a | b, [j == t for t in tiles])
        pl.when(cond)(functools.partial(tile, kinds))


def _inproj(x2, g, w_in, rot, seq):
    T = x2.shape[0]
    tm, tn = INPROJ_TM, INPROJ_TN
    tm = min(tm, seq)
    spt = seq // tm
    grid = (T // tm, len(_INPROJ_TILE_KINDS))
    return pl.pallas_call(
        _inproj_kernel,
        grid=grid,
        in_specs=[
            pl.BlockSpec((tm, D_MODEL), lambda i, j: (i, 0)),
            pl.BlockSpec((1, D_MODEL), lambda i, j: (0, 0)),
            pl.BlockSpec((D_MODEL, tn), lambda i, j: (0, j)),
            pl.BlockSpec((None, tm, 2 * RET_DK), lambda i, j: (jnp.where(j >= _T_RK, 1, 0), i % spt, 0)),
        ],
        out_specs=pl.BlockSpec((tm, tn), lambda i, j: (i, j)),
        out_shape=jax.ShapeDtypeStruct((T, IN_WIDTH), bf16),
        scratch_shapes=[pltpu.VMEM((tm, D_MODEL), bf16)],
        compiler_params=_params(("parallel", "arbitrary")),
        name="inproj",
    )(x2, g, w_in, rot)


def _retention_kernel(q_ref, k_ref, v_ref, gate_ref, dmat_ref, xi_ref, zeta_ref, cdec_ref, o_ref, kv_scr, st_scr):
    S = q_ref.shape[0]
    C = RET_CHUNK
    N = S // C
    q3 = q_ref[...].reshape(N, C, RET_DK)
    k3 = k_ref[...].reshape(N, C, RET_DK)
    v3 = v_ref[...].reshape(N, C, RET_DV)

    kz = (k3.astype(f32) * zeta_ref[...][None]).astype(bf16)
    kzt = jnp.swapaxes(kz, 1, 2)
    kv_scr[...] = jnp.einsum("ndk,nkv->ndv", kzt, v3, preferred_element_type=f32)

    cdec = cdec_ref[...]

    def scan(n, state):
        st_scr[n] = state.astype(bf16)
        return state * cdec + kv_scr[n]

    lax.fori_loop(0, N, scan, jnp.zeros((RET_DK, RET_DV), f32))

    sc = jnp.einsum("nqd,nkd->nqk", q3, k3, preferred_element_type=f32) * dmat_ref[...][None]
    inner = jnp.einsum("nqk,nkv->nqv", sc.astype(bf16), v3, preferred_element_type=f32)
    cross = jnp.einsum("nqd,ndv->nqv", q3, st_scr[...], preferred_element_type=f32) * xi_ref[...][None]
    o = inner + cross
    o = o * lax.rsqrt(jnp.mean(o * o, axis=-1, keepdims=True) + RMS_EPS)
    o = gate_ref[...].astype(f32).reshape(N, C, RET_DV) * o
    o_ref[...] = o.reshape(S, RET_DV).astype(o_ref.dtype)


def _retention_tables():
    C = RET_CHUNK
    log_gamma = jnp.log(1.0 - 2.0 ** (-5.0 - jnp.arange(RET_HEADS, dtype=f32)))
    idx = jnp.arange(C, dtype=f32)
    rel = idx[:, None] - idx[None, :]
    dmat = jnp.where(rel[None] >= 0, jnp.exp(log_gamma[:, None, None] * jnp.maximum(rel, 0.0)[None]), 0.0)
    xi = jnp.exp(log_gamma[:, None] * (idx + 1.0))
    zeta = jnp.exp(log_gamma[:, None] * (C - 1.0 - idx))
    cdec = jnp.exp(log_gamma * C)
    xi_b = jnp.broadcast_to(xi[:, :, None], (RET_HEADS, C, RET_DV))
    zeta_b = jnp.broadcast_to(zeta[:, :, None], (RET_HEADS, C, RET_DK))
    cdec_b = jnp.broadcast_to(cdec[:, None, None], (RET_HEADS, 1, RET_DV))
    return dmat.astype(f32), xi_b.astype(f32), zeta_b.astype(f32), cdec_b.astype(f32)


def _retention(proj, batch, seq):
    T = proj.shape[0]
    C = RET_CHUNK
    N = seq // C
    dmat, xi_b, zeta_b, cdec_b = _retention_tables()
    return pl.pallas_call(
        _retention_kernel,
        grid=(batch, RET_HEADS),
        in_specs=[
            pl.BlockSpec((seq, RET_DK), lambda b, h: (b, COL_RQ // RET_DK + h)),
            pl.BlockSpec((seq, RET_DK), lambda b, h: (b, COL_RK // RET_DK + h)),
            pl.BlockSpec((seq, RET_DV), lambda b, h: (b, COL_RV // RET_DV + h)),
            pl.BlockSpec((seq, RET_DV), lambda b, h: (b, COL_RG // RET_DV + h)),
            pl.BlockSpec((None, C, C), lambda b, h: (h, 0, 0)),
            pl.BlockSpec((None, C, RET_DV), lambda b, h: (h, 0, 0)),
            pl.BlockSpec((None, C, RET_DK), lambda b, h: (h, 0, 0)),
            pl.BlockSpec((None, 1, RET_DV), lambda b, h: (h, 0, 0)),
        ],
        out_specs=pl.BlockSpec((seq, RET_DV), lambda b, h: (b, h)),
        out_shape=jax.ShapeDtypeStruct((T, RET_V_W), bf16),
        scratch_shapes=[pltpu.VMEM((N, RET_DK, RET_DV), f32), pltpu.VMEM((N, RET_DK, RET_DV), bf16)],
        compiler_params=_params(("parallel", "parallel")),
        name="retention",
    )(proj, proj, proj, proj, dmat, xi_b, zeta_b, cdec_b)


KV_PER_STEP = 2
Q_PER_KV = SWA_Q_HEADS // SWA_KV_HEADS
SWA_Q_STEP_W = KV_PER_STEP * Q_PER_KV * SWA_HEAD_DIM
LANES = 128
SWA_UNROLL = 4


def _swa_kernel(sink_ref, bias_ref, q_ref, k_ref, v_ref, o_ref, klo, khi, vlo, vhi):
    S = q_ref.shape[0]
    C = SWA_BLOCK
    gp = pl.program_id(1)
    half = SWA_HEAD_DIM

    lane_s = lax.broadcasted_iota(jnp.int32, (S, LANES), 1)
    low_s = lane_s < half

    def prep(src_ref, lo_scr, hi_scr, scale, with_ones):
        w = src_ref[...].astype(f32) * scale
        r = pltpu.roll(w, half, axis=1)
        zeros = jnp.zeros((C, lo_scr.shape[2]), bf16)
        for t in range(KV_PER_STEP):
            lo_scr[t, 0:C, :] = zeros
            hi_scr[t, 0:C, :] = zeros
        lo_scr[0, C:C + S, 0:LANES] = jnp.where(low_s, w, 0.0).astype(bf16)
        hi_scr[0, C:C + S, 0:LANES] = jnp.where(low_s, 0.0, r).astype(bf16)
        lo_scr[1, C:C + S, 0:LANES] = jnp.where(low_s, r, 0.0).astype(bf16)
        hi_scr[1, C:C + S, 0:LANES] = jnp.where(low_s, 0.0, w).astype(bf16)
        if with_ones:
            for t in range(KV_PER_STEP):
                lo_scr[t, C:C + S, LANES:2 * LANES] = jnp.where(low_s, 1.0, 0.0).astype(bf16)
                hi_scr[t, C:C + S, LANES:2 * LANES] = jnp.where(low_s, 0.0, 1.0).astype(bf16)

    prep(k_ref, klo, khi, SWA_HEAD_DIM ** -0.5, False)
    prep(v_ref, vlo, vhi, 1.0, True)

    rows = 2 * C
    win = 2 * C
    first_pair = lax.broadcasted_iota(jnp.int32, (rows, 1), 0) < C
    low_o = lax.broadcasted_iota(jnp.int32, (rows, LANES), 1) < half
    nt = (((1,), (1,)), ((), ()))

    def softmax_parts(s, sink, bias):
        s = s + bias
        m = jnp.maximum(jnp.max(s, axis=-1, keepdims=True), sink)
        return jnp.exp(s - m).astype(bf16), jnp.exp(sink - m)

    def block(n, carry):
        r0 = pl.multiple_of(n * C, C)
        bias = bias_ref[jnp.minimum(n, 1)]
        for t in range(KV_PER_STEP):
            c0 = t * Q_PER_KV * SWA_HEAD_DIM
            hbase = gp * (KV_PER_STEP * Q_PER_KV) + t * Q_PER_KV
            qs = jnp.concatenate([q_ref[pl.ds(r0, C), c0:c0 + LANES],
                                  q_ref[pl.ds(r0, C), c0 + LANES:c0 + 2 * LANES]], axis=0)
            s_e = lax.dot_general(qs, klo[t, pl.ds(r0, win), :], nt, preferred_element_type=f32)
            s_o = lax.dot_general(qs, khi[t, pl.ds(r0, win), :], nt, preferred_element_type=f32)
            sink_e = jnp.where(first_pair, sink_ref[hbase + 0], sink_ref[hbase + 2])
            sink_o = jnp.where(first_pair, sink_ref[hbase + 1], sink_ref[hbase + 3])
            p_e, z_e = softmax_parts(s_e, sink_e, bias)
            p_o, z_o = softmax_parts(s_o, sink_o, bias)
            pv = (jnp.dot(p_e, vlo[t, pl.ds(r0, win), :], preferred_element_type=f32)
                  + jnp.dot(p_o, vhi[t, pl.ds(r0, win), :], preferred_element_type=f32))
            den = pv[:, LANES:2 * LANES] + jnp.where(low_o, z_e, z_o)
            o = pv[:, 0:LANES] / den
            o_ref[pl.ds(r0, C), c0:c0 + LANES] = o[0:C].astype(o_ref.dtype)
            o_ref[pl.ds(r0, C), c0 + LANES:c0 + 2 * LANES] = o[C:2 * C].astype(o_ref.dtype)
        return carry

    lax.fori_loop(0, S // C, block, 0, unroll=SWA_UNROLL)


def _swa_bias():
    C = SWA_BLOCK
    row_i = jnp.arange(2 * C)[:, None] % C
    col_j = jnp.arange(2 * C)[None, :]
    band = (col_j > row_i) & (col_j <= row_i + C)
    first = band & (col_j >= C)
    return jnp.where(jnp.stack([first, band]), 0.0, -jnp.inf).astype(f32)


def _swa(proj, sinks, batch, seq):
    T = proj.shape[0]
    steps = SWA_KV_HEADS // KV_PER_STEP
    k_scr = pltpu.VMEM((KV_PER_STEP, seq + SWA_BLOCK, LANES), bf16)
    v_scr = pltpu.VMEM((KV_PER_STEP, seq + SWA_BLOCK, 2 * LANES), bf16)
    bias = _swa_bias()
    return pl.pallas_call(
        _swa_kernel,
        grid=(batch, steps),
        in_specs=[
            pl.BlockSpec(memory_space=pltpu.SMEM),
            pl.BlockSpec(bias.shape, lambda b, g: (0, 0, 0)),
            pl.BlockSpec((seq, SWA_Q_STEP_W), lambda b, g: (b, COL_SQ // SWA_Q_STEP_W + g)),
            pl.BlockSpec((seq, LANES), lambda b, g: (b, COL_SK // LANES + g)),
            pl.BlockSpec((seq, LANES), lambda b, g: (b, COL_SV // LANES + g)),
        ],
        out_specs=pl.BlockSpec((seq, SWA_Q_STEP_W), lambda b, g: (b, g)),
        out_shape=jax.ShapeDtypeStruct((T, SWA_Q_W), bf16),
        scratch_shapes=[k_scr, k_scr, v_scr, v_scr],
        compiler_params=_params(("parallel", "parallel")),
        name="swa",
    )(sinks, bias, proj, proj, proj)


def _mix_kernel(ret_ref, swa_ref, *refs):
    n_gate = MIX_TN // MIX_GATE_W
    gr_refs, gs_refs = refs[:n_gate], refs[n_gate:2 * n_gate]
    wr_ref, ws_ref, wo_ref, x_ref, g_ref, o_ref = refs[2 * n_gate:]
    j = pl.program_id(1)

    @pl.when(j == 0)
    def _():
        o_ref[...] = jnp.zeros_like(o_ref)

    part = None
    for c in range(n_gate):
        cs = slice(c * MIX_GATE_W, (c + 1) * MIX_GATE_W)
        ret_out = jnp.dot(ret_ref[...], wr_ref[:, cs], preferred_element_type=f32)
        swa_out = jnp.dot(swa_ref[...], ws_ref[:, cs], preferred_element_type=f32)
        mixed = gr_refs[c][...].astype(f32) * ret_out + gs_refs[c][...].astype(f32) * swa_out
        d = jnp.dot(mixed.astype(bf16), wo_ref[cs, :], preferred_element_type=f32)
        part = d if part is None else part + d
    o_ref[...] += part

    @pl.when(j == pl.num_programs(1) - 1)
    def _():
        _residual_norm_rows(x_ref, o_ref, g_ref, o_ref, o_ref.shape[0])


def _mix(ret, swa, proj, w_ret_o, w_swa_o, w_out, x2, g):
    T = x2.shape[0]
    tm, tn = min(MIX_TM, T), MIX_TN
    nj = D_MODEL // tn
    gw = MIX_GATE_W
    n_gate = tn // gw

    def gate_specs(col0):
        return [pl.BlockSpec((tm, gw), functools.partial(lambda i, j, c: (i, col0 // gw + j * n_gate + c), c=c))
                for c in range(n_gate)]

    return pl.pallas_call(
        _mix_kernel,
        grid=(T // tm, nj),
        in_specs=[
            pl.BlockSpec((tm, RET_V_W), lambda i, j: (i, 0)),
            pl.BlockSpec((tm, SWA_Q_W), lambda i, j: (i, 0)),
            *gate_specs(COL_GR),
            *gate_specs(COL_GS),
            pl.BlockSpec((RET_V_W, tn), lambda i, j: (0, j)),
            pl.BlockSpec((SWA_Q_W, tn), lambda i, j: (0, j)),
            pl.BlockSpec((tn, D_MODEL), lambda i, j: (j, 0)),
            pl.BlockSpec((tm, D_MODEL), lambda i, j: (i, 0)),
            pl.BlockSpec((1, D_MODEL), lambda i, j: (0, 0)),
        ],
        out_specs=pl.BlockSpec((tm, D_MODEL), lambda i, j: (i, 0)),
        out_shape=jax.ShapeDtypeStruct((T, D_MODEL), f32),
        compiler_params=_params(("parallel", "arbitrary")),
        name="mix",
    )(ret, swa, *([proj] * (2 * n_gate)), w_ret_o, w_swa_o, w_out, x2, g)


def _gelu_tanh(x):
    return x * (0.5 * (1.0 + jnp.tanh(0.7978845608028654 * (x + 0.044715 * (x * x * x)))))


def _ffn_kernel(x_ref, halo_ref, gpre_ref, wv_ref, wg_ref, cwv_ref, cwg_ref, cbv_ref, cbg_ref, wd_ref, gpost_ref,
                o_ref, h_scr, *u_scrs, tiles_per_seq):
    i = pl.program_id(0)
    j = pl.program_id(1)
    tm = x_ref.shape[0]
    tn = wv_ref.shape[1]
    H = FFN_HALO

    @pl.when(j == 0)
    def _():
        keep = jnp.where(i % tiles_per_seq == 0, 0.0, 1.0)
        h_scr[0:H, :] = (_rmsnorm(halo_ref[...], gpre_ref[...]) * keep).astype(bf16)
        _norm_rows_to(x_ref, gpre_ref, h_scr, H, tm)
        o_ref[...] = jnp.zeros_like(o_ref)

    def conv(u_scr, row0, nrows, u_cols, w_cols, cw_ref, cb_ref):
        y = cb_ref[:, w_cols]
        for kk in range(CONV_WIDTH):
            y = y + u_scr[pl.ds(H - (CONV_WIDTH - 1) + kk + row0, nrows), u_cols] * cw_ref[kk:kk + 1, w_cols]
        return y

    h = h_scr[...]
    hm = tm // FFN_ROW_SPLIT
    parts = [None] * FFN_ROW_SPLIT
    uv = slice(0, MXU_COLS)
    ug = slice(MXU_COLS, 2 * MXU_COLS)
    for c, u_scr in enumerate(u_scrs):
        cs = slice(c * MXU_COLS, (c + 1) * MXU_COLS)
        u_scr[:, uv] = jnp.dot(h, wv_ref[:, cs], preferred_element_type=f32)
        u_scr[:, ug] = jnp.dot(h, wg_ref[:, cs], preferred_element_type=f32)
        for m in range(FFN_ROW_SPLIT):
            val = conv(u_scr, m * hm, hm, uv, cs, cwv_ref, cbv_ref)
            gate = conv(u_scr, m * hm, hm, ug, cs, cwg_ref, cbg_ref)
            a = (_gelu_tanh(gate) * val).astype(bf16)
            d = jnp.dot(a, wd_ref[cs, :], preferred_element_type=f32)
            parts[m] = d if parts[m] is None else parts[m] + d
    for m in range(FFN_ROW_SPLIT):
        o_ref[m * hm:(m + 1) * hm, :] += parts[m]

    @pl.when(j == pl.num_programs(1) - 1)
    def _():
        _residual_norm_rows(x_ref, o_ref, gpost_ref, o_ref, tm)


def _ffn(x1, g_pre, w_up, conv_w, conv_b, w_down, g_post, seq):
    T = x1.shape[0]
    tm, tn = min(FFN_TM, seq), FFN_TN
    nj = D_FF // tn
    hb = tm // FFN_HALO
    kern = functools.partial(_ffn_kernel, tiles_per_seq=seq // tm)
    return pl.pallas_call(
        kern,
        grid=(T // tm, nj),
        in_specs=[
            pl.BlockSpec((tm, D_MODEL), lambda i, j: (i, 0)),
            pl.BlockSpec((FFN_HALO, D_MODEL), lambda i, j: (jnp.maximum(i * hb - 1, 0), 0)),
            pl.BlockSpec((1, D_MODEL), lambda i, j: (0, 0)),
            pl.BlockSpec((D_MODEL, tn), lambda i, j: (0, j)),
            pl.BlockSpec((D_MODEL, tn), lambda i, j: (0, nj + j)),
            pl.BlockSpec((CONV_WIDTH, tn), lambda i, j: (0, j)),
            pl.BlockSpec((CONV_WIDTH, tn), lambda i, j: (0, nj + j)),
            pl.BlockSpec((1, tn), lambda i, j: (0, j)),
            pl.BlockSpec((1, tn), lambda i, j: (0, nj + j)),
            pl.BlockSpec((tn, D_MODEL), lambda i, j: (j, 0)),
            pl.BlockSpec((1, D_MODEL), lambda i, j: (0, 0)),
        ],
        out_specs=pl.BlockSpec((tm, D_MODEL), lambda i, j: (i, 0)),
        out_shape=jax.ShapeDtypeStruct((T, D_MODEL), f32),
        scratch_shapes=[
            pltpu.VMEM((tm + FFN_HALO, D_MODEL), bf16),
        ] + [pltpu.VMEM((tm + FFN_HALO, 2 * MXU_COLS), f32)] * (tn // MXU_COLS),
        compiler_params=_params(("parallel", "arbitrary")),
        name="ffn",
    )(x1, x1, g_pre, w_up, w_up, conv_w, conv_w, conv_b, conv_b, w_down, g_post)


def _rotary_tables(seq):
    d = RET_DK
    inv = 1.0 / (ROPE_BASE ** (jnp.arange(0, d, 2, dtype=f32) / d))
    ang = jnp.arange(seq, dtype=jnp.int32).astype(f32)[:, None] * inv[None, :]
    cos, sin = jnp.cos(ang), jnp.sin(ang)
    tab = jnp.concatenate([cos, cos, -sin, sin], axis=-1)
    return jnp.stack([tab, tab * (d ** -0.5)])


def kernel(x, g_pre_mix, w_in, w_ret_o, w_swa_o, w_out, swa_sinks, g_post_mix, g_pre_ffn, w_up, conv_w, conv_b,
           w_down, g_post_ffn):
    B, S, D = x.shape
    depth = w_in.shape[0]
    rot = _rotary_tables(S)
    x2 = x.reshape(B * S, D)
    for l in range(depth):
        proj = _inproj(x2, g_pre_mix[l][None], w_in[l].astype(bf16), rot, S)
        ret = _retention(proj, B, S)
        swa = _swa(proj, swa_sinks[l].astype(f32), B, S)
        x2 = _mix(ret, swa, proj, w_ret_o[l].astype(bf16), w_swa_o[l].astype(bf16), w_out[l].astype(bf16), x2,
                  g_post_mix[l][None])
        x2 = _ffn(x2, g_pre_ffn[l][None], w_up[l].astype(bf16), conv_w[l], conv_b[l][None], w_down[l].astype(bf16),
                  g_post_ffn[l][None], S)
    return x2.reshape(B, S, D)
```

```python
import functools

import jax
import jax.numpy as jnp
from jax import lax
from jax.experimental import pallas as pl
from jax.experimental.pallas import tpu as pltpu

D_MODEL = 2048
RET_HEADS = 8
RET_DK = 128
RET_DV = 256
RET_CHUNK = 128
ROPE_BASE = 10000.0
SWA_Q_HEADS = 16
SWA_KV_HEADS = 4
SWA_HEAD_DIM = 64
SWA_BLOCK = 128
D_FF = 5632
CONV_WIDTH = 3
RMS_EPS = 1e-6

RET_QK_W = RET_HEADS * RET_DK
RET_V_W = RET_HEADS * RET_DV
SWA_Q_W = SWA_Q_HEADS * SWA_HEAD_DIM
SWA_KV_W = SWA_KV_HEADS * SWA_HEAD_DIM

COL_RQ = 0
COL_RK = COL_RQ + RET_QK_W
COL_RV = COL_RK + RET_QK_W
COL_RG = COL_RV + RET_V_W
COL_SQ = COL_RG + RET_V_W
COL_SK = COL_SQ + SWA_Q_W
COL_SV = COL_SK + SWA_KV_W
COL_GR = COL_SV + SWA_KV_W
COL_GS = COL_GR + D_MODEL
IN_WIDTH = COL_GS + D_MODEL

V7X_VMEM_LIMIT_BYTES = 60 * 1024 * 1024
BF16_SUBLANES = 16
MXU_COLS = 256
NORM_ROWS = 32
NORM_UNROLL = 4

INPROJ_TM = 1024
INPROJ_TN = 1024
MIX_TM = 512
MIX_TN = 1024
MIX_GATE_W = 512
FFN_TM = 1024
FFN_TN = 512
FFN_HALO = BF16_SUBLANES
FFN_ROW_SPLIT = 4
INPROJ_ROW_SPLIT = 4

f32 = jnp.float32
bf16 = jnp.bfloat16

def _params(semantics):
    return pltpu.CompilerParams(dimension_semantics=semantics, vmem_limit_bytes=V7X_VMEM_LIMIT_BYTES)


def _rmsnorm(x, g):
    ms = jnp.mean(x * x, axis=-1, keepdims=True)
    return x * lax.rsqrt(ms + RMS_EPS) * g


def _norm_rows_to(x_ref, g_ref, dst_ref, dst_off, rows):
    g = g_ref[...]
    group = NORM_ROWS * NORM_UNROLL

    def step(c, carry):
        base = pl.multiple_of(c * group, group)
        for k in range(NORM_UNROLL):
            r = base + k * NORM_ROWS
            dst_ref[pl.ds(dst_off + r, NORM_ROWS), :] = (
                _rmsnorm(x_ref[pl.ds(r, NORM_ROWS), :], g).astype(dst_ref.dtype))
        return carry

    lax.fori_loop(0, rows // group, step, 0)


def _residual_norm_rows(x_ref, y_ref, g_ref, o_ref, rows):
    g = g_ref[...]
    group = NORM_ROWS * NORM_UNROLL

    def step(c, carry):
        base = pl.multiple_of(c * group, group)
        rows_k = [pl.ds(base + k * NORM_ROWS, NORM_ROWS) for k in range(NORM_UNROLL)]
        scales = []
        for r in rows_k:
            y = y_ref[r, :]
            scales.append(lax.rsqrt(jnp.mean(y * y, axis=-1, keepdims=True) + RMS_EPS))
        for r, s in zip(rows_k, scales):
            o_ref[r, :] = x_ref[r, :] + y_ref[r, :] * s * g
        return carry

    lax.fori_loop(0, rows // group, step, 0)


def _inproj_tile_kinds():
    starts = ((COL_RQ, "rot"), (COL_RK, "rot"), (COL_RV, "cast"), (COL_RG, "silu"), (COL_SQ, "cast"),
              (COL_GR, "sig"))
    chunk_kinds = [[k for c0, k in starts if c0 <= col][-1] for col in range(0, IN_WIDTH, MXU_COLS)]
    per_tile = INPROJ_TN // MXU_COLS
    return [tuple(chunk_kinds[t:t + per_tile]) for t in range(0, len(chunk_kinds), per_tile)]


_INPROJ_TILE_KINDS = _inproj_tile_kinds()
_T_RK = COL_RK // INPROJ_TN
assert COL_RK % INPROJ_TN == 0 and COL_RV % INPROJ_TN == 0


def _inproj_kernel(x_ref, g_ref, w_ref, rot_ref, o_ref, h_scr):
    j = pl.program_id(1)
    tm = x_ref.shape[0]

    @pl.when(j == 0)
    def _():
        _norm_rows_to(x_ref, g_ref, h_scr, 0, tm)

    def rotary(acc, rows):
        cos = rot_ref[rows, 0:RET_DK]
        sin = rot_ref[rows, RET_DK:2 * RET_DK]
        heads = []
        for hh in range(acc.shape[1] // RET_DK):
            xh = acc[:, hh * RET_DK:(hh + 1) * RET_DK]
            heads.append(xh * cos + pltpu.roll(xh, RET_DK // 2, axis=1) * sin)
        return jnp.concatenate(heads, axis=1)

    epilogues = {
        "rot": rotary,
        "cast": lambda acc, rows: acc,
        "silu": lambda acc, rows: acc * jax.nn.sigmoid(acc),
        "sig": lambda acc, rows: jax.nn.sigmoid(acc),
    }

    def tile(kinds):
        hm = tm // INPROJ_ROW_SPLIT
        for c, kind in enumerate(kinds):
            cs = slice(c * MXU_COLS, (c + 1) * MXU_COLS)
            w = w_ref[:, cs]
            for m in range(INPROJ_ROW_SPLIT):
                rows = slice(m * hm, (m + 1) * hm)
                acc = jnp.dot(h_scr[rows, :], w, preferred_element_type=f32)
                o_ref[rows, cs] = epilogues[kind](acc, rows).astype(o_ref.dtype)

    for kinds in sorted(set(_INPROJ_TILE_KINDS)):
        tiles = [t for t, k in enumerate(_INPROJ_TILE_KINDS) if k == kinds]
        cond = functools.reduce(lambda a, b: a | b, [j == t for t in tiles])
        pl.when(cond)(functools.partial(tile, kinds))


def _inproj(x2, g, w_in, rot, seq):
    T = x2.shape[0]
    tm, tn = INPROJ_TM, INPROJ_TN
    tm = min(tm, seq)
    spt = seq // tm
    grid = (T // tm, len(_INPROJ_TILE_KINDS))
    return pl.pallas_call(
        _inproj_kernel,
        grid=grid,
        in_specs=[
            pl.BlockSpec((tm, D_MODEL), lambda i, j: (i, 0)),
            pl.BlockSpec((1, D_MODEL), lambda i, j: (0, 0)),
            pl.BlockSpec((D_MODEL, tn), lambda i, j: (0, j)),
            pl.BlockSpec((None, tm, 2 * RET_DK), lambda i, j: (jnp.where(j >= _T_RK, 1, 0), i % spt, 0)),
        ],
        out_specs=pl.BlockSpec((tm, tn), lambda i, j: (i, j)),
        out_shape=jax.ShapeDtypeStruct((T, IN_WIDTH), bf16),
        scratch_shapes=[pltpu.VMEM((tm, D_MODEL), bf16)],
        compiler_params=_params(("parallel", "arbitrary")),
        name="inproj",
    )(x2, g, w_in, rot)


def _retention_kernel(q_ref, k_ref, v_ref, gate_ref, dmat_ref, xi_ref, zeta_ref, cdec_ref, o_ref, kv_scr, st_scr):
    S = q_ref.shape[0]
    C = RET_CHUNK
    N = S // C
    q3 = q_ref[...].reshape(N, C, RET_DK)
    k3 = k_ref[...].reshape(N, C, RET_DK)
    v3 = v_ref[...].reshape(N, C, RET_DV)

    kz = (k3.astype(f32) * zeta_ref[...][None]).astype(bf16)
    kzt = jnp.swapaxes(kz, 1, 2)
    kv_scr[...] = jnp.einsum("ndk,nkv->ndv", kzt, v3, preferred_element_type=f32)

    cdec = cdec_ref[...]

    def scan(n, state):
        st_scr[n] = state.astype(bf16)
        return state * cdec + kv_scr[n]

    lax.fori_loop(0, N, scan, jnp.zeros((RET_DK, RET_DV), f32))

    sc = jnp.einsum("nqd,nkd->nqk", q3, k3, preferred_element_type=f32) * dmat_ref[...][None]
    inner = jnp.einsum("nqk,nkv->nqv", sc.astype(bf16), v3, preferred_element_type=f32)
    cross = jnp.einsum("nqd,ndv->nqv", q3, st_scr[...], preferred_element_type=f32) * xi_ref[...][None]
    o = inner + cross
    o = o * lax.rsqrt(jnp.mean(o * o, axis=-1, keepdims=True) + RMS_EPS)
    o = gate_ref[...].astype(f32).reshape(N, C, RET_DV) * o
    o_ref[...] = o.reshape(S, RET_DV).astype(o_ref.dtype)


def _retention_tables():
    C = RET_CHUNK
    log_gamma = jnp.log(1.0 - 2.0 ** (-5.0 - jnp.arange(RET_HEADS, dtype=f32)))
    idx = jnp.arange(C, dtype=f32)
    rel = idx[:, None] - idx[None, :]
    dmat = jnp.where(rel[None] >= 0, jnp.exp(log_gamma[:, None, None] * jnp.maximum(rel, 0.0)[None]), 0.0)
    xi = jnp.exp(log_gamma[:, None] * (idx + 1.0))
    zeta = jnp.exp(log_gamma[:, None] * (C - 1.0 - idx))
    cdec = jnp.exp(log_gamma * C)
    xi_b = jnp.broadcast_to(xi[:, :, None], (RET_HEADS, C, RET_DV))
    zeta_b = jnp.broadcast_to(zeta[:, :, None], (RET_HEADS, C, RET_DK))
    cdec_b = jnp.broadcast_to(cdec[:, None, None], (RET_HEADS, 1, RET_DV))
    return dmat.astype(f32), xi_b.astype(f32), zeta_b.astype(f32), cdec_b.astype(f32)


def _retention(proj, batch, seq):
    T = proj.shape[0]
    C = RET_CHUNK
    N = seq // C
    dmat, xi_b, zeta_b, cdec_b = _retention_tables()
    return pl.pallas_call(
        _retention_kernel,
        grid=(batch, RET_HEADS),
        in_specs=[
            pl.BlockSpec((seq, RET_DK), lambda b, h: (b, COL_RQ // RET_DK + h)),
            pl.BlockSpec((seq, RET_DK), lambda b, h: (b, COL_RK // RET_DK + h)),
            pl.BlockSpec((seq, RET_DV), lambda b, h: (b, COL_RV // RET_DV + h)),
            pl.BlockSpec((seq, RET_DV), lambda b, h: (b, COL_RG // RET_DV + h)),
            pl.BlockSpec((None, C, C), lambda b, h: (h, 0, 0)),
            pl.BlockSpec((None, C, RET_DV), lambda b, h: (h, 0, 0)),
            pl.BlockSpec((None, C, RET_DK), lambda b, h: (h, 0, 0)),
            pl.BlockSpec((None, 1, RET_DV), lambda b, h: (h, 0, 0)),
        ],
        out_specs=pl.BlockSpec((seq, RET_DV), lambda b, h: (b, h)),
        out_shape=jax.ShapeDtypeStruct((T, RET_V_W), bf16),
        scratch_shapes=[pltpu.VMEM((N, RET_DK, RET_DV), f32), pltpu.VMEM((N, RET_DK, RET_DV), bf16)],
        compiler_params=_params(("parallel", "parallel")),
        name="retention",
    )(proj, proj, proj, proj, dmat, xi_b, zeta_b, cdec_b)


KV_PER_STEP = 2
Q_PER_KV = SWA_Q_HEADS // SWA_KV_HEADS
SWA_Q_STEP_W = KV_PER_STEP * Q_PER_KV * SWA_HEAD_DIM
LANES = 128
SWA_UNROLL = 4


def _swa_kernel(sink_ref, bias_ref, q_ref, k_ref, v_ref, o_ref, klo, khi, vlo, vhi):
    S = q_ref.shape[0]
    C = SWA_BLOCK
    gp = pl.program_id(1)
    half = SWA_HEAD_DIM

    lane_s = lax.broadcasted_iota(jnp.int32, (S, LANES), 1)
    low_s = lane_s < half

    def prep(src_ref, lo_scr, hi_scr, scale, with_ones):
        w = src_ref[...].astype(f32) * scale
        r = pltpu.roll(w, half, axis=1)
        zeros = jnp.zeros((C, lo_scr.shape[2]), bf16)
        for t in range(KV_PER_STEP):
            lo_scr[t, 0:C, :] = zeros
            hi_scr[t, 0:C, :] = zeros
        lo_scr[0, C:C + S, 0:LANES] = jnp.where(low_s, w, 0.0).astype(bf16)
        hi_scr[0, C:C + S, 0:LANES] = jnp.where(low_s, 0.0, r).astype(bf16)
        lo_scr[1, C:C + S, 0:LANES] = jnp.where(low_s, r, 0.0).astype(bf16)
        hi_scr[1, C:C + S, 0:LANES] = jnp.where(low_s, 0.0, w).astype(bf16)
        if with_ones:
            for t in range(KV_PER_STEP):
                lo_scr[t, C:C + S, LANES:2 * LANES] = jnp.where(low_s, 1.0, 0.0).astype(bf16)
                hi_scr[t, C:C + S, LANES:2 * LANES] = jnp.where(low_s, 0.0, 1.0).astype(bf16)

    prep(k_ref, klo, khi, SWA_HEAD_DIM ** -0.5, False)
    prep(v_ref, vlo, vhi, 1.0, True)

    rows = 2 * C
    win = 2 * C
    first_pair = lax.broadcasted_iota(jnp.int32, (rows, 1), 0) < C
    low_o = lax.broadcasted_iota(jnp.int32, (rows, LANES), 1) < half
    nt = (((1,), (1,)), ((), ()))

    def softmax_parts(s, sink, bias):
        s = s + bias
        m = jnp.maximum(jnp.max(s, axis=-1, keepdims=True), sink)
        return jnp.exp(s - m).astype(bf16), jnp.exp(sink - m)

    def block(n, carry):
        r0 = pl.multiple_of(n * C, C)
        bias = bias_ref[jnp.minimum(n, 1)]
        for t in range(KV_PER_STEP):
            c0 = t * Q_PER_KV * SWA_HEAD_DIM
            hbase = gp * (KV_PER_STEP * Q_PER_KV) + t * Q_PER_KV
            qs = jnp.concatenate([q_ref[pl.ds(r0, C), c0:c0 + LANES],
                                  q_ref[pl.ds(r0, C), c0 + LANES:c0 + 2 * LANES]], axis=0)
            s_e = lax.dot_general(qs, klo[t, pl.ds(r0, win), :], nt, preferred_element_type=f32)
            s_o = lax.dot_general(qs, khi[t, pl.ds(r0, win), :], nt, preferred_element_type=f32)
            sink_e = jnp.where(first_pair, sink_ref[hbase + 0], sink_ref[hbase + 2])
            sink_o = jnp.where(first_pair, sink_ref[hbase + 1], sink_ref[hbase + 3])
            p_e, z_e = softmax_parts(s_e, sink_e, bias)
            p_o, z_o = softmax_parts(s_o, sink_o, bias)
            pv = (jnp.dot(p_e, vlo[t, pl.ds(r0, win), :], preferred_element_type=f32)
                  + jnp.dot(p_o, vhi[t, pl.ds(r0, win), :], preferred_element_type=f32))
            den = pv[:, LANES:2 * LANES] + jnp.where(low_o, z_e, z_o)
            o = pv[:, 0:LANES] / den
            o_ref[pl.ds(r0, C), c0:c0 + LANES] = o[0:C].astype(o_ref.dtype)
            o_ref[pl.ds(r0, C), c0 + LANES:c0 + 2 * LANES] = o[C:2 * C].astype(o_ref.dtype)
        return carry

    lax.fori_loop(0, S // C, block, 0, unroll=SWA_UNROLL)


def _swa_bias():
    C = SWA_BLOCK
    row_i = jnp.arange(2 * C)[:, None] % C
    col_j = jnp.arange(2 * C)[None, :]
    band = (col_j > row_i) & (col_j <= row_i + C)
    first = band & (col_j >= C)
    return jnp.where(jnp.stack([first, band]), 0.0, -jnp.inf).astype(f32)


def _swa(proj, sinks, batch, seq):
    T = proj.shape[0]
    steps = SWA_KV_HEADS // KV_PER_STEP
    k_scr = pltpu.VMEM((KV_PER_STEP, seq + SWA_BLOCK, LANES), bf16)
    v_scr = pltpu.VMEM((KV_PER_STEP, seq + SWA_BLOCK, 2 * LANES), bf16)
    bias = _swa_bias()
    return pl.pallas_call(
        _swa_kernel,
        grid=(batch, steps),
        in_specs=[
            pl.BlockSpec(memory_space=pltpu.SMEM),
            pl.BlockSpec(bias.shape, lambda b, g: (0, 0, 0)),
            pl.BlockSpec((seq, SWA_Q_STEP_W), lambda b, g: (b, COL_SQ // SWA_Q_STEP_W + g)),
            pl.BlockSpec((seq, LANES), lambda b, g: (b, COL_SK // LANES + g)),
            pl.BlockSpec((seq, LANES), lambda b, g: (b, COL_SV // LANES + g)),
        ],
        out_specs=pl.BlockSpec((seq, SWA_Q_STEP_W), lambda b, g: (b, g)),
        out_shape=jax.ShapeDtypeStruct((T, SWA_Q_W), bf16),
        scratch_shapes=[k_scr, k_scr, v_scr, v_scr],
        compiler_params=_params(("parallel", "parallel")),
        name="swa",
    )(sinks, bias, proj, proj, proj)


def _mix_kernel(ret_ref, swa_ref, *refs):
    n_gate = MIX_TN // MIX_GATE_W
    gr_refs, gs_refs = refs[:n_gate], refs[n_gate:2 * n_gate]
    wr_ref, ws_ref, wo_ref, x_ref, g_ref, o_ref = refs[2 * n_gate:]
    j = pl.program_id(1)

    @pl.when(j == 0)
    def _():
        o_ref[...] = jnp.zeros_like(o_ref)

    part = None
    for c in range(n_gate):
        cs = slice(c * MIX_GATE_W, (c + 1) * MIX_GATE_W)
        ret_out = jnp.dot(ret_ref[...], wr_ref[:, cs], preferred_element_type=f32)
        swa_out = jnp.dot(swa_ref[...], ws_ref[:, cs], preferred_element_type=f32)
        mixed = gr_refs[c][...].astype(f32) * ret_out + gs_refs[c][...].astype(f32) * swa_out
        d = jnp.dot(mixed.astype(bf16), wo_ref[cs, :], preferred_element_type=f32)
        part = d if part is None else part + d
    o_ref[...] += part

    @pl.when(j == pl.num_programs(1) - 1)
    def _():
        _residual_norm_rows(x_ref, o_ref, g_ref, o_ref, o_ref.shape[0])


def _mix(ret, swa, proj, w_ret_o, w_swa_o, w_out, x2, g):
    T = x2.shape[0]
    tm, tn = min(MIX_TM, T), MIX_TN
    nj = D_MODEL // tn
    gw = MIX_GATE_W
    n_gate = tn // gw

    def gate_specs(col0):
        return [pl.BlockSpec((tm, gw), functools.partial(lambda i, j, c: (i, col0 // gw + j * n_gate + c), c=c))
                for c in range(n_gate)]

    return pl.pallas_call(
        _mix_kernel,
        grid=(T // tm, nj),
        in_specs=[
            pl.BlockSpec((tm, RET_V_W), lambda i, j: (i, 0)),
            pl.BlockSpec((tm, SWA_Q_W), lambda i, j: (i, 0)),
            *gate_specs(COL_GR),
            *gate_specs(COL_GS),
            pl.BlockSpec((RET_V_W, tn), lambda i, j: (0, j)),
            pl.BlockSpec((SWA_Q_W, tn), lambda i, j: (0, j)),
            pl.BlockSpec((tn, D_MODEL), lambda i, j: (j, 0)),
            pl.BlockSpec((tm, D_MODEL), lambda i, j: (i, 0)),
            pl.BlockSpec((1, D_MODEL), lambda i, j: (0, 0)),
        ],
        out_specs=pl.BlockSpec((tm, D_MODEL), lambda i, j: (i, 0)),
        out_shape=jax.ShapeDtypeStruct((T, D_MODEL), f32),
        compiler_params=_params(("parallel", "arbitrary")),
        name="mix",
    )(ret, swa, *([proj] * (2 * n_gate)), w_ret_o, w_swa_o, w_out, x2, g)


def _gelu_tanh(x):
    return x * (0.5 * (1.0 + jnp.tanh(0.7978845608028654 * (x + 0.044715 * (x * x * x)))))


def _ffn_kernel(x_ref, halo_ref, gpre_ref, wv_ref, wg_ref, cwv_ref, cwg_ref, cbv_ref, cbg_ref, wd_ref, gpost_ref,
                o_ref, h_scr, *u_scrs, tiles_per_seq):
    i = pl.program_id(0)
    j = pl.program_id(1)
    tm = x_ref.shape[0]
    tn = wv_ref.shape[1]
    H = FFN_HALO

    @pl.when(j == 0)
    def _():
        keep = jnp.where(i % tiles_per_seq == 0, 0.0, 1.0)
        h_scr[0:H, :] = (_rmsnorm(halo_ref[...], gpre_ref[...]) * keep).astype(bf16)
        _norm_rows_to(x_ref, gpre_ref, h_scr, H, tm)
        o_ref[...] = jnp.zeros_like(o_ref)

    def conv(u_scr, row0, nrows, u_cols, w_cols, cw_ref, cb_ref):
        y = cb_ref[:, w_cols]
        for kk in range(CONV_WIDTH):
            y = y + u_scr[pl.ds(H - (CONV_WIDTH - 1) + kk + row0, nrows), u_cols] * cw_ref[kk:kk + 1, w_cols]
        return y

    h = h_scr[...]
    hm = tm // FFN_ROW_SPLIT
    parts = [None] * FFN_ROW_SPLIT
    uv = slice(0, MXU_COLS)
    ug = slice(MXU_COLS, 2 * MXU_COLS)
    n_chunks = len(u_scrs)
    for c, u_scr in enumerate(u_scrs):
        cs = slice(c * MXU_COLS, (c + 1) * MXU_COLS)
        if c < n_chunks - 1:
            u_scr[:, uv] = jnp.dot(h, wv_ref[:, cs], preferred_element_type=f32)
            u_scr[:, ug] = jnp.dot(h, wg_ref[:, cs], preferred_element_type=f32)
        else:
            wv, wg = wv_ref[:, cs], wg_ref[:, cs]
            r0 = 0
            for m in range(FFN_ROW_SPLIT):
                r1 = H + (m + 1) * hm
                u_scr[r0:r1, uv] = jnp.dot(h_scr[r0:r1, :], wv, preferred_element_type=f32)
                u_scr[r0:r1, ug] = jnp.dot(h_scr[r0:r1, :], wg, preferred_element_type=f32)
                r0 = r1
        for m in range(FFN_ROW_SPLIT):
            val = conv(u_scr, m * hm, hm, uv, cs, cwv_ref, cbv_ref)
            gate = conv(u_scr, m * hm, hm, ug, cs, cwg_ref, cbg_ref)
            a = (_gelu_tanh(gate) * val).astype(bf16)
            d = jnp.dot(a, wd_ref[cs, :], preferred_element_type=f32)
            parts[m] = d if parts[m] is None else parts[m] + d
    for m in range(FFN_ROW_SPLIT):
        o_ref[m * hm:(m + 1) * hm, :] += parts[m]

    @pl.when(j == pl.num_programs(1) - 1)
    def _():
        _residual_norm_rows(x_ref, o_ref, gpost_ref, o_ref, tm)


def _ffn(x1, g_pre, w_up, conv_w, conv_b, w_down, g_post, seq):
    T = x1.shape[0]
    tm, tn = min(FFN_TM, seq), FFN_TN
    nj = D_FF // tn
    hb = tm // FFN_HALO
    kern = functools.partial(_ffn_kernel, tiles_per_seq=seq // tm)
    return pl.pallas_call(
        kern,
        grid=(T // tm, nj),
        in_specs=[
            pl.BlockSpec((tm, D_MODEL), lambda i, j: (i, 0)),
            pl.BlockSpec((FFN_HALO, D_MODEL), lambda i, j: (jnp.maximum(i * hb - 1, 0), 0)),
            pl.BlockSpec((1, D_MODEL), lambda i, j: (0, 0)),
            pl.BlockSpec((D_MODEL, tn), lambda i, j: (0, j)),
            pl.BlockSpec((D_MODEL, tn), lambda i, j: (0, nj + j)),
            pl.BlockSpec((CONV_WIDTH, tn), lambda i, j: (0, j)),
            pl.BlockSpec((CONV_WIDTH, tn), lambda i, j: (0, nj + j)),
            pl.BlockSpec((1, tn), lambda i, j: (0, j)),
            pl.BlockSpec((1, tn), lambda i, j: (0, nj + j)),
            pl.BlockSpec((tn, D_MODEL), lambda i, j: (j, 0)),
            pl.BlockSpec((1, D_MODEL), lambda i, j: (0, 0)),
        ],
        out_specs=pl.BlockSpec((tm, D_MODEL), lambda i, j: (i, 0)),
        out_shape=jax.ShapeDtypeStruct((T, D_MODEL), f32),
        scratch_shapes=[
            pltpu.VMEM((tm + FFN_HALO, D_MODEL), bf16),
        ] + [pltpu.VMEM((tm + FFN_HALO, 2 * MXU_COLS), f32)] * (tn // MXU_COLS),
        compiler_params=_params(("parallel", "arbitrary")),
        name="ffn",
    )(x1, x1, g_pre, w_up, w_up, conv_w, conv_w, conv_b, conv_b, w_down, g_post)


def _rotary_tables(seq):
    d = RET_DK
    inv = 1.0 / (ROPE_BASE ** (jnp.arange(0, d, 2, dtype=f32) / d))
    ang = jnp.arange(seq, dtype=jnp.int32).astype(f32)[:, None] * inv[None, :]
    cos, sin = jnp.cos(ang), jnp.sin(ang)
    tab = jnp.concatenate([cos, cos, -sin, sin], axis=-1)
    return jnp.stack([tab, tab * (d ** -0.5)])


def kernel(x, g_pre_mix, w_in, w_ret_o, w_swa_o, w_out, swa_sinks, g_post_mix, g_pre_ffn, w_up, conv_w, conv_b,
           w_down, g_post_ffn):
    B, S, D = x.shape
    depth = w_in.shape[0]
    rot = _rotary_tables(S)
    x2 = x.reshape(B * S, D)
    for l in range(depth):
        proj = _inproj(x2, g_pre_mix[l][None], w_in[l].astype(bf16), rot, S)
        ret = _retention(proj, B, S)
        swa = _swa(proj, swa_sinks[l].astype(f32), B, S)
        x2 = _mix(ret, swa, proj, w_ret_o[l].astype(bf16), w_swa_o[l].astype(bf16), w_out[l].astype(bf16), x2,
                  g_post_mix[l][None])
        x2 = _ffn(x2, g_pre_ffn[l][None], w_up[l].astype(bf16), conv_w[l], conv_b[l][None], w_down[l].astype(bf16),
                  g_post_ffn[l][None], S)
    return x2.reshape(B, S, D)
```

```python
import functools

import jax
import jax.numpy as jnp
from jax import lax
from jax.experimental import pallas as pl
from jax.experimental.pallas import tpu as pltpu

D_MODEL = 2048
RET_HEADS = 8
RET_DK = 128
RET_DV = 256
RET_CHUNK = 128
ROPE_BASE = 10000.0
SWA_Q_HEADS = 16
SWA_KV_HEADS = 4
SWA_HEAD_DIM = 64
SWA_BLOCK = 128
D_FF = 5632
CONV_WIDTH = 3
RMS_EPS = 1e-6

RET_QK_W = RET_HEADS * RET_DK
RET_V_W = RET_HEADS * RET_DV
SWA_Q_W = SWA_Q_HEADS * SWA_HEAD_DIM
SWA_KV_W = SWA_KV_HEADS * SWA_HEAD_DIM

COL_RQ = 0
COL_RK = COL_RQ + RET_QK_W
COL_RV = COL_RK + RET_QK_W
COL_RG = COL_RV + RET_V_W
COL_SQ = COL_RG + RET_V_W
COL_SK = COL_SQ + SWA_Q_W
COL_SV = COL_SK + SWA_KV_W
COL_GR = COL_SV + SWA_KV_W
COL_GS = COL_GR + D_MODEL
IN_WIDTH = COL_GS + D_MODEL

V7X_VMEM_LIMIT_BYTES = 60 * 1024 * 1024
BF16_SUBLANES = 16
MXU_COLS = 256
NORM_ROWS = 32
NORM_UNROLL = 4

INPROJ_TM = 1024
INPROJ_TN = 1024
MIX_TM = 512
MIX_TN = 1024
MIX_GATE_W = 512
FFN_TM = 1024
FFN_TN = 512
FFN_HALO = BF16_SUBLANES
FFN_ROW_SPLIT = 4
INPROJ_ROW_SPLIT = 4
RET_HEADS_PER_STEP = 2

f32 = jnp.float32
bf16 = jnp.bfloat16

def _params(semantics):
    return pltpu.CompilerParams(dimension_semantics=semantics, vmem_limit_bytes=V7X_VMEM_LIMIT_BYTES)


def _rmsnorm(x, g):
    ms = jnp.mean(x * x, axis=-1, keepdims=True)
    return x * lax.rsqrt(ms + RMS_EPS) * g


def _norm_rows_to(x_ref, g_ref, dst_ref, dst_off, rows):
    g = g_ref[...]
    group = NORM_ROWS * NORM_UNROLL

    def step(c, carry):
        base = pl.multiple_of(c * group, group)
        for k in range(NORM_UNROLL):
            r = base + k * NORM_ROWS
            dst_ref[pl.ds(dst_off + r, NORM_ROWS), :] = (
                _rmsnorm(x_ref[pl.ds(r, NORM_ROWS), :], g).astype(dst_ref.dtype))
        return carry

    lax.fori_loop(0, rows // group, step, 0)


def _residual_norm_rows(x_ref, y_ref, g_ref, o_ref, rows):
    g = g_ref[...]
    group = NORM_ROWS * NORM_UNROLL

    def step(c, carry):
        base = pl.multiple_of(c * group, group)
        rows_k = [pl.ds(base + k * NORM_ROWS, NORM_ROWS) for k in range(NORM_UNROLL)]
        scales = []
        for r in rows_k:
            y = y_ref[r, :]
            scales.append(lax.rsqrt(jnp.mean(y * y, axis=-1, keepdims=True) + RMS_EPS))
        for r, s in zip(rows_k, scales):
            o_ref[r, :] = x_ref[r, :] + y_ref[r, :] * s * g
        return carry

    lax.fori_loop(0, rows // group, step, 0)


def _inproj_tile_kinds():
    starts = ((COL_RQ, "rot"), (COL_RK, "rot"), (COL_RV, "cast"), (COL_RG, "silu"), (COL_SQ, "cast"),
              (COL_GR, "sig"))
    chunk_kinds = [[k for c0, k in starts if c0 <= col][-1] for col in range(0, IN_WIDTH, MXU_COLS)]
    per_tile = INPROJ_TN // MXU_COLS
    return [tuple(chunk_kinds[t:t + per_tile]) for t in range(0, len(chunk_kinds), per_tile)]


_INPROJ_TILE_KINDS = _inproj_tile_kinds()
_T_RK = COL_RK // INPROJ_TN
assert COL_RK % INPROJ_TN == 0 and COL_RV % INPROJ_TN == 0


def _inproj_kernel(x_ref, g_ref, w_ref, rot_ref, o_ref, h_scr):
    j = pl.program_id(1)
    tm = x_ref.shape[0]

    @pl.when(j == 0)
    def _():
        _norm_rows_to(x_ref, g_ref, h_scr, 0, tm)

    def rotary(acc, rows):
        cos = rot_ref[rows, 0:RET_DK]
        sin = rot_ref[rows, RET_DK:2 * RET_DK]
        heads = []
        for hh in range(acc.shape[1] // RET_DK):
            xh = acc[:, hh * RET_DK:(hh + 1) * RET_DK]
            heads.append(xh * cos + pltpu.roll(xh, RET_DK // 2, axis=1) * sin)
        return jnp.concatenate(heads, axis=1)

    epilogues = {
        "rot": rotary,
        "cast": lambda acc, rows: acc,
        "silu": lambda acc, rows: acc * jax.nn.sigmoid(acc),
        "sig": lambda acc, rows: jax.nn.sigmoid(acc),
    }

    def tile(kinds):
        hm = tm // INPROJ_ROW_SPLIT
        for c, kind in enumerate(kinds):
            cs = slice(c * MXU_COLS, (c + 1) * MXU_COLS)
            w = w_ref[:, cs]
            for m in range(INPROJ_ROW_SPLIT):
                rows = slice(m * hm, (m + 1) * hm)
                acc = jnp.dot(h_scr[rows, :], w, preferred_element_type=f32)
                o_ref[rows, cs] = epilogues[kind](acc, rows).astype(o_ref.dtype)

    for kinds in sorted(set(_INPROJ_TILE_KINDS)):
        tiles = [t for t, k in enumerate(_INPROJ_TILE_KINDS) if k == kinds]
        cond = functools.reduce(lambda a, b: a | b, [j == t for t in tiles])
        pl.when(cond)(functools.partial(tile, kinds))


def _inproj(x2, g, w_in, rot, seq):
    T = x2.shape[0]
    tm, tn = INPROJ_TM, INPROJ_TN
    tm = min(tm, seq)
    spt = seq // tm
    grid = (T // tm, len(_INPROJ_TILE_KINDS))
    return pl.pallas_call(
        _inproj_kernel,
        grid=grid,
        in_specs=[
            pl.BlockSpec((tm, D_MODEL), lambda i, j: (i, 0)),
            pl.BlockSpec((1, D_MODEL), lambda i, j: (0, 0)),
            pl.BlockSpec((D_MODEL, tn), lambda i, j: (0, j)),
            pl.BlockSpec((None, tm, 2 * RET_DK), lambda i, j: (jnp.where(j >= _T_RK, 1, 0), i % spt, 0)),
        ],
        out_specs=pl.BlockSpec((tm, tn), lambda i, j: (i, j)),
        out_shape=jax.ShapeDtypeStruct((T, IN_WIDTH), bf16),
        scratch_shapes=[pltpu.VMEM((tm, D_MODEL), bf16)],
        compiler_params=_params(("parallel", "arbitrary")),
        name="inproj",
    )(x2, g, w_in, rot)


def _retention_kernel(q_ref, k_ref, v_ref, gate_ref, dmat_ref, xi_ref, zeta_ref, cdec_ref, o_ref, kv_scr, st_scr):
    S = q_ref.shape[0]
    C = RET_CHUNK
    N = S // C
    heads = range(RET_HEADS_PER_STEP)
    qk = [slice(hh * RET_DK, (hh + 1) * RET_DK) for hh in heads]
    vv = [slice(hh * RET_DV, (hh + 1) * RET_DV) for hh in heads]
    q3 = [q_ref[:, qk[hh]].reshape(N, C, RET_DK) for hh in heads]
    k3 = [k_ref[:, qk[hh]].reshape(N, C, RET_DK) for hh in heads]
    v3 = [v_ref[:, vv[hh]].reshape(N, C, RET_DV) for hh in heads]

    for hh in heads:
        kz = (k3[hh].astype(f32) * zeta_ref[hh][None]).astype(bf16)
        kv_scr[hh] = jnp.einsum("ndk,nkv->ndv", jnp.swapaxes(kz, 1, 2), v3[hh], preferred_element_type=f32)

    cdec = [cdec_ref[hh] for hh in heads]

    def scan(n, states):
        for hh in heads:
            st_scr[hh, n] = states[hh].astype(bf16)
        return tuple(states[hh] * cdec[hh] + kv_scr[hh, n] for hh in heads)

    lax.fori_loop(0, N, scan, tuple(jnp.zeros((RET_DK, RET_DV), f32) for _ in heads))

    for hh in heads:
        sc = jnp.einsum("nqd,nkd->nqk", q3[hh], k3[hh], preferred_element_type=f32) * dmat_ref[hh][None]
        inner = jnp.einsum("nqk,nkv->nqv", sc.astype(bf16), v3[hh], preferred_element_type=f32)
        cross = jnp.einsum("nqd,ndv->nqv", q3[hh], st_scr[hh], preferred_element_type=f32) * xi_ref[hh][None]
        o = inner + cross
        o = o * lax.rsqrt(jnp.mean(o * o, axis=-1, keepdims=True) + RMS_EPS)
        o = gate_ref[:, vv[hh]].astype(f32).reshape(N, C, RET_DV) * o
        o_ref[:, vv[hh]] = o.reshape(S, RET_DV).astype(o_ref.dtype)


def _retention_tables():
    C = RET_CHUNK
    log_gamma = jnp.log(1.0 - 2.0 ** (-5.0 - jnp.arange(RET_HEADS, dtype=f32)))
    idx = jnp.arange(C, dtype=f32)
    rel = idx[:, None] - idx[None, :]
    dmat = jnp.where(rel[None] >= 0, jnp.exp(log_gamma[:, None, None] * jnp.maximum(rel, 0.0)[None]), 0.0)
    xi = jnp.exp(log_gamma[:, None] * (idx + 1.0))
    zeta = jnp.exp(log_gamma[:, None] * (C - 1.0 - idx))
    cdec = jnp.exp(log_gamma * C)
    xi_b = jnp.broadcast_to(xi[:, :, None], (RET_HEADS, C, RET_DV))
    zeta_b = jnp.broadcast_to(zeta[:, :, None], (RET_HEADS, C, RET_DK))
    cdec_b = jnp.broadcast_to(cdec[:, None, None], (RET_HEADS, 1, RET_DV))
    return dmat.astype(f32), xi_b.astype(f32), zeta_b.astype(f32), cdec_b.astype(f32)


def _retention(proj, batch, seq):
    T = proj.shape[0]
    C = RET_CHUNK
    N = seq // C
    dmat, xi_b, zeta_b, cdec_b = _retention_tables()
    hp = RET_HEADS_PER_STEP
    qk_w, v_w = hp * RET_DK, hp * RET_DV
    return pl.pallas_call(
        _retention_kernel,
        grid=(batch, RET_HEADS // hp),
        in_specs=[
            pl.BlockSpec((seq, qk_w), lambda b, h: (b, COL_RQ // qk_w + h)),
            pl.BlockSpec((seq, qk_w), lambda b, h: (b, COL_RK // qk_w + h)),
            pl.BlockSpec((seq, v_w), lambda b, h: (b, COL_RV // v_w + h)),
            pl.BlockSpec((seq, v_w), lambda b, h: (b, COL_RG // v_w + h)),
            pl.BlockSpec((hp, C, C), lambda b, h: (h, 0, 0)),
            pl.BlockSpec((hp, C, RET_DV), lambda b, h: (h, 0, 0)),
            pl.BlockSpec((hp, C, RET_DK), lambda b, h: (h, 0, 0)),
            pl.BlockSpec((hp, 1, RET_DV), lambda b, h: (h, 0, 0)),
        ],
        out_specs=pl.BlockSpec((seq, v_w), lambda b, h: (b, h)),
        out_shape=jax.ShapeDtypeStruct((T, RET_V_W), bf16),
        scratch_shapes=[pltpu.VMEM((hp, N, RET_DK, RET_DV), f32), pltpu.VMEM((hp, N, RET_DK, RET_DV), bf16)],
        compiler_params=_params(("parallel", "parallel")),
        name="retention",
    )(proj, proj, proj, proj, dmat, xi_b, zeta_b, cdec_b)


KV_PER_STEP = 2
Q_PER_KV = SWA_Q_HEADS // SWA_KV_HEADS
SWA_Q_STEP_W = KV_PER_STEP * Q_PER_KV * SWA_HEAD_DIM
LANES = 128
SWA_UNROLL = 8


def _swa_kernel(sink_ref, bias_ref, q_ref, k_ref, v_ref, o_ref, klo, khi, vlo, vhi):
    S = q_ref.shape[0]
    C = SWA_BLOCK
    gp = pl.program_id(1)
    half = SWA_HEAD_DIM

    lane_s = lax.broadcasted_iota(jnp.int32, (S, LANES), 1)
    low_s = lane_s < half

    def prep(src_ref, lo_scr, hi_scr, scale, with_ones):
        w = src_ref[...].astype(f32) * scale
        r = pltpu.roll(w, half, axis=1)
        zeros = jnp.zeros((C, lo_scr.shape[2]), bf16)
        for t in range(KV_PER_STEP):
            lo_scr[t, 0:C, :] = zeros
            hi_scr[t, 0:C, :] = zeros
        lo_scr[0, C:C + S, 0:LANES] = jnp.where(low_s, w, 0.0).astype(bf16)
        hi_scr[0, C:C + S, 0:LANES] = jnp.where(low_s, 0.0, r).astype(bf16)
        lo_scr[1, C:C + S, 0:LANES] = jnp.where(low_s, r, 0.0).astype(bf16)
        hi_scr[1, C:C + S, 0:LANES] = jnp.where(low_s, 0.0, w).astype(bf16)
        if with_ones:
            for t in range(KV_PER_STEP):
                lo_scr[t, C:C + S, LANES:2 * LANES] = jnp.where(low_s, 1.0, 0.0).astype(bf16)
                hi_scr[t, C:C + S, LANES:2 * LANES] = jnp.where(low_s, 0.0, 1.0).astype(bf16)

    prep(k_ref, klo, khi, SWA_HEAD_DIM ** -0.5, False)
    prep(v_ref, vlo, vhi, 1.0, True)

    rows = 2 * C
    win = 2 * C
    first_pair = lax.broadcasted_iota(jnp.int32, (rows, 1), 0) < C
    low_o = lax.broadcasted_iota(jnp.int32, (rows, LANES), 1) < half
    nt = (((1,), (1,)), ((), ()))

    def softmax_parts(s, sink, bias):
        s = s + bias
        m = jnp.maximum(jnp.max(s, axis=-1, keepdims=True), sink)
        return jnp.exp(s - m).astype(bf16), jnp.exp(sink - m)

    def block(n, carry):
        r0 = pl.multiple_of(n * C, C)
        bias = bias_ref[jnp.minimum(n, 1)]
        for t in range(KV_PER_STEP):
            c0 = t * Q_PER_KV * SWA_HEAD_DIM
            hbase = gp * (KV_PER_STEP * Q_PER_KV) + t * Q_PER_KV
            qs = jnp.concatenate([q_ref[pl.ds(r0, C), c0:c0 + LANES],
                                  q_ref[pl.ds(r0, C), c0 + LANES:c0 + 2 * LANES]], axis=0)
            s_e = lax.dot_general(qs, klo[t, pl.ds(r0, win), :], nt, preferred_element_type=f32)
            s_o = lax.dot_general(qs, khi[t, pl.ds(r0, win), :], nt, preferred_element_type=f32)
            sink_e = jnp.where(first_pair, sink_ref[hbase + 0], sink_ref[hbase + 2])
            sink_o = jnp.where(first_pair, sink_ref[hbase + 1], sink_ref[hbase + 3])
            p_e, z_e = softmax_parts(s_e, sink_e, bias)
            p_o, z_o = softmax_parts(s_o, sink_o, bias)
            pv = (jnp.dot(p_e, vlo[t, pl.ds(r0, win), :], preferred_element_type=f32)
                  + jnp.dot(p_o, vhi[t, pl.ds(r0, win), :], preferred_element_type=f32))
            den = pv[:, LANES:2 * LANES] + jnp.where(low_o, z_e, z_o)
            o = pv[:, 0:LANES] / den
            o_ref[pl.ds(r0, C), c0:c0 + LANES] = o[0:C].astype(o_ref.dtype)
            o_ref[pl.ds(r0, C), c0 + LANES:c0 + 2 * LANES] = o[C:2 * C].astype(o_ref.dtype)
        return carry

    lax.fori_loop(0, S // C, block, 0, unroll=SWA_UNROLL)


def _swa_bias():
    C = SWA_BLOCK
    row_i = jnp.arange(2 * C)[:, None] % C
    col_j = jnp.arange(2 * C)[None, :]
    band = (col_j > row_i) & (col_j <= row_i + C)
    first = band & (col_j >= C)
    return jnp.where(jnp.stack([first, band]), 0.0, -jnp.inf).astype(f32)


def _swa(proj, sinks, batch, seq):
    T = proj.shape[0]
    steps = SWA_KV_HEADS // KV_PER_STEP
    k_scr = pltpu.VMEM((KV_PER_STEP, seq + SWA_BLOCK, LANES), bf16)
    v_scr = pltpu.VMEM((KV_PER_STEP, seq + SWA_BLOCK, 2 * LANES), bf16)
    bias = _swa_bias()
    return pl.pallas_call(
        _swa_kernel,
        grid=(batch, steps),
        in_specs=[
            pl.BlockSpec(memory_space=pltpu.SMEM),
            pl.BlockSpec(bias.shape, lambda b, g: (0, 0, 0)),
            pl.BlockSpec((seq, SWA_Q_STEP_W), lambda b, g: (b, COL_SQ // SWA_Q_STEP_W + g)),
            pl.BlockSpec((seq, LANES), lambda b, g: (b, COL_SK // LANES + g)),
            pl.BlockSpec((seq, LANES), lambda b, g: (b, COL_SV // LANES + g)),
        ],
        out_specs=pl.BlockSpec((seq, SWA_Q_STEP_W), lambda b, g: (b, g)),
        out_shape=jax.ShapeDtypeStruct((T, SWA_Q_W), bf16),
        scratch_shapes=[k_scr, k_scr, v_scr, v_scr],
        compiler_params=_params(("parallel", "parallel")),
        name="swa",
    )(sinks, bias, proj, proj, proj)


def _mix_kernel(ret_ref, swa_ref, *refs):
    n_gate = MIX_TN // MIX_GATE_W
    gr_refs, gs_refs = refs[:n_gate], refs[n_gate:2 * n_gate]
    wr_ref, ws_ref, wo_ref, x_ref, g_ref, o_ref = refs[2 * n_gate:]
    j = pl.program_id(1)

    @pl.when(j == 0)
    def _():
        o_ref[...] = jnp.zeros_like(o_ref)

    part = None
    for c in range(n_gate):
        cs = slice(c * MIX_GATE_W, (c + 1) * MIX_GATE_W)
        ret_out = jnp.dot(ret_ref[...], wr_ref[:, cs], preferred_element_type=f32)
        swa_out = jnp.dot(swa_ref[...], ws_ref[:, cs], preferred_element_type=f32)
        mixed = gr_refs[c][...].astype(f32) * ret_out + gs_refs[c][...].astype(f32) * swa_out
        d = jnp.dot(mixed.astype(bf16), wo_ref[cs, :], preferred_element_type=f32)
        part = d if part is None else part + d
    o_ref[...] += part

    @pl.when(j == pl.num_programs(1) - 1)
    def _():
        _residual_norm_rows(x_ref, o_ref, g_ref, o_ref, o_ref.shape[0])


def _mix(ret, swa, proj, w_ret_o, w_swa_o, w_out, x2, g):
    T = x2.shape[0]
    tm, tn = min(MIX_TM, T), MIX_TN
    nj = D_MODEL // tn
    gw = MIX_GATE_W
    n_gate = tn // gw

    def gate_specs(col0):
        return [pl.BlockSpec((tm, gw), functools.partial(lambda i, j, c: (i, col0 // gw + j * n_gate + c), c=c))
                for c in range(n_gate)]

    return pl.pallas_call(
        _mix_kernel,
        grid=(T // tm, nj),
        in_specs=[
            pl.BlockSpec((tm, RET_V_W), lambda i, j: (i, 0)),
            pl.BlockSpec((tm, SWA_Q_W), lambda i, j: (i, 0)),
            *gate_specs(COL_GR),
            *gate_specs(COL_GS),
            pl.BlockSpec((RET_V_W, tn), lambda i, j: (0, j)),
            pl.BlockSpec((SWA_Q_W, tn), lambda i, j: (0, j)),
            pl.BlockSpec((tn, D_MODEL), lambda i, j: (j, 0)),
            pl.BlockSpec((tm, D_MODEL), lambda i, j: (i, 0)),
            pl.BlockSpec((1, D_MODEL), lambda i, j: (0, 0)),
        ],
        out_specs=pl.BlockSpec((tm, D_MODEL), lambda i, j: (i, 0)),
        out_shape=jax.ShapeDtypeStruct((T, D_MODEL), f32),
        compiler_params=_params(("parallel", "arbitrary")),
        name="mix",
    )(ret, swa, *([proj] * (2 * n_gate)), w_ret_o, w_swa_o, w_out, x2, g)


def _gelu_tanh(x):
    return x * (0.5 * (1.0 + jnp.tanh(0.7978845608028654 * (x + 0.044715 * (x * x * x)))))


def _ffn_kernel(x_ref, halo_ref, gpre_ref, wv_ref, wg_ref, cwv_ref, cwg_ref, cbv_ref, cbg_ref, wd_ref, gpost_ref,
                o_ref, h_scr, *u_scrs, tiles_per_seq):
    i = pl.program_id(0)
    j = pl.program_id(1)
    tm = x_ref.shape[0]
    tn = wv_ref.shape[1]
    H = FFN_HALO

    @pl.when(j == 0)
    def _():
        keep = jnp.where(i % tiles_per_seq == 0, 0.0, 1.0)
        h_scr[0:H, :] = (_rmsnorm(halo_ref[...], gpre_ref[...]) * keep).astype(bf16)
        _norm_rows_to(x_ref, gpre_ref, h_scr, H, tm)
        o_ref[...] = jnp.zeros_like(o_ref)

    def conv(u_scr, row0, nrows, u_cols, w_cols, cw_ref, cb_ref):
        y = cb_ref[:, w_cols]
        for kk in range(CONV_WIDTH):
            y = y + u_scr[pl.ds(H - (CONV_WIDTH - 1) + kk + row0, nrows), u_cols] * cw_ref[kk:kk + 1, w_cols]
        return y

    h = h_scr[...]
    hm = tm // FFN_ROW_SPLIT
    parts = [None] * FFN_ROW_SPLIT
    uv = slice(0, MXU_COLS)
    ug = slice(MXU_COLS, 2 * MXU_COLS)
    n_chunks = len(u_scrs)
    for c, u_scr in enumerate(u_scrs):
        cs = slice(c * MXU_COLS, (c + 1) * MXU_COLS)
        if c < n_chunks - 1:
            u_scr[:, uv] = jnp.dot(h, wv_ref[:, cs], preferred_element_type=f32)
            u_scr[:, ug] = jnp.dot(h, wg_ref[:, cs], preferred_element_type=f32)
        else:
            wv, wg = wv_ref[:, cs], wg_ref[:, cs]
            r0 = 0
            for m in range(FFN_ROW_SPLIT):
                r1 = H + (m + 1) * hm
                u_scr[r0:r1, uv] = jnp.dot(h_scr[r0:r1, :], wv, preferred_element_type=f32)
                u_scr[r0:r1, ug] = jnp.dot(h_scr[r0:r1, :], wg, preferred_element_type=f32)
                r0 = r1
        for m in range(FFN_ROW_SPLIT):
            val = conv(u_scr, m * hm, hm, uv, cs, cwv_ref, cbv_ref)
            gate = conv(u_scr, m * hm, hm, ug, cs, cwg_ref, cbg_ref)
            a = (_gelu_tanh(gate) * val).astype(bf16)
            d = jnp.dot(a, wd_ref[cs, :], preferred_element_type=f32)
            parts[m] = d if parts[m] is None else parts[m] + d
    for m in range(FFN_ROW_SPLIT):
        o_ref[m * hm:(m + 1) * hm, :] += parts[m]

    @pl.when(j == pl.num_programs(1) - 1)
    def _():
        _residual_norm_rows(x_ref, o_ref, gpost_ref, o_ref, tm)


def _ffn(x1, g_pre, w_up, conv_w, conv_b, w_down, g_post, seq):
    T = x1.shape[0]
    tm, tn = min(FFN_TM, seq), FFN_TN
    nj = D_FF // tn
    hb = tm // FFN_HALO
    kern = functools.partial(_ffn_kernel, tiles_per_seq=seq // tm)
    return pl.pallas_call(
        kern,
        grid=(T // tm, nj),
        in_specs=[
            pl.BlockSpec((tm, D_MODEL), lambda i, j: (i, 0)),
            pl.BlockSpec((FFN_HALO, D_MODEL), lambda i, j: (jnp.maximum(i * hb - 1, 0), 0)),
            pl.BlockSpec((1, D_MODEL), lambda i, j: (0, 0)),
            pl.BlockSpec((D_MODEL, tn), lambda i, j: (0, j)),
            pl.BlockSpec((D_MODEL, tn), lambda i, j: (0, nj + j)),
            pl.BlockSpec((CONV_WIDTH, tn), lambda i, j: (0, j)),
            pl.BlockSpec((CONV_WIDTH, tn), lambda i, j: (0, nj + j)),
            pl.BlockSpec((1, tn), lambda i, j: (0, j)),
            pl.BlockSpec((1, tn), lambda i, j: (0, nj + j)),
            pl.BlockSpec((tn, D_MODEL), lambda i, j: (j, 0)),
            pl.BlockSpec((1, D_MODEL), lambda i, j: (0, 0)),
        ],
        out_specs=pl.BlockSpec((tm, D_MODEL), lambda i, j: (i, 0)),
        out_shape=jax.ShapeDtypeStruct((T, D_MODEL), f32),
        scratch_shapes=[
            pltpu.VMEM((tm + FFN_HALO, D_MODEL), bf16),
        ] + [pltpu.VMEM((tm + FFN_HALO, 2 * MXU_COLS), f32)] * (tn // MXU_COLS),
        compiler_params=_params(("parallel", "arbitrary")),
        name="ffn",
    )(x1, x1, g_pre, w_up, w_up, conv_w, conv_w, conv_b, conv_b, w_down, g_post)


def _rotary_tables(seq):
    d = RET_DK
    inv = 1.0 / (ROPE_BASE ** (jnp.arange(0, d, 2, dtype=f32) / d))
    ang = jnp.arange(seq, dtype=jnp.int32).astype(f32)[:, None] * inv[None, :]
    cos, sin = jnp.cos(ang), jnp.sin(ang)
    tab = jnp.concatenate([cos, cos, -sin, sin], axis=-1)
    return jnp.stack([tab, tab * (d ** -0.5)])


def kernel(x, g_pre_mix, w_in, w_ret_o, w_swa_o, w_out, swa_sinks, g_post_mix, g_pre_ffn, w_up, conv_w, conv_b,
           w_down, g_post_ffn):
    B, S, D = x.shape
    depth = w_in.shape[0]
    rot = _rotary_tables(S)
    x2 = x.reshape(B * S, D)
    for l in range(depth):
        proj = _inproj(x2, g_pre_mix[l][None], w_in[l].astype(bf16), rot, S)
        ret = _retention(proj, B, S)
        swa = _swa(proj, swa_sinks[l].astype(f32), B, S)
        x2 = _mix(ret, swa, proj, w_ret_o[l].astype(bf16), w_swa_o[l].astype(bf16), w_out[l].astype(bf16), x2,
                  g_post_mix[l][None])
        x2 = _ffn(x2, g_pre_ffn[l][None], w_up[l].astype(bf16), conv_w[l], conv_b[l][None], w_down[l].astype(bf16),
                  g_post_ffn[l][None], S)
    return x2.reshape(B, S, D)
```

```python
import functools

import jax
import jax.numpy as jnp
from jax import lax
from jax.experimental import pallas as pl
from jax.experimental.pallas import tpu as pltpu

D_MODEL = 2048
RET_HEADS = 8
RET_DK = 128
RET_DV = 256
RET_CHUNK = 128
ROPE_BASE = 10000.0
SWA_Q_HEADS = 16
SWA_KV_HEADS = 4
SWA_HEAD_DIM = 64
SWA_BLOCK = 128
D_FF = 5632
CONV_WIDTH = 3
RMS_EPS = 1e-6

RET_QK_W = RET_HEADS * RET_DK
RET_V_W = RET_HEADS * RET_DV
SWA_Q_W = SWA_Q_HEADS * SWA_HEAD_DIM
SWA_KV_W = SWA_KV_HEADS * SWA_HEAD_DIM

COL_RQ = 0
COL_RK = COL_RQ + RET_QK_W
COL_RV = COL_RK + RET_QK_W
COL_RG = COL_RV + RET_V_W
COL_SQ = COL_RG + RET_V_W
COL_SK = COL_SQ + SWA_Q_W
COL_SV = COL_SK + SWA_KV_W
COL_GR = COL_SV + SWA_KV_W
COL_GS = COL_GR + D_MODEL
IN_WIDTH = COL_GS + D_MODEL

V7X_VMEM_LIMIT_BYTES = 60 * 1024 * 1024
BF16_SUBLANES = 16
MXU_COLS = 256
NORM_ROWS = 32
NORM_UNROLL = 4

INPROJ_TM = 1024
INPROJ_TN = 1024
MIX_TM = 512
MIX_TN = 1024
MIX_GATE_W = 512
FFN_TM = 1024
FFN_TN = 512
FFN_HALO = BF16_SUBLANES
FFN_ROW_SPLIT = 4
INPROJ_ROW_SPLIT = 4
RET_HEADS_PER_STEP = 2

f32 = jnp.float32
bf16 = jnp.bfloat16

def _params(semantics):
    return pltpu.CompilerParams(dimension_semantics=semantics, vmem_limit_bytes=V7X_VMEM_LIMIT_BYTES)


def _rmsnorm(x, g):
    ms = jnp.mean(x * x, axis=-1, keepdims=True)
    return x * lax.rsqrt(ms + RMS_EPS) * g


def _norm_rows_to(x_ref, g_ref, dst_ref, dst_off, rows):
    g = g_ref[...]
    group = NORM_ROWS * NORM_UNROLL

    def step(c, carry):
        base = pl.multiple_of(c * group, group)
        for k in range(NORM_UNROLL):
            r = base + k * NORM_ROWS
            dst_ref[pl.ds(dst_off + r, NORM_ROWS), :] = (
                _rmsnorm(x_ref[pl.ds(r, NORM_ROWS), :], g).astype(dst_ref.dtype))
        return carry

    lax.fori_loop(0, rows // group, step, 0)


def _residual_norm_rows(x_ref, y_ref, g_ref, o_ref, rows):
    g = g_ref[...]
    group = NORM_ROWS * NORM_UNROLL

    def step(c, carry):
        base = pl.multiple_of(c * group, group)
        rows_k = [pl.ds(base + k * NORM_ROWS, NORM_ROWS) for k in range(NORM_UNROLL)]
        scales = []
        for r in rows_k:
            y = y_ref[r, :]
            scales.append(lax.rsqrt(jnp.mean(y * y, axis=-1, keepdims=True) + RMS_EPS))
        for r, s in zip(rows_k, scales):
            o_ref[r, :] = x_ref[r, :] + y_ref[r, :] * s * g
        return carry

    lax.fori_loop(0, rows // group, step, 0)


def _inproj_tile_kinds():
    starts = ((COL_RQ, "rot"), (COL_RK, "rot"), (COL_RV, "cast"), (COL_RG, "silu"), (COL_SQ, "cast"),
              (COL_GR, "sig"))
    chunk_kinds = [[k for c0, k in starts if c0 <= col][-1] for col in range(0, IN_WIDTH, MXU_COLS)]
    per_tile = INPROJ_TN // MXU_COLS
    return [tuple(chunk_kinds[t:t + per_tile]) for t in range(0, len(chunk_kinds), per_tile)]


_INPROJ_TILE_KINDS = _inproj_tile_kinds()
_T_RK = COL_RK // INPROJ_TN
assert COL_RK % INPROJ_TN == 0 and COL_RV % INPROJ_TN == 0


def _inproj_kernel(x_ref, g_ref, w_ref, rot_ref, o_ref, h_scr):
    j = pl.program_id(1)
    tm = x_ref.shape[0]

    @pl.when(j == 0)
    def _():
        _norm_rows_to(x_ref, g_ref, h_scr, 0, tm)

    def rotary(acc, rows):
        cos = rot_ref[rows, 0:RET_DK]
        sin = rot_ref[rows, RET_DK:2 * RET_DK]
        heads = []
        for hh in range(acc.shape[1] // RET_DK):
            xh = acc[:, hh * RET_DK:(hh + 1) * RET_DK]
            heads.append(xh * cos + pltpu.roll(xh, RET_DK // 2, axis=1) * sin)
        return jnp.concatenate(heads, axis=1)

    epilogues = {
        "rot": rotary,
        "cast": lambda acc, rows: acc,
        "silu": lambda acc, rows: acc * jax.nn.sigmoid(acc),
        "sig": lambda acc, rows: jax.nn.sigmoid(acc),
    }

    def tile(kinds):
        hm = tm // INPROJ_ROW_SPLIT
        for c, kind in enumerate(kinds):
            cs = slice(c * MXU_COLS, (c + 1) * MXU_COLS)
            w = w_ref[:, cs]
            for m in range(INPROJ_ROW_SPLIT):
                rows = slice(m * hm, (m + 1) * hm)
                acc = jnp.dot(h_scr[rows, :], w, preferred_element_type=f32)
                o_ref[rows, cs] = epilogues[kind](acc, rows).astype(o_ref.dtype)

    for kinds in sorted(set(_INPROJ_TILE_KINDS)):
        tiles = [t for t, k in enumerate(_INPROJ_TILE_KINDS) if k == kinds]
        cond = functools.reduce(lambda a, b: a | b, [j == t for t in tiles])
        pl.when(cond)(functools.partial(tile, kinds))


def _inproj(x2, g, w_in, rot, seq):
    T = x2.shape[0]
    tm, tn = INPROJ_TM, INPROJ_TN
    tm = min(tm, seq)
    spt = seq // tm
    grid = (T // tm, len(_INPROJ_TILE_KINDS))
    return pl.pallas_call(
        _inproj_kernel,
        grid=grid,
        in_specs=[
            pl.BlockSpec((tm, D_MODEL), lambda i, j: (i, 0)),
            pl.BlockSpec((1, D_MODEL), lambda i, j: (0, 0)),
            pl.BlockSpec((None, D_MODEL, tn), lambda i, j: (j, 0, 0)),
            pl.BlockSpec((None, tm, 2 * RET_DK), lambda i, j: (jnp.where(j >= _T_RK, 1, 0), i % spt, 0)),
        ],
        out_specs=pl.BlockSpec((tm, tn), lambda i, j: (i, j)),
        out_shape=jax.ShapeDtypeStruct((T, IN_WIDTH), bf16),
        scratch_shapes=[pltpu.VMEM((tm, D_MODEL), bf16)],
        compiler_params=_params(("parallel", "arbitrary")),
        name="inproj",
    )(x2, g, w_in, rot)


def _retention_kernel(q_ref, k_ref, v_ref, gate_ref, dmat_ref, xi_ref, zeta_ref, cdec_ref, o_ref, kv_scr, st_scr):
    S = q_ref.shape[0]
    C = RET_CHUNK
    N = S // C
    heads = range(RET_HEADS_PER_STEP)
    qk = [slice(hh * RET_DK, (hh + 1) * RET_DK) for hh in heads]
    vv = [slice(hh * RET_DV, (hh + 1) * RET_DV) for hh in heads]
    q3 = [q_ref[:, qk[hh]].reshape(N, C, RET_DK) for hh in heads]
    k3 = [k_ref[:, qk[hh]].reshape(N, C, RET_DK) for hh in heads]
    v3 = [v_ref[:, vv[hh]].reshape(N, C, RET_DV) for hh in heads]

    for hh in heads:
        kz = (k3[hh].astype(f32) * zeta_ref[hh][None]).astype(bf16)
        kv_scr[hh] = jnp.einsum("ndk,nkv->ndv", jnp.swapaxes(kz, 1, 2), v3[hh], preferred_element_type=f32)

    cdec = [cdec_ref[hh] for hh in heads]

    def scan(n, states):
        for hh in heads:
            st_scr[hh, n] = states[hh].astype(bf16)
        return tuple(states[hh] * cdec[hh] + kv_scr[hh, n] for hh in heads)

    lax.fori_loop(0, N, scan, tuple(jnp.zeros((RET_DK, RET_DV), f32) for _ in heads))

    for hh in heads:
        sc = jnp.einsum("nqd,nkd->nqk", q3[hh], k3[hh], preferred_element_type=f32) * dmat_ref[hh][None]
        inner = jnp.einsum("nqk,nkv->nqv", sc.astype(bf16), v3[hh], preferred_element_type=f32)
        cross = jnp.einsum("nqd,ndv->nqv", q3[hh], st_scr[hh], preferred_element_type=f32) * xi_ref[hh][None]
        o = inner + cross
        o = o * lax.rsqrt(jnp.mean(o * o, axis=-1, keepdims=True) + RMS_EPS)
        o = gate_ref[:, vv[hh]].astype(f32).reshape(N, C, RET_DV) * o
        o_ref[:, vv[hh]] = o.reshape(S, RET_DV).astype(o_ref.dtype)


def _retention_tables():
    C = RET_CHUNK
    log_gamma = jnp.log(1.0 - 2.0 ** (-5.0 - jnp.arange(RET_HEADS, dtype=f32)))
    idx = jnp.arange(C, dtype=f32)
    rel = idx[:, None] - idx[None, :]
    dmat = jnp.where(rel[None] >= 0, jnp.exp(log_gamma[:, None, None] * jnp.maximum(rel, 0.0)[None]), 0.0)
    xi = jnp.exp(log_gamma[:, None] * (idx + 1.0))
    zeta = jnp.exp(log_gamma[:, None] * (C - 1.0 - idx))
    cdec = jnp.exp(log_gamma * C)
    xi_b = jnp.broadcast_to(xi[:, :, None], (RET_HEADS, C, RET_DV))
    zeta_b = jnp.broadcast_to(zeta[:, :, None], (RET_HEADS, C, RET_DK))
    cdec_b = jnp.broadcast_to(cdec[:, None, None], (RET_HEADS, 1, RET_DV))
    return dmat.astype(f32), xi_b.astype(f32), zeta_b.astype(f32), cdec_b.astype(f32)


def _retention(proj, batch, seq):
    T = proj.shape[0]
    C = RET_CHUNK
    N = seq // C
    dmat, xi_b, zeta_b, cdec_b = _retention_tables()
    hp = RET_HEADS_PER_STEP
    qk_w, v_w = hp * RET_DK, hp * RET_DV
    return pl.pallas_call(
        _retention_kernel,
        grid=(batch, RET_HEADS // hp),
        in_specs=[
            pl.BlockSpec((seq, qk_w), lambda b, h: (b, COL_RQ // qk_w + h)),
            pl.BlockSpec((seq, qk_w), lambda b, h: (b, COL_RK // qk_w + h)),
            pl.BlockSpec((seq, v_w), lambda b, h: (b, COL_RV // v_w + h)),
            pl.BlockSpec((seq, v_w), lambda b, h: (b, COL_RG // v_w + h)),
            pl.BlockSpec((hp, C, C), lambda b, h: (h, 0, 0)),
            pl.BlockSpec((hp, C, RET_DV), lambda b, h: (h, 0, 0)),
            pl.BlockSpec((hp, C, RET_DK), lambda b, h: (h, 0, 0)),
            pl.BlockSpec((hp, 1, RET_DV), lambda b, h: (h, 0, 0)),
        ],
        out_specs=pl.BlockSpec((seq, v_w), lambda b, h: (b, h)),
        out_shape=jax.ShapeDtypeStruct((T, RET_V_W), bf16),
        scratch_shapes=[pltpu.VMEM((hp, N, RET_DK, RET_DV), f32), pltpu.VMEM((hp, N, RET_DK, RET_DV), bf16)],
        compiler_params=_params(("parallel", "parallel")),
        name="retention",
    )(proj, proj, proj, proj, dmat, xi_b, zeta_b, cdec_b)


KV_PER_STEP = 2
Q_PER_KV = SWA_Q_HEADS // SWA_KV_HEADS
SWA_Q_STEP_W = KV_PER_STEP * Q_PER_KV * SWA_HEAD_DIM
LANES = 128
SWA_UNROLL = 8


def _swa_kernel(sink_ref, bias_ref, q_ref, k_ref, v_ref, o_ref, klo, khi, vlo, vhi):
    S = q_ref.shape[0]
    C = SWA_BLOCK
    gp = pl.program_id(1)
    half = SWA_HEAD_DIM

    lane_s = lax.broadcasted_iota(jnp.int32, (S, LANES), 1)
    low_s = lane_s < half

    def prep(src_ref, lo_scr, hi_scr, scale, with_ones):
        w = src_ref[...].astype(f32) * scale
        r = pltpu.roll(w, half, axis=1)
        zeros = jnp.zeros((C, lo_scr.shape[2]), bf16)
        for t in range(KV_PER_STEP):
            lo_scr[t, 0:C, :] = zeros
            hi_scr[t, 0:C, :] = zeros
        lo_scr[0, C:C + S, 0:LANES] = jnp.where(low_s, w, 0.0).astype(bf16)
        hi_scr[0, C:C + S, 0:LANES] = jnp.where(low_s, 0.0, r).astype(bf16)
        lo_scr[1, C:C + S, 0:LANES] = jnp.where(low_s, r, 0.0).astype(bf16)
        hi_scr[1, C:C + S, 0:LANES] = jnp.where(low_s, 0.0, w).astype(bf16)
        if with_ones:
            for t in range(KV_PER_STEP):
                lo_scr[t, C:C + S, LANES:2 * LANES] = jnp.where(low_s, 1.0, 0.0).astype(bf16)
                hi_scr[t, C:C + S, LANES:2 * LANES] = jnp.where(low_s, 0.0, 1.0).astype(bf16)

    prep(k_ref, klo, khi, SWA_HEAD_DIM ** -0.5, False)
    prep(v_ref, vlo, vhi, 1.0, True)

    rows = 2 * C
    win = 2 * C
    first_pair = lax.broadcasted_iota(jnp.int32, (rows, 1), 0) < C
    low_o = lax.broadcasted_iota(jnp.int32, (rows, LANES), 1) < half
    nt = (((1,), (1,)), ((), ()))

    def softmax_parts(s, sink, bias):
        s = s + bias
        m = jnp.maximum(jnp.max(s, axis=-1, keepdims=True), sink)
        return jnp.exp(s - m).astype(bf16), jnp.exp(sink - m)

    def block(n, carry):
        r0 = pl.multiple_of(n * C, C)
        bias = bias_ref[jnp.minimum(n, 1)]
        for t in range(KV_PER_STEP):
            c0 = t * Q_PER_KV * SWA_HEAD_DIM
            hbase = gp * (KV_PER_STEP * Q_PER_KV) + t * Q_PER_KV
            qs = jnp.concatenate([q_ref[pl.ds(r0, C), c0:c0 + LANES],
                                  q_ref[pl.ds(r0, C), c0 + LANES:c0 + 2 * LANES]], axis=0)
            s_e = lax.dot_general(qs, klo[t, pl.ds(r0, win), :], nt, preferred_element_type=f32)
            s_o = lax.dot_general(qs, khi[t, pl.ds(r0, win), :], nt, preferred_element_type=f32)
            sink_e = jnp.where(first_pair, sink_ref[hbase + 0], sink_ref[hbase + 2])
            sink_o = jnp.where(first_pair, sink_ref[hbase + 1], sink_ref[hbase + 3])
            p_e, z_e = softmax_parts(s_e, sink_e, bias)
            p_o, z_o = softmax_parts(s_o, sink_o, bias)
            pv = (jnp.dot(p_e, vlo[t, pl.ds(r0, win), :], preferred_element_type=f32)
                  + jnp.dot(p_o, vhi[t, pl.ds(r0, win), :], preferred_element_type=f32))
            den = pv[:, LANES:2 * LANES] + jnp.where(low_o, z_e, z_o)
            o = pv[:, 0:LANES] / den
            o_ref[pl.ds(r0, C), c0:c0 + LANES] = o[0:C].astype(o_ref.dtype)
            o_ref[pl.ds(r0, C), c0 + LANES:c0 + 2 * LANES] = o[C:2 * C].astype(o_ref.dtype)
        return carry

    lax.fori_loop(0, S // C, block, 0, unroll=SWA_UNROLL)


def _swa_bias():
    C = SWA_BLOCK
    row_i = jnp.arange(2 * C)[:, None] % C
    col_j = jnp.arange(2 * C)[None, :]
    band = (col_j > row_i) & (col_j <= row_i + C)
    first = band & (col_j >= C)
    return jnp.where(jnp.stack([first, band]), 0.0, -jnp.inf).astype(f32)


def _swa(proj, sinks, batch, seq):
    T = proj.shape[0]
    steps = SWA_KV_HEADS // KV_PER_STEP
    k_scr = pltpu.VMEM((KV_PER_STEP, seq + SWA_BLOCK, LANES), bf16)
    v_scr = pltpu.VMEM((KV_PER_STEP, seq + SWA_BLOCK, 2 * LANES), bf16)
    bias = _swa_bias()
    return pl.pallas_call(
        _swa_kernel,
        grid=(batch, steps),
        in_specs=[
            pl.BlockSpec(memory_space=pltpu.SMEM),
            pl.BlockSpec(bias.shape, lambda b, g: (0, 0, 0)),
            pl.BlockSpec((seq, SWA_Q_STEP_W), lambda b, g: (b, COL_SQ // SWA_Q_STEP_W + g)),
            pl.BlockSpec((seq, LANES), lambda b, g: (b, COL_SK // LANES + g)),
            pl.BlockSpec((seq, LANES), lambda b, g: (b, COL_SV // LANES + g)),
        ],
        out_specs=pl.BlockSpec((seq, SWA_Q_STEP_W), lambda b, g: (b, g)),
        out_shape=jax.ShapeDtypeStruct((T, SWA_Q_W), bf16),
        scratch_shapes=[k_scr, k_scr, v_scr, v_scr],
        compiler_params=_params(("parallel", "parallel")),
        name="swa",
    )(sinks, bias, proj, proj, proj)


def _mix_kernel(ret_ref, swa_ref, *refs):
    n_gate = MIX_TN // MIX_GATE_W
    gr_refs, gs_refs = refs[:n_gate], refs[n_gate:2 * n_gate]
    wr_ref, ws_ref, wo_ref, x_ref, g_ref, o_ref = refs[2 * n_gate:]
    j = pl.program_id(1)

    @pl.when(j == 0)
    def _():
        o_ref[...] = jnp.zeros_like(o_ref)

    part = None
    for c in range(n_gate):
        cs = slice(c * MIX_GATE_W, (c + 1) * MIX_GATE_W)
        ret_out = jnp.dot(ret_ref[...], wr_ref[:, cs], preferred_element_type=f32)
        swa_out = jnp.dot(swa_ref[...], ws_ref[:, cs], preferred_element_type=f32)
        mixed = gr_refs[c][...].astype(f32) * ret_out + gs_refs[c][...].astype(f32) * swa_out
        d = jnp.dot(mixed.astype(bf16), wo_ref[cs, :], preferred_element_type=f32)
        part = d if part is None else part + d
    o_ref[...] += part

    @pl.when(j == pl.num_programs(1) - 1)
    def _():
        _residual_norm_rows(x_ref, o_ref, g_ref, o_ref, o_ref.shape[0])


def _mix(ret, swa, proj, w_ret_o, w_swa_o, w_out, x2, g):
    T = x2.shape[0]
    tm, tn = min(MIX_TM, T), MIX_TN
    nj = D_MODEL // tn
    gw = MIX_GATE_W
    n_gate = tn // gw

    def gate_specs(col0):
        return [pl.BlockSpec((tm, gw), functools.partial(lambda i, j, c: (i, col0 // gw + j * n_gate + c), c=c))
                for c in range(n_gate)]

    return pl.pallas_call(
        _mix_kernel,
        grid=(T // tm, nj),
        in_specs=[
            pl.BlockSpec((tm, RET_V_W), lambda i, j: (i, 0)),
            pl.BlockSpec((tm, SWA_Q_W), lambda i, j: (i, 0)),
            *gate_specs(COL_GR),
            *gate_specs(COL_GS),
            pl.BlockSpec((None, RET_V_W, tn), lambda i, j: (j, 0, 0)),
            pl.BlockSpec((None, SWA_Q_W, tn), lambda i, j: (j, 0, 0)),
            pl.BlockSpec((tn, D_MODEL), lambda i, j: (j, 0)),
            pl.BlockSpec((tm, D_MODEL), lambda i, j: (i, 0)),
            pl.BlockSpec((1, D_MODEL), lambda i, j: (0, 0)),
        ],
        out_specs=pl.BlockSpec((tm, D_MODEL), lambda i, j: (i, 0)),
        out_shape=jax.ShapeDtypeStruct((T, D_MODEL), f32),
        compiler_params=_params(("parallel", "arbitrary")),
        name="mix",
    )(ret, swa, *([proj] * (2 * n_gate)), w_ret_o, w_swa_o, w_out, x2, g)


def _gelu_tanh(x):
    return x * (0.5 * (1.0 + jnp.tanh(0.7978845608028654 * (x + 0.044715 * (x * x * x)))))


def _ffn_kernel(x_ref, halo_ref, gpre_ref, wv_ref, wg_ref, cwv_ref, cwg_ref, cbv_ref, cbg_ref, wd_ref, gpost_ref,
                o_ref, h_scr, *u_scrs, tiles_per_seq):
    i = pl.program_id(0)
    j = pl.program_id(1)
    tm = x_ref.shape[0]
    tn = wv_ref.shape[1]
    H = FFN_HALO

    @pl.when(j == 0)
    def _():
        keep = jnp.where(i % tiles_per_seq == 0, 0.0, 1.0)
        h_scr[0:H, :] = (_rmsnorm(halo_ref[...], gpre_ref[...]) * keep).astype(bf16)
        _norm_rows_to(x_ref, gpre_ref, h_scr, H, tm)
        o_ref[...] = jnp.zeros_like(o_ref)

    def conv(u_scr, row0, nrows, u_cols, w_cols, cw_ref, cb_ref):
        y = cb_ref[:, w_cols]
        for kk in range(CONV_WIDTH):
            y = y + u_scr[pl.ds(H - (CONV_WIDTH - 1) + kk + row0, nrows), u_cols] * cw_ref[kk:kk + 1, w_cols]
        return y

    h = h_scr[...]
    hm = tm // FFN_ROW_SPLIT
    parts = [None] * FFN_ROW_SPLIT
    uv = slice(0, MXU_COLS)
    ug = slice(MXU_COLS, 2 * MXU_COLS)
    n_chunks = len(u_scrs)
    for c, u_scr in enumerate(u_scrs):
        cs = slice(c * MXU_COLS, (c + 1) * MXU_COLS)
        if c < n_chunks - 1:
            u_scr[:, uv] = jnp.dot(h, wv_ref[:, cs], preferred_element_type=f32)
            u_scr[:, ug] = jnp.dot(h, wg_ref[:, cs], preferred_element_type=f32)
        else:
            wv, wg = wv_ref[:, cs], wg_ref[:, cs]
            r0 = 0
            for m in range(FFN_ROW_SPLIT):
                r1 = H + (m + 1) * hm
                u_scr[r0:r1, uv] = jnp.dot(h_scr[r0:r1, :], wv, preferred_element_type=f32)
                u_scr[r0:r1, ug] = jnp.dot(h_scr[r0:r1, :], wg, preferred_element_type=f32)
                r0 = r1
        for m in range(FFN_ROW_SPLIT):
            val = conv(u_scr, m * hm, hm, uv, cs, cwv_ref, cbv_ref)
            gate = conv(u_scr, m * hm, hm, ug, cs, cwg_ref, cbg_ref)
            a = (_gelu_tanh(gate) * val).astype(bf16)
            d = jnp.dot(a, wd_ref[cs, :], preferred_element_type=f32)
            parts[m] = d if parts[m] is None else parts[m] + d
    for m in range(FFN_ROW_SPLIT):
        o_ref[m * hm:(m + 1) * hm, :] += parts[m]

    @pl.when(j == pl.num_programs(1) - 1)
    def _():
        _residual_norm_rows(x_ref, o_ref, gpost_ref, o_ref, tm)


def _ffn(x1, g_pre, w_up, conv_w, conv_b, w_down, g_post, seq):
    T = x1.shape[0]
    tm, tn = min(FFN_TM, seq), FFN_TN
    nj = D_FF // tn
    hb = tm // FFN_HALO
    kern = functools.partial(_ffn_kernel, tiles_per_seq=seq // tm)
    return pl.pallas_call(
        kern,
        grid=(T // tm, nj),
        in_specs=[
            pl.BlockSpec((tm, D_MODEL), lambda i, j: (i, 0)),
            pl.BlockSpec((FFN_HALO, D_MODEL), lambda i, j: (jnp.maximum(i * hb - 1, 0), 0)),
            pl.BlockSpec((1, D_MODEL), lambda i, j: (0, 0)),
            pl.BlockSpec((None, D_MODEL, tn), lambda i, j: (j, 0, 0)),
            pl.BlockSpec((None, D_MODEL, tn), lambda i, j: (nj + j, 0, 0)),
            pl.BlockSpec((CONV_WIDTH, tn), lambda i, j: (0, j)),
            pl.BlockSpec((CONV_WIDTH, tn), lambda i, j: (0, nj + j)),
            pl.BlockSpec((1, tn), lambda i, j: (0, j)),
            pl.BlockSpec((1, tn), lambda i, j: (0, nj + j)),
            pl.BlockSpec((tn, D_MODEL), lambda i, j: (j, 0)),
            pl.BlockSpec((1, D_MODEL), lambda i, j: (0, 0)),
        ],
        out_specs=pl.BlockSpec((tm, D_MODEL), lambda i, j: (i, 0)),
        out_shape=jax.ShapeDtypeStruct((T, D_MODEL), f32),
        scratch_shapes=[
            pltpu.VMEM((tm + FFN_HALO, D_MODEL), bf16),
        ] + [pltpu.VMEM((tm + FFN_HALO, 2 * MXU_COLS), f32)] * (tn // MXU_COLS),
        compiler_params=_params(("parallel", "arbitrary")),
        name="ffn",
    )(x1, x1, g_pre, w_up, w_up, conv_w, conv_w, conv_b, conv_b, w_down, g_post)


def _rotary_tables(seq):
    d = RET_DK
    inv = 1.0 / (ROPE_BASE ** (jnp.arange(0, d, 2, dtype=f32) / d))
    ang = jnp.arange(seq, dtype=jnp.int32).astype(f32)[:, None] * inv[None, :]
    cos, sin = jnp.cos(ang), jnp.sin(ang)
    tab = jnp.concatenate([cos, cos, -sin, sin], axis=-1)
    return jnp.stack([tab, tab * (d ** -0.5)])


def _tile_major(w, tn):
    k, n = w.shape
    nt = -(-n // tn)
    w = jnp.pad(w.astype(bf16), ((0, 0), (0, nt * tn - n)))
    return w.reshape(k, nt, tn).transpose(1, 0, 2)


def kernel(x, g_pre_mix, w_in, w_ret_o, w_swa_o, w_out, swa_sinks, g_post_mix, g_pre_ffn, w_up, conv_w, conv_b,
           w_down, g_post_ffn):
    B, S, D = x.shape
    depth = w_in.shape[0]
    rot = _rotary_tables(S)
    x2 = x.reshape(B * S, D)
    for l in range(depth):
        proj = _inproj(x2, g_pre_mix[l][None], _tile_major(w_in[l], INPROJ_TN), rot, S)
        ret = _retention(proj, B, S)
        swa = _swa(proj, swa_sinks[l].astype(f32), B, S)
        x2 = _mix(ret, swa, proj, _tile_major(w_ret_o[l], MIX_TN), _tile_major(w_swa_o[l], MIX_TN),
                  w_out[l].astype(bf16), x2, g_post_mix[l][None])
        x2 = _ffn(x2, g_pre_ffn[l][None], _tile_major(w_up[l], FFN_TN), conv_w[l], conv_b[l][None],
                  w_down[l].astype(bf16), g_post_ffn[l][None], S)
    return x2.reshape(B, S, D)
```

```python
import functools

import jax
import jax.numpy as jnp
from jax import lax
from jax.experimental import pallas as pl
from jax.experimental.pallas import tpu as pltpu

D_MODEL = 2048
RET_HEADS = 8
RET_DK = 128
RET_DV = 256
RET_CHUNK = 128
ROPE_BASE = 10000.0
SWA_Q_HEADS = 16
SWA_KV_HEADS = 4
SWA_HEAD_DIM = 64
SWA_BLOCK = 128
D_FF = 5632
CONV_WIDTH = 3
RMS_EPS = 1e-6

RET_QK_W = RET_HEADS * RET_DK
RET_V_W = RET_HEADS * RET_DV
SWA_Q_W = SWA_Q_HEADS * SWA_HEAD_DIM
SWA_KV_W = SWA_KV_HEADS * SWA_HEAD_DIM

COL_RQ = 0
COL_RK = COL_RQ + RET_QK_W
COL_RV = COL_RK + RET_QK_W
COL_RG = COL_RV + RET_V_W
COL_SQ = COL_RG + RET_V_W
COL_SK = COL_SQ + SWA_Q_W
COL_SV = COL_SK + SWA_KV_W
COL_GR = COL_SV + SWA_KV_W
COL_GS = COL_GR + D_MODEL
IN_WIDTH = COL_GS + D_MODEL

V7X_VMEM_LIMIT_BYTES = 60 * 1024 * 1024
BF16_SUBLANES = 16
MXU_COLS = 256
NORM_ROWS = 32
NORM_UNROLL = 4

INPROJ_TM = 1024
INPROJ_TN = 2048
MIX_TM = 512
MIX_TN = 1024
MIX_GATE_W = 512
FFN_TM = 1024
FFN_TN = 512
FFN_HALO = BF16_SUBLANES
FFN_ROW_SPLIT = 4
INPROJ_ROW_SPLIT = 4
RET_HEADS_PER_STEP = 2

f32 = jnp.float32
bf16 = jnp.bfloat16

def _params(semantics):
    return pltpu.CompilerParams(dimension_semantics=semantics, vmem_limit_bytes=V7X_VMEM_LIMIT_BYTES)


def _rmsnorm(x, g):
    ms = jnp.mean(x * x, axis=-1, keepdims=True)
    return x * lax.rsqrt(ms + RMS_EPS) * g


def _norm_rows_to(x_ref, g_ref, dst_ref, dst_off, rows):
    g = g_ref[...]
    group = NORM_ROWS * NORM_UNROLL

    def step(c, carry):
        base = pl.multiple_of(c * group, group)
        for k in range(NORM_UNROLL):
            r = base + k * NORM_ROWS
            dst_ref[pl.ds(dst_off + r, NORM_ROWS), :] = (
                _rmsnorm(x_ref[pl.ds(r, NORM_ROWS), :], g).astype(dst_ref.dtype))
        return carry

    lax.fori_loop(0, rows // group, step, 0)


def _residual_norm_rows(x_ref, y_ref, g_ref, o_ref, rows):
    g = g_ref[...]
    group = NORM_ROWS * NORM_UNROLL

    def step(c, carry):
        base = pl.multiple_of(c * group, group)
        rows_k = [pl.ds(base + k * NORM_ROWS, NORM_ROWS) for k in range(NORM_UNROLL)]
        scales = []
        for r in rows_k:
            y = y_ref[r, :]
            scales.append(lax.rsqrt(jnp.mean(y * y, axis=-1, keepdims=True) + RMS_EPS))
        for r, s in zip(rows_k, scales):
            o_ref[r, :] = x_ref[r, :] + y_ref[r, :] * s * g
        return carry

    lax.fori_loop(0, rows // group, step, 0)


def _inproj_tile_kinds():
    starts = ((COL_RQ, "rot_q"), (COL_RK, "rot_k"), (COL_RV, "cast"), (COL_RG, "silu"), (COL_SQ, "cast"),
              (COL_GR, "sig"))
    chunk_kinds = [[k for c0, k in starts if c0 <= col][-1] for col in range(0, IN_WIDTH, MXU_COLS)]
    per_tile = INPROJ_TN // MXU_COLS
    return [tuple(chunk_kinds[t:t + per_tile]) for t in range(0, len(chunk_kinds), per_tile)]


_INPROJ_TILE_KINDS = _inproj_tile_kinds()


def _inproj_kernel(x_ref, g_ref, w_ref, rot_ref, o_ref, h_scr):
    j = pl.program_id(1)
    tm = x_ref.shape[0]

    @pl.when(j == 0)
    def _():
        _norm_rows_to(x_ref, g_ref, h_scr, 0, tm)

    def rotary(table, acc, rows):
        cos = rot_ref[table, rows, 0:RET_DK]
        sin = rot_ref[table, rows, RET_DK:2 * RET_DK]
        heads = []
        for hh in range(acc.shape[1] // RET_DK):
            xh = acc[:, hh * RET_DK:(hh + 1) * RET_DK]
            heads.append(xh * cos + pltpu.roll(xh, RET_DK // 2, axis=1) * sin)
        return jnp.concatenate(heads, axis=1)

    epilogues = {
        "rot_q": functools.partial(rotary, 0),
        "rot_k": functools.partial(rotary, 1),
        "cast": lambda acc, rows: acc,
        "silu": lambda acc, rows: acc * jax.nn.sigmoid(acc),
        "sig": lambda acc, rows: jax.nn.sigmoid(acc),
    }

    def tile(kinds):
        hm = tm // INPROJ_ROW_SPLIT
        for c, kind in enumerate(kinds):
            cs = slice(c * MXU_COLS, (c + 1) * MXU_COLS)
            w = w_ref[:, cs]
            for m in range(INPROJ_ROW_SPLIT):
                rows = slice(m * hm, (m + 1) * hm)
                acc = jnp.dot(h_scr[rows, :], w, preferred_element_type=f32)
                o_ref[rows, cs] = epilogues[kind](acc, rows).astype(o_ref.dtype)

    for kinds in sorted(set(_INPROJ_TILE_KINDS)):
        tiles = [t for t, k in enumerate(_INPROJ_TILE_KINDS) if k == kinds]
        cond = functools.reduce(lambda a, b: a | b, [j == t for t in tiles])
        pl.when(cond)(functools.partial(tile, kinds))


def _inproj(x2, g, w_in, rot, seq):
    T = x2.shape[0]
    tm, tn = INPROJ_TM, INPROJ_TN
    tm = min(tm, seq)
    spt = seq // tm
    grid = (T // tm, len(_INPROJ_TILE_KINDS))
    return pl.pallas_call(
        _inproj_kernel,
        grid=grid,
        in_specs=[
            pl.BlockSpec((tm, D_MODEL), lambda i, j: (i, 0)),
            pl.BlockSpec((1, D_MODEL), lambda i, j: (0, 0)),
            pl.BlockSpec((D_MODEL, tn), lambda i, j: (0, j)),
            pl.BlockSpec((2, tm, 2 * RET_DK), lambda i, j: (0, i % spt, 0)),
        ],
        out_specs=pl.BlockSpec((tm, tn), lambda i, j: (i, j)),
        out_shape=jax.ShapeDtypeStruct((T, IN_WIDTH), bf16),
        scratch_shapes=[pltpu.VMEM((tm, D_MODEL), bf16)],
        compiler_params=_params(("parallel", "arbitrary")),
        name="inproj",
    )(x2, g, w_in, rot)


def _retention_kernel(q_ref, k_ref, v_ref, gate_ref, dmat_ref, xi_ref, zeta_ref, cdec_ref, o_ref, kv_scr, st_scr):
    S = q_ref.shape[0]
    C = RET_CHUNK
    N = S // C
    heads = range(RET_HEADS_PER_STEP)
    qk = [slice(hh * RET_DK, (hh + 1) * RET_DK) for hh in heads]
    vv = [slice(hh * RET_DV, (hh + 1) * RET_DV) for hh in heads]
    q3 = [q_ref[:, qk[hh]].reshape(N, C, RET_DK) for hh in heads]
    k3 = [k_ref[:, qk[hh]].reshape(N, C, RET_DK) for hh in heads]
    v3 = [v_ref[:, vv[hh]].reshape(N, C, RET_DV) for hh in heads]

    for hh in heads:
        kz = (k3[hh].astype(f32) * zeta_ref[hh][None]).astype(bf16)
        kv_scr[hh] = jnp.einsum("ndk,nkv->ndv", jnp.swapaxes(kz, 1, 2), v3[hh], preferred_element_type=f32)

    cdec = [cdec_ref[hh] for hh in heads]

    def scan(n, states):
        for hh in heads:
            st_scr[hh, n] = states[hh].astype(bf16)
        return tuple(states[hh] * cdec[hh] + kv_scr[hh, n] for hh in heads)

    lax.fori_loop(0, N, scan, tuple(jnp.zeros((RET_DK, RET_DV), f32) for _ in heads))

    for hh in heads:
        sc = jnp.einsum("nqd,nkd->nqk", q3[hh], k3[hh], preferred_element_type=f32) * dmat_ref[hh][None]
        inner = jnp.einsum("nqk,nkv->nqv", sc.astype(bf16), v3[hh], preferred_element_type=f32)
        cross = jnp.einsum("nqd,ndv->nqv", q3[hh], st_scr[hh], preferred_element_type=f32) * xi_ref[hh][None]
        o = inner + cross
        o = o * lax.rsqrt(jnp.mean(o * o, axis=-1, keepdims=True) + RMS_EPS)
        o = gate_ref[:, vv[hh]].astype(f32).reshape(N, C, RET_DV) * o
        o_ref[:, vv[hh]] = o.reshape(S, RET_DV).astype(o_ref.dtype)


def _retention_tables():
    C = RET_CHUNK
    log_gamma = jnp.log(1.0 - 2.0 ** (-5.0 - jnp.arange(RET_HEADS, dtype=f32)))
    idx = jnp.arange(C, dtype=f32)
    rel = idx[:, None] - idx[None, :]
    dmat = jnp.where(rel[None] >= 0, jnp.exp(log_gamma[:, None, None] * jnp.maximum(rel, 0.0)[None]), 0.0)
    xi = jnp.exp(log_gamma[:, None] * (idx + 1.0))
    zeta = jnp.exp(log_gamma[:, None] * (C - 1.0 - idx))
    cdec = jnp.exp(log_gamma * C)
    xi_b = jnp.broadcast_to(xi[:, :, None], (RET_HEADS, C, RET_DV))
    zeta_b = jnp.broadcast_to(zeta[:, :, None], (RET_HEADS, C, RET_DK))
    cdec_b = jnp.broadcast_to(cdec[:, None, None], (RET_HEADS, 1, RET_DV))
    return dmat.astype(f32), xi_b.astype(f32), zeta_b.astype(f32), cdec_b.astype(f32)


def _retention(proj, batch, seq):
    T = proj.shape[0]
    C = RET_CHUNK
    N = seq // C
    dmat, xi_b, zeta_b, cdec_b = _retention_tables()
    hp = RET_HEADS_PER_STEP
    qk_w, v_w = hp * RET_DK, hp * RET_DV
    return pl.pallas_call(
        _retention_kernel,
        grid=(batch, RET_HEADS // hp),
        in_specs=[
            pl.BlockSpec((seq, qk_w), lambda b, h: (b, COL_RQ // qk_w + h)),
            pl.BlockSpec((seq, qk_w), lambda b, h: (b, COL_RK // qk_w + h)),
            pl.BlockSpec((seq, v_w), lambda b, h: (b, COL_RV // v_w + h)),
            pl.BlockSpec((seq, v_w), lambda b, h: (b, COL_RG // v_w + h)),
            pl.BlockSpec((hp, C, C), lambda b, h: (h, 0, 0)),
            pl.BlockSpec((hp, C, RET_DV), lambda b, h: (h, 0, 0)),
            pl.BlockSpec((hp, C, RET_DK), lambda b, h: (h, 0, 0)),
            pl.BlockSpec((hp, 1, RET_DV), lambda b, h: (h, 0, 0)),
        ],
        out_specs=pl.BlockSpec((seq, v_w), lambda b, h: (b, h)),
        out_shape=jax.ShapeDtypeStruct((T, RET_V_W), bf16),
        scratch_shapes=[pltpu.VMEM((hp, N, RET_DK, RET_DV), f32), pltpu.VMEM((hp, N, RET_DK, RET_DV), bf16)],
        compiler_params=_params(("parallel", "parallel")),
        name="retention",
    )(proj, proj, proj, proj, dmat, xi_b, zeta_b, cdec_b)


KV_PER_STEP = 2
Q_PER_KV = SWA_Q_HEADS // SWA_KV_HEADS
SWA_Q_STEP_W = KV_PER_STEP * Q_PER_KV * SWA_HEAD_DIM
LANES = 128
SWA_UNROLL = 8


def _swa_kernel(sink_ref, bias_ref, q_ref, k_ref, v_ref, o_ref, klo, khi, vlo, vhi):
    S = q_ref.shape[0]
    C = SWA_BLOCK
    gp = pl.program_id(1)
    half = SWA_HEAD_DIM

    lane_s = lax.broadcasted_iota(jnp.int32, (S, LANES), 1)
    low_s = lane_s < half

    def prep(src_ref, lo_scr, hi_scr, scale, with_ones):
        w = src_ref[...].astype(f32) * scale
        r = pltpu.roll(w, half, axis=1)
        zeros = jnp.zeros((C, lo_scr.shape[2]), bf16)
        for t in range(KV_PER_STEP):
            lo_scr[t, 0:C, :] = zeros
            hi_scr[t, 0:C, :] = zeros
        lo_scr[0, C:C + S, 0:LANES] = jnp.where(low_s, w, 0.0).astype(bf16)
        hi_scr[0, C:C + S, 0:LANES] = jnp.where(low_s, 0.0, r).astype(bf16)
        lo_scr[1, C:C + S, 0:LANES] = jnp.where(low_s, r, 0.0).astype(bf16)
        hi_scr[1, C:C + S, 0:LANES] = jnp.where(low_s, 0.0, w).astype(bf16)
        if with_ones:
            for t in range(KV_PER_STEP):
                lo_scr[t, C:C + S, LANES:2 * LANES] = jnp.where(low_s, 1.0, 0.0).astype(bf16)
                hi_scr[t, C:C + S, LANES:2 * LANES] = jnp.where(low_s, 0.0, 1.0).astype(bf16)

    prep(k_ref, klo, khi, SWA_HEAD_DIM ** -0.5, False)
    prep(v_ref, vlo, vhi, 1.0, True)

    rows = 2 * C
    win = 2 * C
    first_pair = lax.broadcasted_iota(jnp.int32, (rows, 1), 0) < C
    low_o = lax.broadcasted_iota(jnp.int32, (rows, LANES), 1) < half
    nt = (((1,), (1,)), ((), ()))

    def softmax_parts(s, sink, bias):
        s = s + bias
        m = jnp.maximum(jnp.max(s, axis=-1, keepdims=True), sink)
        return jnp.exp(s - m).astype(bf16), jnp.exp(sink - m)

    def block(n, carry):
        r0 = pl.multiple_of(n * C, C)
        bias = bias_ref[jnp.minimum(n, 1)]
        for t in range(KV_PER_STEP):
            c0 = t * Q_PER_KV * SWA_HEAD_DIM
            hbase = gp * (KV_PER_STEP * Q_PER_KV) + t * Q_PER_KV
            qs = jnp.concatenate([q_ref[pl.ds(r0, C), c0:c0 + LANES],
                                  q_ref[pl.ds(r0, C), c0 + LANES:c0 + 2 * LANES]], axis=0)
            s_e = lax.dot_general(qs, klo[t, pl.ds(r0, win), :], nt, preferred_element_type=f32)
            s_o = lax.dot_general(qs, khi[t, pl.ds(r0, win), :], nt, preferred_element_type=f32)
            sink_e = jnp.where(first_pair, sink_ref[hbase + 0], sink_ref[hbase + 2])
            sink_o = jnp.where(first_pair, sink_ref[hbase + 1], sink_ref[hbase + 3])
            p_e, z_e = softmax_parts(s_e, sink_e, bias)
            p_o, z_o = softmax_parts(s_o, sink_o, bias)
            pv = (jnp.dot(p_e, vlo[t, pl.ds(r0, win), :], preferred_element_type=f32)
                  + jnp.dot(p_o, vhi[t, pl.ds(r0, win), :], preferred_element_type=f32))
            den = pv[:, LANES:2 * LANES] + jnp.where(low_o, z_e, z_o)
            o = pv[:, 0:LANES] / den
            o_ref[pl.ds(r0, C), c0:c0 + LANES] = o[0:C].astype(o_ref.dtype)
            o_ref[pl.ds(r0, C), c0 + LANES:c0 + 2 * LANES] = o[C:2 * C].astype(o_ref.dtype)
        return carry

    lax.fori_loop(0, S // C, block, 0, unroll=SWA_UNROLL)


def _swa_bias():
    C = SWA_BLOCK
    row_i = jnp.arange(2 * C)[:, None] % C
    col_j = jnp.arange(2 * C)[None, :]
    band = (col_j > row_i) & (col_j <= row_i + C)
    first = band & (col_j >= C)
    return jnp.where(jnp.stack([first, band]), 0.0, -jnp.inf).astype(f32)


def _swa(proj, sinks, batch, seq):
    T = proj.shape[0]
    steps = SWA_KV_HEADS // KV_PER_STEP
    k_scr = pltpu.VMEM((KV_PER_STEP, seq + SWA_BLOCK, LANES), bf16)
    v_scr = pltpu.VMEM((KV_PER_STEP, seq + SWA_BLOCK, 2 * LANES), bf16)
    bias = _swa_bias()
    return pl.pallas_call(
        _swa_kernel,
        grid=(batch, steps),
        in_specs=[
            pl.BlockSpec(memory_space=pltpu.SMEM),
            pl.BlockSpec(bias.shape, lambda b, g: (0, 0, 0)),
            pl.BlockSpec((seq, SWA_Q_STEP_W), lambda b, g: (b, COL_SQ // SWA_Q_STEP_W + g)),
            pl.BlockSpec((seq, LANES), lambda b, g: (b, COL_SK // LANES + g)),
            pl.BlockSpec((seq, LANES), lambda b, g: (b, COL_SV // LANES + g)),
        ],
        out_specs=pl.BlockSpec((seq, SWA_Q_STEP_W), lambda b, g: (b, g)),
        out_shape=jax.ShapeDtypeStruct((T, SWA_Q_W), bf16),
        scratch_shapes=[k_scr, k_scr, v_scr, v_scr],
        compiler_params=_params(("parallel", "parallel")),
        name="swa",
    )(sinks, bias, proj, proj, proj)


def _mix_kernel(ret_ref, swa_ref, *refs):
    n_gate = MIX_TN // MIX_GATE_W
    gr_refs, gs_refs = refs[:n_gate], refs[n_gate:2 * n_gate]
    wr_ref, ws_ref, wo_ref, x_ref, g_ref, o_ref = refs[2 * n_gate:]
    j = pl.program_id(1)

    @pl.when(j == 0)
    def _():
        o_ref[...] = jnp.zeros_like(o_ref)

    part = None
    for c in range(n_gate):
        cs = slice(c * MIX_GATE_W, (c + 1) * MIX_GATE_W)
        ret_out = jnp.dot(ret_ref[...], wr_ref[:, cs], preferred_element_type=f32)
        swa_out = jnp.dot(swa_ref[...], ws_ref[:, cs], preferred_element_type=f32)
        mixed = gr_refs[c][...].astype(f32) * ret_out + gs_refs[c][...].astype(f32) * swa_out
        d = jnp.dot(mixed.astype(bf16), wo_ref[cs, :], preferred_element_type=f32)
        part = d if part is None else part + d
    o_ref[...] += part

    @pl.when(j == pl.num_programs(1) - 1)
    def _():
        _residual_norm_rows(x_ref, o_ref, g_ref, o_ref, o_ref.shape[0])


def _mix(ret, swa, proj, w_ret_o, w_swa_o, w_out, x2, g):
    T = x2.shape[0]
    tm, tn = min(MIX_TM, T), MIX_TN
    nj = D_MODEL // tn
    gw = MIX_GATE_W
    n_gate = tn // gw

    def gate_specs(col0):
        return [pl.BlockSpec((tm, gw), functools.partial(lambda i, j, c: (i, col0 // gw + j * n_gate + c), c=c))
                for c in range(n_gate)]

    return pl.pallas_call(
        _mix_kernel,
        grid=(T // tm, nj),
        in_specs=[
            pl.BlockSpec((tm, RET_V_W), lambda i, j: (i, 0)),
            pl.BlockSpec((tm, SWA_Q_W), lambda i, j: (i, 0)),
            *gate_specs(COL_GR),
            *gate_specs(COL_GS),
            pl.BlockSpec((RET_V_W, tn), lambda i, j: (0, j)),
            pl.BlockSpec((SWA_Q_W, tn), lambda i, j: (0, j)),
            pl.BlockSpec((tn, D_MODEL), lambda i, j: (j, 0)),
            pl.BlockSpec((tm, D_MODEL), lambda i, j: (i, 0)),
            pl.BlockSpec((1, D_MODEL), lambda i, j: (0, 0)),
        ],
        out_specs=pl.BlockSpec((tm, D_MODEL), lambda i, j: (i, 0)),
        out_shape=jax.ShapeDtypeStruct((T, D_MODEL), f32),
        compiler_params=_params(("parallel", "arbitrary")),
        name="mix",
    )(ret, swa, *([proj] * (2 * n_gate)), w_ret_o, w_swa_o, w_out, x2, g)


def _gelu_tanh(x):
    return x * (0.5 * (1.0 + jnp.tanh(0.7978845608028654 * (x + 0.044715 * (x * x * x)))))


def _ffn_kernel(x_ref, halo_ref, gpre_ref, wv_ref, wg_ref, cwv_ref, cwg_ref, cbv_ref, cbg_ref, wd_ref, gpost_ref,
                o_ref, h_scr, *u_scrs, tiles_per_seq):
    i = pl.program_id(0)
    j = pl.program_id(1)
    tm = x_ref.shape[0]
    tn = wv_ref.shape[1]
    H = FFN_HALO

    @pl.when(j == 0)
    def _():
        keep = jnp.where(i % tiles_per_seq == 0, 0.0, 1.0)
        h_scr[0:H, :] = (_rmsnorm(halo_ref[...], gpre_ref[...]) * keep).astype(bf16)
        _norm_rows_to(x_ref, gpre_ref, h_scr, H, tm)
        o_ref[...] = jnp.zeros_like(o_ref)

    def conv(u_scr, row0, nrows, u_cols, w_cols, cw_ref, cb_ref):
        y = cb_ref[:, w_cols]
        for kk in range(CONV_WIDTH):
            y = y + u_scr[pl.ds(H - (CONV_WIDTH - 1) + kk + row0, nrows), u_cols] * cw_ref[kk:kk + 1, w_cols]
        return y

    h = h_scr[...]
    hm = tm // FFN_ROW_SPLIT
    parts = [None] * FFN_ROW_SPLIT
    uv = slice(0, MXU_COLS)
    ug = slice(MXU_COLS, 2 * MXU_COLS)
    n_chunks = len(u_scrs)
    for c, u_scr in enumerate(u_scrs):
        cs = slice(c * MXU_COLS, (c + 1) * MXU_COLS)
        if c < n_chunks - 1:
            u_scr[:, uv] = jnp.dot(h, wv_ref[:, cs], preferred_element_type=f32)
            u_scr[:, ug] = jnp.dot(h, wg_ref[:, cs], preferred_element_type=f32)
        else:
            wv, wg = wv_ref[:, cs], wg_ref[:, cs]
            r0 = 0
            for m in range(FFN_ROW_SPLIT):
                r1 = H + (m + 1) * hm
                u_scr[r0:r1, uv] = jnp.dot(h_scr[r0:r1, :], wv, preferred_element_type=f32)
                u_scr[r0:r1, ug] = jnp.dot(h_scr[r0:r1, :], wg, preferred_element_type=f32)
                r0 = r1
        for m in range(FFN_ROW_SPLIT):
            val = conv(u_scr, m * hm, hm, uv, cs, cwv_ref, cbv_ref)
            gate = conv(u_scr, m * hm, hm, ug, cs, cwg_ref, cbg_ref)
            a = (_gelu_tanh(gate) * val).astype(bf16)
            d = jnp.dot(a, wd_ref[cs, :], preferred_element_type=f32)
            parts[m] = d if parts[m] is None else parts[m] + d
    for m in range(FFN_ROW_SPLIT):
        o_ref[m * hm:(m + 1) * hm, :] += parts[m]

    @pl.when(j == pl.num_programs(1) - 1)
    def _():
        _residual_norm_rows(x_ref, o_ref, gpost_ref, o_ref, tm)


def _ffn(x1, g_pre, w_up, conv_w, conv_b, w_down, g_post, seq):
    T = x1.shape[0]
    tm, tn = min(FFN_TM, seq), FFN_TN
    nj = D_FF // tn
    hb = tm // FFN_HALO
    kern = functools.partial(_ffn_kernel, tiles_per_seq=seq // tm)
    return pl.pallas_call(
        kern,
        grid=(T // tm, nj),
        in_specs=[
            pl.BlockSpec((tm, D_MODEL), lambda i, j: (i, 0)),
            pl.BlockSpec((FFN_HALO, D_MODEL), lambda i, j: (jnp.maximum(i * hb - 1, 0), 0)),
            pl.BlockSpec((1, D_MODEL), lambda i, j: (0, 0)),
            pl.BlockSpec((D_MODEL, tn), lambda i, j: (0, j)),
            pl.BlockSpec((D_MODEL, tn), lambda i, j: (0, nj + j)),
            pl.BlockSpec((CONV_WIDTH, tn), lambda i, j: (0, j)),
            pl.BlockSpec((CONV_WIDTH, tn), lambda i, j: (0, nj + j)),
            pl.BlockSpec((1, tn), lambda i, j: (0, j)),
            pl.BlockSpec((1, tn), lambda i, j: (0, nj + j)),
            pl.BlockSpec((tn, D_MODEL), lambda i, j: (j, 0)),
            pl.BlockSpec((1, D_MODEL), lambda i, j: (0, 0)),
        ],
        out_specs=pl.BlockSpec((tm, D_MODEL), lambda i, j: (i, 0)),
        out_shape=jax.ShapeDtypeStruct((T, D_MODEL), f32),
        scratch_shapes=[
            pltpu.VMEM((tm + FFN_HALO, D_MODEL), bf16),
        ] + [pltpu.VMEM((tm + FFN_HALO, 2 * MXU_COLS), f32)] * (tn // MXU_COLS),
        compiler_params=_params(("parallel", "arbitrary")),
        name="ffn",
    )(x1, x1, g_pre, w_up, w_up, conv_w, conv_w, conv_b, conv_b, w_down, g_post)


def _rotary_tables(seq):
    d = RET_DK
    inv = 1.0 / (ROPE_BASE ** (jnp.arange(0, d, 2, dtype=f32) / d))
    ang = jnp.arange(seq, dtype=jnp.int32).astype(f32)[:, None] * inv[None, :]
    cos, sin = jnp.cos(ang), jnp.sin(ang)
    tab = jnp.concatenate([cos, cos, -sin, sin], axis=-1)
    return jnp.stack([tab, tab * (d ** -0.5)])


def kernel(x, g_pre_mix, w_in, w_ret_o, w_swa_o, w_out, swa_sinks, g_post_mix, g_pre_ffn, w_up, conv_w, conv_b,
           w_down, g_post_ffn):
    B, S, D = x.shape
    depth = w_in.shape[0]
    rot = _rotary_tables(S)
    x2 = x.reshape(B * S, D)
    for l in range(depth):
        proj = _inproj(x2, g_pre_mix[l][None], w_in[l].astype(bf16), rot, S)
        ret = _retention(proj, B, S)
        swa = _swa(proj, swa_sinks[l].astype(f32), B, S)
        x2 = _mix(ret, swa, proj, w_ret_o[l].astype(bf16), w_swa_o[l].astype(bf16), w_out[l].astype(bf16), x2,
                  g_post_mix[l][None])
        x2 = _ffn(x2, g_pre_ffn[l][None], w_up[l].astype(bf16), conv_w[l], conv_b[l][None], w_down[l].astype(bf16),
                  g_post_ffn[l][None], S)
    return x2.reshape(B, S, D)
```

```python
import functools

import jax
import jax.numpy as jnp
from jax import lax
from jax.experimental import pallas as pl
from jax.experimental.pallas import tpu as pltpu

D_MODEL = 2048
RET_HEADS = 8
RET_DK = 128
RET_DV = 256
RET_CHUNK = 256
ROPE_BASE = 10000.0
SWA_Q_HEADS = 16
SWA_KV_HEADS = 4
SWA_HEAD_DIM = 64
SWA_BLOCK = 128
D_FF = 5632
CONV_WIDTH = 3
RMS_EPS = 1e-6

RET_QK_W = RET_HEADS * RET_DK
RET_V_W = RET_HEADS * RET_DV
SWA_Q_W = SWA_Q_HEADS * SWA_HEAD_DIM
SWA_KV_W = SWA_KV_HEADS * SWA_HEAD_DIM

COL_RQ = 0
COL_RK = COL_RQ + RET_QK_W
COL_RV = COL_RK + RET_QK_W
COL_RG = COL_RV + RET_V_W
COL_SQ = COL_RG + RET_V_W
COL_SK = COL_SQ + SWA_Q_W
COL_SV = COL_SK + SWA_KV_W
COL_GR = COL_SV + SWA_KV_W
COL_GS = COL_GR + D_MODEL
IN_WIDTH = COL_GS + D_MODEL

V7X_VMEM_LIMIT_BYTES = 60 * 1024 * 1024
BF16_SUBLANES = 16
MXU_COLS = 256
NORM_ROWS = 32
NORM_UNROLL = 4

INPROJ_TM = 1024
INPROJ_TN = 1024
MIX_TM = 512
MIX_TN = 1024
MIX_GATE_W = 512
FFN_TM = 1024
FFN_TN = 512
FFN_HALO = BF16_SUBLANES
FFN_ROW_SPLIT = 4
INPROJ_ROW_SPLIT = 4
RET_HEADS_PER_STEP = 2

f32 = jnp.float32
bf16 = jnp.bfloat16

def _params(semantics):
    return pltpu.CompilerParams(dimension_semantics=semantics, vmem_limit_bytes=V7X_VMEM_LIMIT_BYTES)


def _rmsnorm(x, g):
    ms = jnp.mean(x * x, axis=-1, keepdims=True)
    return x * lax.rsqrt(ms + RMS_EPS) * g


def _norm_rows_to(x_ref, g_ref, dst_ref, dst_off, rows):
    g = g_ref[...]
    group = NORM_ROWS * NORM_UNROLL

    def step(c, carry):
        base = pl.multiple_of(c * group, group)
        for k in range(NORM_UNROLL):
            r = base + k * NORM_ROWS
            dst_ref[pl.ds(dst_off + r, NORM_ROWS), :] = (
                _rmsnorm(x_ref[pl.ds(r, NORM_ROWS), :], g).astype(dst_ref.dtype))
        return carry

    lax.fori_loop(0, rows // group, step, 0)


def _residual_norm_rows(x_ref, y_ref, g_ref, o_ref, rows):
    g = g_ref[...]
    group = NORM_ROWS * NORM_UNROLL

    def step(c, carry):
        base = pl.multiple_of(c * group, group)
        rows_k = [pl.ds(base + k * NORM_ROWS, NORM_ROWS) for k in range(NORM_UNROLL)]
        scales = []
        for r in rows_k:
            y = y_ref[r, :]
            scales.append(lax.rsqrt(jnp.mean(y * y, axis=-1, keepdims=True) + RMS_EPS))
        for r, s in zip(rows_k, scales):
            o_ref[r, :] = x_ref[r, :] + y_ref[r, :] * s * g
        return carry

    lax.fori_loop(0, rows // group, step, 0)


def _inproj_tile_kinds():
    starts = ((COL_RQ, "rot"), (COL_RK, "rot"), (COL_RV, "cast"), (COL_RG, "silu"), (COL_SQ, "cast"),
              (COL_GR, "sig"))
    chunk_kinds = [[k for c0, k in starts if c0 <= col][-1] for col in range(0, IN_WIDTH, MXU_COLS)]
    per_tile = INPROJ_TN // MXU_COLS
    return [tuple(chunk_kinds[t:t + per_tile]) for t in range(0, len(chunk_kinds), per_tile)]


_INPROJ_TILE_KINDS = _inproj_tile_kinds()
_T_RK = COL_RK // INPROJ_TN
assert COL_RK % INPROJ_TN == 0 and COL_RV % INPROJ_TN == 0


def _inproj_kernel(x_ref, g_ref, w_ref, rot_ref, o_ref, h_scr):
    j = pl.program_id(1)
    tm = x_ref.shape[0]

    @pl.when(j == 0)
    def _():
        _norm_rows_to(x_ref, g_ref, h_scr, 0, tm)

    def rotary(acc, rows):
        cos = rot_ref[rows, 0:RET_DK]
        sin = rot_ref[rows, RET_DK:2 * RET_DK]
        heads = []
        for hh in range(acc.shape[1] // RET_DK):
            xh = acc[:, hh * RET_DK:(hh + 1) * RET_DK]
            heads.append(xh * cos + pltpu.roll(xh, RET_DK // 2, axis=1) * sin)
        return jnp.concatenate(heads, axis=1)

    epilogues = {
        "rot": rotary,
        "cast": lambda acc, rows: acc,
        "silu": lambda acc, rows: acc * jax.nn.sigmoid(acc),
        "sig": lambda acc, rows: jax.nn.sigmoid(acc),
    }

    def tile(kinds):
        hm = tm // INPROJ_ROW_SPLIT
        for c, kind in enumerate(kinds):
            cs = slice(c * MXU_COLS, (c + 1) * MXU_COLS)
            w = w_ref[:, cs]
            for m in range(INPROJ_ROW_SPLIT):
                rows = slice(m * hm, (m + 1) * hm)
                acc = jnp.dot(h_scr[rows, :], w, preferred_element_type=f32)
                o_ref[rows, cs] = epilogues[kind](acc, rows).astype(o_ref.dtype)

    for kinds in sorted(set(_INPROJ_TILE_KINDS)):
        tiles = [t for t, k in enumerate(_INPROJ_TILE_KINDS) if k == kinds]
        cond = functools.reduce(lambda a, b: a | b, [j == t for t in tiles])
        pl.when(cond)(functools.partial(tile, kinds))


def _inproj(x2, g, w_in, rot, seq):
    T = x2.shape[0]
    tm, tn = INPROJ_TM, INPROJ_TN
    tm = min(tm, seq)
    spt = seq // tm
    grid = (T // tm, len(_INPROJ_TILE_KINDS))
    return pl.pallas_call(
        _inproj_kernel,
        grid=grid,
        in_specs=[
            pl.BlockSpec((tm, D_MODEL), lambda i, j: (i, 0)),
            pl.BlockSpec((1, D_MODEL), lambda i, j: (0, 0)),
            pl.BlockSpec((D_MODEL, tn), lambda i, j: (0, j)),
            pl.BlockSpec((None, tm, 2 * RET_DK), lambda i, j: (jnp.where(j >= _T_RK, 1, 0), i % spt, 0)),
        ],
        out_specs=pl.BlockSpec((tm, tn), lambda i, j: (i, j)),
        out_shape=jax.ShapeDtypeStruct((T, IN_WIDTH), bf16),
        scratch_shapes=[pltpu.VMEM((tm, D_MODEL), bf16)],
        compiler_params=_params(("parallel", "arbitrary")),
        name="inproj",
    )(x2, g, w_in, rot)


def _retention_kernel(q_ref, k_ref, v_ref, gate_ref, dmat_ref, xi_ref, zeta_ref, cdec_ref, o_ref, kv_scr, st_scr):
    S = q_ref.shape[0]
    C = RET_CHUNK
    N = S // C
    heads = range(RET_HEADS_PER_STEP)
    qk = [slice(hh * RET_DK, (hh + 1) * RET_DK) for hh in heads]
    vv = [slice(hh * RET_DV, (hh + 1) * RET_DV) for hh in heads]
    q3 = [q_ref[:, qk[hh]].reshape(N, C, RET_DK) for hh in heads]
    k3 = [k_ref[:, qk[hh]].reshape(N, C, RET_DK) for hh in heads]
    v3 = [v_ref[:, vv[hh]].reshape(N, C, RET_DV) for hh in heads]

    for hh in heads:
        kz = (k3[hh].astype(f32) * zeta_ref[hh][None]).astype(bf16)
        kv_scr[hh] = jnp.einsum("ndk,nkv->ndv", jnp.swapaxes(kz, 1, 2), v3[hh], preferred_element_type=f32)

    cdec = [cdec_ref[hh] for hh in heads]

    def scan(n, states):
        for hh in heads:
            st_scr[hh, n] = states[hh].astype(bf16)
        return tuple(states[hh] * cdec[hh] + kv_scr[hh, n] for hh in heads)

    lax.fori_loop(0, N, scan, tuple(jnp.zeros((RET_DK, RET_DV), f32) for _ in heads))

    for hh in heads:
        sc = jnp.einsum("nqd,nkd->nqk", q3[hh], k3[hh], preferred_element_type=f32) * dmat_ref[hh][None]
        inner = jnp.einsum("nqk,nkv->nqv", sc.astype(bf16), v3[hh], preferred_element_type=f32)
        cross = jnp.einsum("nqd,ndv->nqv", q3[hh], st_scr[hh], preferred_element_type=f32) * xi_ref[hh][None]
        o = inner + cross
        o = o * lax.rsqrt(jnp.mean(o * o, axis=-1, keepdims=True) + RMS_EPS)
        o = gate_ref[:, vv[hh]].astype(f32).reshape(N, C, RET_DV) * o
        o_ref[:, vv[hh]] = o.reshape(S, RET_DV).astype(o_ref.dtype)


def _retention_tables():
    C = RET_CHUNK
    log_gamma = jnp.log(1.0 - 2.0 ** (-5.0 - jnp.arange(RET_HEADS, dtype=f32)))
    idx = jnp.arange(C, dtype=f32)
    rel = idx[:, None] - idx[None, :]
    dmat = jnp.where(rel[None] >= 0, jnp.exp(log_gamma[:, None, None] * jnp.maximum(rel, 0.0)[None]), 0.0)
    xi = jnp.exp(log_gamma[:, None] * (idx + 1.0))
    zeta = jnp.exp(log_gamma[:, None] * (C - 1.0 - idx))
    cdec = jnp.exp(log_gamma * C)
    xi_b = jnp.broadcast_to(xi[:, :, None], (RET_HEADS, C, RET_DV))
    zeta_b = jnp.broadcast_to(zeta[:, :, None], (RET_HEADS, C, RET_DK))
    cdec_b = jnp.broadcast_to(cdec[:, None, None], (RET_HEADS, 1, RET_DV))
    return dmat.astype(f32), xi_b.astype(f32), zeta_b.astype(f32), cdec_b.astype(f32)


def _retention(proj, batch, seq):
    T = proj.shape[0]
    C = RET_CHUNK
    N = seq // C
    dmat, xi_b, zeta_b, cdec_b = _retention_tables()
    hp = RET_HEADS_PER_STEP
    qk_w, v_w = hp * RET_DK, hp * RET_DV
    return pl.pallas_call(
        _retention_kernel,
        grid=(batch, RET_HEADS // hp),
        in_specs=[
            pl.BlockSpec((seq, qk_w), lambda b, h: (b, COL_RQ // qk_w + h)),
            pl.BlockSpec((seq, qk_w), lambda b, h: (b, COL_RK // qk_w + h)),
            pl.BlockSpec((seq, v_w), lambda b, h: (b, COL_RV // v_w + h)),
            pl.BlockSpec((seq, v_w), lambda b, h: (b, COL_RG // v_w + h)),
            pl.BlockSpec((hp, C, C), lambda b, h: (h, 0, 0)),
            pl.BlockSpec((hp, C, RET_DV), lambda b, h: (h, 0, 0)),
            pl.BlockSpec((hp, C, RET_DK), lambda b, h: (h, 0, 0)),
            pl.BlockSpec((hp, 1, RET_DV), lambda b, h: (h, 0, 0)),
        ],
        out_specs=pl.BlockSpec((seq, v_w), lambda b, h: (b, h)),
        out_shape=jax.ShapeDtypeStruct((T, RET_V_W), bf16),
        scratch_shapes=[pltpu.VMEM((hp, N, RET_DK, RET_DV), f32), pltpu.VMEM((hp, N, RET_DK, RET_DV), bf16)],
        compiler_params=_params(("parallel", "parallel")),
        name="retention",
    )(proj, proj, proj, proj, dmat, xi_b, zeta_b, cdec_b)


KV_PER_STEP = 2
Q_PER_KV = SWA_Q_HEADS // SWA_KV_HEADS
SWA_Q_STEP_W = KV_PER_STEP * Q_PER_KV * SWA_HEAD_DIM
LANES = 128
SWA_UNROLL = 8


def _swa_kernel(sink_ref, bias_ref, q_ref, k_ref, v_ref, o_ref, klo, khi, vlo, vhi):
    S = q_ref.shape[0]
    C = SWA_BLOCK
    gp = pl.program_id(1)
    half = SWA_HEAD_DIM

    lane_s = lax.broadcasted_iota(jnp.int32, (S, LANES), 1)
    low_s = lane_s < half

    def prep(src_ref, lo_scr, hi_scr, scale, with_ones):
        w = src_ref[...].astype(f32) * scale
        r = pltpu.roll(w, half, axis=1)
        zeros = jnp.zeros((C, lo_scr.shape[2]), bf16)
        for t in range(KV_PER_STEP):
            lo_scr[t, 0:C, :] = zeros
            hi_scr[t, 0:C, :] = zeros
        lo_scr[0, C:C + S, 0:LANES] = jnp.where(low_s, w, 0.0).astype(bf16)
        hi_scr[0, C:C + S, 0:LANES] = jnp.where(low_s, 0.0, r).astype(bf16)
        lo_scr[1, C:C + S, 0:LANES] = jnp.where(low_s, r, 0.0).astype(bf16)
        hi_scr[1, C:C + S, 0:LANES] = jnp.where(low_s, 0.0, w).astype(bf16)
        if with_ones:
            for t in range(KV_PER_STEP):
                lo_scr[t, C:C + S, LANES:2 * LANES] = jnp.where(low_s, 1.0, 0.0).astype(bf16)
                hi_scr[t, C:C + S, LANES:2 * LANES] = jnp.where(low_s, 0.0, 1.0).astype(bf16)

    prep(k_ref, klo, khi, SWA_HEAD_DIM ** -0.5, False)
    prep(v_ref, vlo, vhi, 1.0, True)

    rows = 2 * C
    win = 2 * C
    first_pair = lax.broadcasted_iota(jnp.int32, (rows, 1), 0) < C
    low_o = lax.broadcasted_iota(jnp.int32, (rows, LANES), 1) < half
    nt = (((1,), (1,)), ((), ()))

    def softmax_parts(s, sink, bias):
        s = s + bias
        m = jnp.maximum(jnp.max(s, axis=-1, keepdims=True), sink)
        return jnp.exp(s - m).astype(bf16), jnp.exp(sink - m)

    def block(n, carry):
        r0 = pl.multiple_of(n * C, C)
        bias = bias_ref[jnp.minimum(n, 1)]
        for t in range(KV_PER_STEP):
            c0 = t * Q_PER_KV * SWA_HEAD_DIM
            hbase = gp * (KV_PER_STEP * Q_PER_KV) + t * Q_PER_KV
            qs = jnp.concatenate([q_ref[pl.ds(r0, C), c0:c0 + LANES],
                                  q_ref[pl.ds(r0, C), c0 + LANES:c0 + 2 * LANES]], axis=0)
            s_e = lax.dot_general(qs, klo[t, pl.ds(r0, win), :], nt, preferred_element_type=f32)
            s_o = lax.dot_general(qs, khi[t, pl.ds(r0, win), :], nt, preferred_element_type=f32)
            sink_e = jnp.where(first_pair, sink_ref[hbase + 0], sink_ref[hbase + 2])
            sink_o = jnp.where(first_pair, sink_ref[hbase + 1], sink_ref[hbase + 3])
            p_e, z_e = softmax_parts(s_e, sink_e, bias)
            p_o, z_o = softmax_parts(s_o, sink_o, bias)
            pv = (jnp.dot(p_e, vlo[t, pl.ds(r0, win), :], preferred_element_type=f32)
                  + jnp.dot(p_o, vhi[t, pl.ds(r0, win), :], preferred_element_type=f32))
            den = pv[:, LANES:2 * LANES] + jnp.where(low_o, z_e, z_o)
            o = pv[:, 0:LANES] / den
            o_ref[pl.ds(r0, C), c0:c0 + LANES] = o[0:C].astype(o_ref.dtype)
            o_ref[pl.ds(r0, C), c0 + LANES:c0 + 2 * LANES] = o[C:2 * C].astype(o_ref.dtype)
        return carry

    lax.fori_loop(0, S // C, block, 0, unroll=SWA_UNROLL)


def _swa_bias():
    C = SWA_BLOCK
    row_i = jnp.arange(2 * C)[:, None] % C
    col_j = jnp.arange(2 * C)[None, :]
    band = (col_j > row_i) & (col_j <= row_i + C)
    first = band & (col_j >= C)
    return jnp.where(jnp.stack([first, band]), 0.0, -jnp.inf).astype(f32)


def _swa(proj, sinks, batch, seq):
    T = proj.shape[0]
    steps = SWA_KV_HEADS // KV_PER_STEP
    k_scr = pltpu.VMEM((KV_PER_STEP, seq + SWA_BLOCK, LANES), bf16)
    v_scr = pltpu.VMEM((KV_PER_STEP, seq + SWA_BLOCK, 2 * LANES), bf16)
    bias = _swa_bias()
    return pl.pallas_call(
        _swa_kernel,
        grid=(batch, steps),
        in_specs=[
            pl.BlockSpec(memory_space=pltpu.SMEM),
            pl.BlockSpec(bias.shape, lambda b, g: (0, 0, 0)),
            pl.BlockSpec((seq, SWA_Q_STEP_W), lambda b, g: (b, COL_SQ // SWA_Q_STEP_W + g)),
            pl.BlockSpec((seq, LANES), lambda b, g: (b, COL_SK // LANES + g)),
            pl.BlockSpec((seq, LANES), lambda b, g: (b, COL_SV // LANES + g)),
        ],
        out_specs=pl.BlockSpec((seq, SWA_Q_STEP_W), lambda b, g: (b, g)),
        out_shape=jax.ShapeDtypeStruct((T, SWA_Q_W), bf16),
        scratch_shapes=[k_scr, k_scr, v_scr, v_scr],
        compiler_params=_params(("parallel", "parallel")),
        name="swa",
    )(sinks, bias, proj, proj, proj)


def _mix_kernel(ret_ref, swa_ref, *refs):
    n_gate = MIX_TN // MIX_GATE_W
    gr_refs, gs_refs = refs[:n_gate], refs[n_gate:2 * n_gate]
    wr_ref, ws_ref, wo_ref, x_ref, g_ref, o_ref = refs[2 * n_gate:]
    j = pl.program_id(1)

    @pl.when(j == 0)
    def _():
        o_ref[...] = jnp.zeros_like(o_ref)

    part = None
    for c in range(n_gate):
        cs = slice(c * MIX_GATE_W, (c + 1) * MIX_GATE_W)
        ret_out = jnp.dot(ret_ref[...], wr_ref[:, cs], preferred_element_type=f32)
        swa_out = jnp.dot(swa_ref[...], ws_ref[:, cs], preferred_element_type=f32)
        mixed = gr_refs[c][...].astype(f32) * ret_out + gs_refs[c][...].astype(f32) * swa_out
        d = jnp.dot(mixed.astype(bf16), wo_ref[cs, :], preferred_element_type=f32)
        part = d if part is None else part + d
    o_ref[...] += part

    @pl.when(j == pl.num_programs(1) - 1)
    def _():
        _residual_norm_rows(x_ref, o_ref, g_ref, o_ref, o_ref.shape[0])


def _mix(ret, swa, proj, w_ret_o, w_swa_o, w_out, x2, g):
    T = x2.shape[0]
    tm, tn = min(MIX_TM, T), MIX_TN
    nj = D_MODEL // tn
    gw = MIX_GATE_W
    n_gate = tn // gw

    def gate_specs(col0):
        return [pl.BlockSpec((tm, gw), functools.partial(lambda i, j, c: (i, col0 // gw + j * n_gate + c), c=c))
                for c in range(n_gate)]

    return pl.pallas_call(
        _mix_kernel,
        grid=(T // tm, nj),
        in_specs=[
            pl.BlockSpec((tm, RET_V_W), lambda i, j: (i, 0)),
            pl.BlockSpec((tm, SWA_Q_W), lambda i, j: (i, 0)),
            *gate_specs(COL_GR),
            *gate_specs(COL_GS),
            pl.BlockSpec((RET_V_W, tn), lambda i, j: (0, j)),
            pl.BlockSpec((SWA_Q_W, tn), lambda i, j: (0, j)),
            pl.BlockSpec((tn, D_MODEL), lambda i, j: (j, 0)),
            pl.BlockSpec((tm, D_MODEL), lambda i, j: (i, 0)),
            pl.BlockSpec((1, D_MODEL), lambda i, j: (0, 0)),
        ],
        out_specs=pl.BlockSpec((tm, D_MODEL), lambda i, j: (i, 0)),
        out_shape=jax.ShapeDtypeStruct((T, D_MODEL), f32),
        compiler_params=_params(("parallel", "arbitrary")),
        name="mix",
    )(ret, swa, *([proj] * (2 * n_gate)), w_ret_o, w_swa_o, w_out, x2, g)


def _gelu_tanh(x):
    return x * (0.5 * (1.0 + jnp.tanh(0.7978845608028654 * (x + 0.044715 * (x * x * x)))))


def _ffn_kernel(x_ref, halo_ref, gpre_ref, wv_ref, wg_ref, cwv_ref, cwg_ref, cbv_ref, cbg_ref, wd_ref, gpost_ref,
                o_ref, h_scr, *u_scrs, tiles_per_seq):
    i = pl.program_id(0)
    j = pl.program_id(1)
    tm = x_ref.shape[0]
    tn = wv_ref.shape[1]
    H = FFN_HALO

    @pl.when(j == 0)
    def _():
        keep = jnp.where(i % tiles_per_seq == 0, 0.0, 1.0)
        h_scr[0:H, :] = (_rmsnorm(halo_ref[...], gpre_ref[...]) * keep).astype(bf16)
        _norm_rows_to(x_ref, gpre_ref, h_scr, H, tm)
        o_ref[...] = jnp.zeros_like(o_ref)

    def conv(u_scr, row0, nrows, u_cols, w_cols, cw_ref, cb_ref):
        y = cb_ref[:, w_cols]
        for kk in range(CONV_WIDTH):
            y = y + u_scr[pl.ds(H - (CONV_WIDTH - 1) + kk + row0, nrows), u_cols] * cw_ref[kk:kk + 1, w_cols]
        return y

    h = h_scr[...]
    hm = tm // FFN_ROW_SPLIT
    parts = [None] * FFN_ROW_SPLIT
    uv = slice(0, MXU_COLS)
    ug = slice(MXU_COLS, 2 * MXU_COLS)
    n_chunks = len(u_scrs)
    for c, u_scr in enumerate(u_scrs):
        cs = slice(c * MXU_COLS, (c + 1) * MXU_COLS)
        if c < n_chunks - 1:
            u_scr[:, uv] = jnp.dot(h, wv_ref[:, cs], preferred_element_type=f32)
            u_scr[:, ug] = jnp.dot(h, wg_ref[:, cs], preferred_element_type=f32)
        else:
            wv, wg = wv_ref[:, cs], wg_ref[:, cs]
            r0 = 0
            for m in range(FFN_ROW_SPLIT):
                r1 = H + (m + 1) * hm
                u_scr[r0:r1, uv] = jnp.dot(h_scr[r0:r1, :], wv, preferred_element_type=f32)
                u_scr[r0:r1, ug] = jnp.dot(h_scr[r0:r1, :], wg, preferred_element_type=f32)
                r0 = r1
        for m in range(FFN_ROW_SPLIT):
            val = conv(u_scr, m * hm, hm, uv, cs, cwv_ref, cbv_ref)
            gate = conv(u_scr, m * hm, hm, ug, cs, cwg_ref, cbg_ref)
            a = (_gelu_tanh(gate) * val).astype(bf16)
            d = jnp.dot(a, wd_ref[cs, :], preferred_element_type=f32)
            parts[m] = d if parts[m] is None else parts[m] + d
    for m in range(FFN_ROW_SPLIT):
        o_ref[m * hm:(m + 1) * hm, :] += parts[m]

    @pl.when(j == pl.num_programs(1) - 1)
    def _():
        _residual_norm_rows(x_ref, o_ref, gpost_ref, o_ref, tm)


def _ffn(x1, g_pre, w_up, conv_w, conv_b, w_down, g_post, seq):
    T = x1.shape[0]
    tm, tn = min(FFN_TM, seq), FFN_TN
    nj = D_FF // tn
    hb = tm // FFN_HALO
    kern = functools.partial(_ffn_kernel, tiles_per_seq=seq // tm)
    return pl.pallas_call(
        kern,
        grid=(T // tm, nj),
        in_specs=[
            pl.BlockSpec((tm, D_MODEL), lambda i, j: (i, 0)),
            pl.BlockSpec((FFN_HALO, D_MODEL), lambda i, j: (jnp.maximum(i * hb - 1, 0), 0)),
            pl.BlockSpec((1, D_MODEL), lambda i, j: (0, 0)),
            pl.BlockSpec((D_MODEL, tn), lambda i, j: (0, j)),
            pl.BlockSpec((D_MODEL, tn), lambda i, j: (0, nj + j)),
            pl.BlockSpec((CONV_WIDTH, tn), lambda i, j: (0, j)),
            pl.BlockSpec((CONV_WIDTH, tn), lambda i, j: (0, nj + j)),
            pl.BlockSpec((1, tn), lambda i, j: (0, j)),
            pl.BlockSpec((1, tn), lambda i, j: (0, nj + j)),
            pl.BlockSpec((tn, D_MODEL), lambda i, j: (j, 0)),
            pl.BlockSpec((1, D_MODEL), lambda i, j: (0, 0)),
        ],
        out_specs=pl.BlockSpec((tm, D_MODEL), lambda i, j: (i, 0)),
        out_shape=jax.ShapeDtypeStruct((T, D_MODEL), f32),
        scratch_shapes=[
            pltpu.VMEM((tm + FFN_HALO, D_MODEL), bf16),
        ] + [pltpu.VMEM((tm + FFN_HALO, 2 * MXU_COLS), f32)] * (tn // MXU_COLS),
        compiler_params=_params(("parallel", "arbitrary")),
        name="ffn",
    )(x1, x1, g_pre, w_up, w_up, conv_w, conv_w, conv_b, conv_b, w_down, g_post)


def _rotary_tables(seq):
    d = RET_DK
    inv = 1.0 / (ROPE_BASE ** (jnp.arange(0, d, 2, dtype=f32) / d))
    ang = jnp.arange(seq, dtype=jnp.int32).astype(f32)[:, None] * inv[None, :]
    cos, sin = jnp.cos(ang), jnp.sin(ang)
    tab = jnp.concatenate([cos, cos, -sin, sin], axis=-1)
    return jnp.stack([tab, tab * (d ** -0.5)])


def kernel(x, g_pre_mix, w_in, w_ret_o, w_swa_o, w_out, swa_sinks, g_post_mix, g_pre_ffn, w_up, conv_w, conv_b,
           w_down, g_post_ffn):
    B, S, D = x.shape
    depth = w_in.shape[0]
    rot = _rotary_tables(S)
    x2 = x.reshape(B * S, D)
    for l in range(depth):
        proj = _inproj(x2, g_pre_mix[l][None], w_in[l].astype(bf16), rot, S)
        ret = _retention(proj, B, S)
        swa = _swa(proj, swa_sinks[l].astype(f32), B, S)
        x2 = _mix(ret, swa, proj, w_ret_o[l].astype(bf16), w_swa_o[l].astype(bf16), w_out[l].astype(bf16), x2,
                  g_post_mix[l][None])
        x2 = _ffn(x2, g_pre_ffn[l][None], w_up[l].astype(bf16), conv_w[l], conv_b[l][None], w_down[l].astype(bf16),
                  g_post_ffn[l][None], S)
    return x2.reshape(B, S, D)
```

```python
import functools

import jax
import jax.numpy as jnp
from jax import lax
from jax.experimental import pallas as pl
from jax.experimental.pallas import tpu as pltpu

D_MODEL = 2048
RET_HEADS = 8
RET_DK = 128
RET_DV = 256
RET_CHUNK = 256
ROPE_BASE = 10000.0
SWA_Q_HEADS = 16
SWA_KV_HEADS = 4
SWA_HEAD_DIM = 64
SWA_BLOCK = 128
D_FF = 5632
CONV_WIDTH = 3
RMS_EPS = 1e-6

RET_QK_W = RET_HEADS * RET_DK
RET_V_W = RET_HEADS * RET_DV
SWA_Q_W = SWA_Q_HEADS * SWA_HEAD_DIM
SWA_KV_W = SWA_KV_HEADS * SWA_HEAD_DIM

COL_RQ = 0
COL_RK = COL_RQ + RET_QK_W
COL_RV = COL_RK + RET_QK_W
COL_RG = COL_RV + RET_V_W
COL_SQ = COL_RG + RET_V_W
COL_SK = COL_SQ + SWA_Q_W
COL_SV = COL_SK + SWA_KV_W
COL_GR = COL_SV + SWA_KV_W
COL_GS = COL_GR + D_MODEL
IN_WIDTH = COL_GS + D_MODEL

V7X_VMEM_LIMIT_BYTES = 60 * 1024 * 1024
BF16_SUBLANES = 16
MXU_COLS = 256
NORM_ROWS = 32
NORM_UNROLL = 4

INPROJ_TM = 1024
INPROJ_TN = 1024
MIX_TM = 512
MIX_TN = 1024
MIX_GATE_W = 512
FFN_TM = 1024
FFN_TN = 512
FFN_HALO = BF16_SUBLANES
FFN_ROW_SPLIT = 4
INPROJ_ROW_SPLIT = 4
RET_HEADS_PER_STEP = 2

f32 = jnp.float32
bf16 = jnp.bfloat16

def _params(semantics):
    return pltpu.CompilerParams(dimension_semantics=semantics, vmem_limit_bytes=V7X_VMEM_LIMIT_BYTES)


def _rmsnorm(x, g):
    ms = jnp.mean(x * x, axis=-1, keepdims=True)
    return x * lax.rsqrt(ms + RMS_EPS) * g


def _norm_rows_to(x_ref, g_ref, dst_ref, dst_off, rows):
    g = g_ref[...]
    group = NORM_ROWS * NORM_UNROLL

    def step(c, carry):
        base = pl.multiple_of(c * group, group)
        for k in range(NORM_UNROLL):
            r = base + k * NORM_ROWS
            dst_ref[pl.ds(dst_off + r, NORM_ROWS), :] = (
                _rmsnorm(x_ref[pl.ds(r, NORM_ROWS), :], g).astype(dst_ref.dtype))
        return carry

    lax.fori_loop(0, rows // group, step, 0)


def _residual_norm_rows(x_ref, y_ref, g_ref, o_ref, rows):
    g = g_ref[...]
    group = NORM_ROWS * NORM_UNROLL

    def step(c, carry):
        base = pl.multiple_of(c * group, group)
        rows_k = [pl.ds(base + k * NORM_ROWS, NORM_ROWS) for k in range(NORM_UNROLL)]
        scales = []
        for r in rows_k:
            y = y_ref[r, :]
            scales.append(lax.rsqrt(jnp.mean(y * y, axis=-1, keepdims=True) + RMS_EPS))
        for r, s in zip(rows_k, scales):
            o_ref[r, :] = x_ref[r, :] + y_ref[r, :] * s * g
        return carry

    lax.fori_loop(0, rows // group, step, 0)


def _inproj_tile_kinds():
    starts = ((COL_RQ, "rot"), (COL_RK, "rot"), (COL_RV, "cast"), (COL_RG, "silu"), (COL_SQ, "cast"),
              (COL_GR, "sig"))
    chunk_kinds = [[k for c0, k in starts if c0 <= col][-1] for col in range(0, IN_WIDTH, MXU_COLS)]
    per_tile = INPROJ_TN // MXU_COLS
    return [tuple(chunk_kinds[t:t + per_tile]) for t in range(0, len(chunk_kinds), per_tile)]


_INPROJ_TILE_KINDS = _inproj_tile_kinds()
_T_RK = COL_RK // INPROJ_TN
assert COL_RK % INPROJ_TN == 0 and COL_RV % INPROJ_TN == 0


def _inproj_kernel(x_ref, g_ref, w_ref, rot_ref, o_ref, h_scr):
    j = pl.program_id(1)
    tm = x_ref.shape[0]

    @pl.when(j == 0)
    def _():
        _norm_rows_to(x_ref, g_ref, h_scr, 0, tm)

    def rotary(acc, rows):
        cos = rot_ref[rows, 0:RET_DK]
        sin = rot_ref[rows, RET_DK:2 * RET_DK]
        heads = []
        for hh in range(acc.shape[1] // RET_DK):
            xh = acc[:, hh * RET_DK:(hh + 1) * RET_DK]
            heads.append(xh * cos + pltpu.roll(xh, RET_DK // 2, axis=1) * sin)
        return jnp.concatenate(heads, axis=1)

    epilogues = {
        "rot": rotary,
        "cast": lambda acc, rows: acc,
        "silu": lambda acc, rows: acc * jax.nn.sigmoid(acc),
        "sig": lambda acc, rows: jax.nn.sigmoid(acc),
    }

    def tile(kinds):
        hm = tm // INPROJ_ROW_SPLIT
        for c, kind in enumerate(kinds):
            cs = slice(c * MXU_COLS, (c + 1) * MXU_COLS)
            w = w_ref[:, cs]
            for m in range(INPROJ_ROW_SPLIT):
                rows = slice(m * hm, (m + 1) * hm)
                acc = jnp.dot(h_scr[rows, :], w, preferred_element_type=f32)
                o_ref[rows, cs] = epilogues[kind](acc, rows).astype(o_ref.dtype)

    for kinds in sorted(set(_INPROJ_TILE_KINDS)):
        tiles = [t for t, k in enumerate(_INPROJ_TILE_KINDS) if k == kinds]
        cond = functools.reduce(lambda a, b: a | b, [j == t for t in tiles])
        pl.when(cond)(functools.partial(tile, kinds))


def _inproj(x2, g, w_in, rot, seq):
    T = x2.shape[0]
    tm, tn = INPROJ_TM, INPROJ_TN
    tm = min(tm, seq)
    spt = seq // tm
    grid = (T // tm, len(_INPROJ_TILE_KINDS))
    return pl.pallas_call(
        _inproj_kernel,
        grid=grid,
        in_specs=[
            pl.BlockSpec((tm, D_MODEL), lambda i, j: (i, 0)),
            pl.BlockSpec((1, D_MODEL), lambda i, j: (0, 0)),
            pl.BlockSpec((D_MODEL, tn), lambda i, j: (0, j)),
            pl.BlockSpec((None, tm, 2 * RET_DK), lambda i, j: (jnp.where(j >= _T_RK, 1, 0), i % spt, 0)),
        ],
        out_specs=pl.BlockSpec((tm, tn), lambda i, j: (i, j)),
        out_shape=jax.ShapeDtypeStruct((T, IN_WIDTH), bf16),
        scratch_shapes=[pltpu.VMEM((tm, D_MODEL), bf16)],
        compiler_params=_params(("parallel", "arbitrary")),
        name="inproj",
    )(x2, g, w_in, rot)


def _retention_kernel(q_ref, k_ref, v_ref, gate_ref, dmat_ref, xi_ref, zeta_ref, cdec_ref, o_ref, kv_scr, st_scr):
    S = q_ref.shape[0]
    C = RET_CHUNK
    N = S // C
    heads = range(RET_HEADS_PER_STEP)
    qk = [slice(hh * RET_DK, (hh + 1) * RET_DK) for hh in heads]
    vv = [slice(hh * RET_DV, (hh + 1) * RET_DV) for hh in heads]
    q3 = [q_ref[:, qk[hh]].reshape(N, C, RET_DK) for hh in heads]
    k3 = [k_ref[:, qk[hh]].reshape(N, C, RET_DK) for hh in heads]
    v3 = [v_ref[:, vv[hh]].reshape(N, C, RET_DV) for hh in heads]

    for hh in heads:
        kz = (k3[hh].astype(f32) * zeta_ref[hh][None]).astype(bf16)
        kv_scr[hh] = jnp.einsum("ndk,nkv->ndv", jnp.swapaxes(kz, 1, 2), v3[hh], preferred_element_type=f32)

    cdec = [cdec_ref[hh] for hh in heads]

    def scan(n, states):
        for hh in heads:
            st_scr[hh, n] = states[hh].astype(bf16)
        return tuple(states[hh] * cdec[hh] + kv_scr[hh, n] for hh in heads)

    lax.fori_loop(0, N, scan, tuple(jnp.zeros((RET_DK, RET_DV), f32) for _ in heads))

    for hh in heads:
        sc = jnp.einsum("nqd,nkd->nqk", q3[hh], k3[hh], preferred_element_type=f32) * dmat_ref[hh][None]
        inner = jnp.einsum("nqk,nkv->nqv", sc.astype(bf16), v3[hh], preferred_element_type=f32)
        cross = jnp.einsum("nqd,ndv->nqv", q3[hh], st_scr[hh], preferred_element_type=f32) * xi_ref[hh][None]
        o = inner + cross
        o = o * lax.rsqrt(jnp.mean(o * o, axis=-1, keepdims=True) + RMS_EPS)
        o = gate_ref[:, vv[hh]].astype(f32).reshape(N, C, RET_DV) * o
        o_ref[:, vv[hh]] = o.reshape(S, RET_DV).astype(o_ref.dtype)


def _retention_tables():
    C = RET_CHUNK
    log_gamma = jnp.log(1.0 - 2.0 ** (-5.0 - jnp.arange(RET_HEADS, dtype=f32)))
    idx = jnp.arange(C, dtype=f32)
    rel = idx[:, None] - idx[None, :]
    dmat = jnp.where(rel[None] >= 0, jnp.exp(log_gamma[:, None, None] * jnp.maximum(rel, 0.0)[None]), 0.0)
    xi = jnp.exp(log_gamma[:, None] * (idx + 1.0))
    zeta = jnp.exp(log_gamma[:, None] * (C - 1.0 - idx))
    cdec = jnp.exp(log_gamma * C)
    xi_b = jnp.broadcast_to(xi[:, :, None], (RET_HEADS, C, RET_DV))
    zeta_b = jnp.broadcast_to(zeta[:, :, None], (RET_HEADS, C, RET_DK))
    cdec_b = jnp.broadcast_to(cdec[:, None, None], (RET_HEADS, 1, RET_DV))
    return dmat.astype(f32), xi_b.astype(f32), zeta_b.astype(f32), cdec_b.astype(f32)


def _retention(proj, batch, seq):
    T = proj.shape[0]
    C = RET_CHUNK
    N = seq // C
    dmat, xi_b, zeta_b, cdec_b = _retention_tables()
    hp = RET_HEADS_PER_STEP
    qk_w, v_w = hp * RET_DK, hp * RET_DV
    return pl.pallas_call(
        _retention_kernel,
        grid=(batch, RET_HEADS // hp),
        in_specs=[
            pl.BlockSpec((seq, qk_w), lambda b, h: (b, COL_RQ // qk_w + h)),
            pl.BlockSpec((seq, qk_w), lambda b, h: (b, COL_RK // qk_w + h)),
            pl.BlockSpec((seq, v_w), lambda b, h: (b, COL_RV // v_w + h)),
            pl.BlockSpec((seq, v_w), lambda b, h: (b, COL_RG // v_w + h)),
            pl.BlockSpec((hp, C, C), lambda b, h: (h, 0, 0)),
            pl.BlockSpec((hp, C, RET_DV), lambda b, h: (h, 0, 0)),
            pl.BlockSpec((hp, C, RET_DK), lambda b, h: (h, 0, 0)),
            pl.BlockSpec((hp, 1, RET_DV), lambda b, h: (h, 0, 0)),
        ],
        out_specs=pl.BlockSpec((seq, v_w), lambda b, h: (b, h)),
        out_shape=jax.ShapeDtypeStruct((T, RET_V_W), bf16),
        scratch_shapes=[pltpu.VMEM((hp, N, RET_DK, RET_DV), f32), pltpu.VMEM((hp, N, RET_DK, RET_DV), bf16)],
        compiler_params=_params(("parallel", "parallel")),
        name="retention",
    )(proj, proj, proj, proj, dmat, xi_b, zeta_b, cdec_b)


KV_PER_STEP = 2
Q_PER_KV = SWA_Q_HEADS // SWA_KV_HEADS
SWA_Q_STEP_W = KV_PER_STEP * Q_PER_KV * SWA_HEAD_DIM
LANES = 128
SWA_UNROLL = 8


def _swa_kernel(sink_ref, bias_ref, q_ref, k_ref, v_ref, o_ref, klo, khi, vlo, vhi):
    S = q_ref.shape[0]
    C = SWA_BLOCK
    gp = pl.program_id(1)
    half = SWA_HEAD_DIM

    lane_s = lax.broadcasted_iota(jnp.int32, (S, LANES), 1)
    low_s = lane_s < half

    def prep(src_ref, lo_scr, hi_scr, scale, with_ones):
        w = src_ref[...].astype(f32) * scale
        r = pltpu.roll(w, half, axis=1)
        zeros = jnp.zeros((C, lo_scr.shape[2]), bf16)
        for t in range(KV_PER_STEP):
            lo_scr[t, 0:C, :] = zeros
            hi_scr[t, 0:C, :] = zeros
        lo_scr[0, C:C + S, 0:LANES] = jnp.where(low_s, w, 0.0).astype(bf16)
        hi_scr[0, C:C + S, 0:LANES] = jnp.where(low_s, 0.0, r).astype(bf16)
        lo_scr[1, C:C + S, 0:LANES] = jnp.where(low_s, r, 0.0).astype(bf16)
        hi_scr[1, C:C + S, 0:LANES] = jnp.where(low_s, 0.0, w).astype(bf16)
        if with_ones:
            for t in range(KV_PER_STEP):
                lo_scr[t, C:C + S, LANES:2 * LANES] = jnp.where(low_s, 1.0, 0.0).astype(bf16)
                hi_scr[t, C:C + S, LANES:2 * LANES] = jnp.where(low_s, 0.0, 1.0).astype(bf16)

    prep(k_ref, klo, khi, SWA_HEAD_DIM ** -0.5, False)
    prep(v_ref, vlo, vhi, 1.0, True)

    rows = 2 * C
    win = 2 * C
    first_pair = lax.broadcasted_iota(jnp.int32, (rows, 1), 0) < C
    low_o = lax.broadcasted_iota(jnp.int32, (rows, LANES), 1) < half
    nt = (((1,), (1,)), ((), ()))

    def softmax_parts(s, sink, bias):
        s = s + bias
        m = jnp.maximum(jnp.max(s, axis=-1, keepdims=True), sink)
        return jnp.exp(s - m).astype(bf16), jnp.exp(sink - m)

    def block(n, carry):
        r0 = pl.multiple_of(n * C, C)
        bias = bias_ref[jnp.minimum(n, 1)]
        for t in range(KV_PER_STEP):
            c0 = t * Q_PER_KV * SWA_HEAD_DIM
            hbase = gp * (KV_PER_STEP * Q_PER_KV) + t * Q_PER_KV
            qs = jnp.concatenate([q_ref[pl.ds(r0, C), c0:c0 + LANES],
                                  q_ref[pl.ds(r0, C), c0 + LANES:c0 + 2 * LANES]], axis=0)
            s_e = lax.dot_general(qs, klo[t, pl.ds(r0, win), :], nt, preferred_element_type=f32)
            s_o = lax.dot_general(qs, khi[t, pl.ds(r0, win), :], nt, preferred_element_type=f32)
            sink_e = jnp.where(first_pair, sink_ref[hbase + 0], sink_ref[hbase + 2])
            sink_o = jnp.where(first_pair, sink_ref[hbase + 1], sink_ref[hbase + 3])
            p_e, z_e = softmax_parts(s_e, sink_e, bias)
            p_o, z_o = softmax_parts(s_o, sink_o, bias)
            pv = (jnp.dot(p_e, vlo[t, pl.ds(r0, win), :], preferred_element_type=f32)
                  + jnp.dot(p_o, vhi[t, pl.ds(r0, win), :], preferred_element_type=f32))
            den = pv[:, LANES:2 * LANES] + jnp.where(low_o, z_e, z_o)
            o = pv[:, 0:LANES] / den
            o_ref[pl.ds(r0, C), c0:c0 + LANES] = o[0:C].astype(o_ref.dtype)
            o_ref[pl.ds(r0, C), c0 + LANES:c0 + 2 * LANES] = o[C:2 * C].astype(o_ref.dtype)
        return carry

    lax.fori_loop(0, S // C, block, 0, unroll=SWA_UNROLL)


def _swa_bias():
    C = SWA_BLOCK
    row_i = jnp.arange(2 * C)[:, None] % C
    col_j = jnp.arange(2 * C)[None, :]
    band = (col_j > row_i) & (col_j <= row_i + C)
    first = band & (col_j >= C)
    return jnp.where(jnp.stack([first, band]), 0.0, -jnp.inf).astype(f32)


def _swa(proj, sinks, batch, seq):
    T = proj.shape[0]
    steps = SWA_KV_HEADS // KV_PER_STEP
    k_scr = pltpu.VMEM((KV_PER_STEP, seq + SWA_BLOCK, LANES), bf16)
    v_scr = pltpu.VMEM((KV_PER_STEP, seq + SWA_BLOCK, 2 * LANES), bf16)
    bias = _swa_bias()
    return pl.pallas_call(
        _swa_kernel,
        grid=(batch, steps),
        in_specs=[
            pl.BlockSpec(memory_space=pltpu.SMEM),
            pl.BlockSpec(bias.shape, lambda b, g: (0, 0, 0)),
            pl.BlockSpec((seq, SWA_Q_STEP_W), lambda b, g: (b, COL_SQ // SWA_Q_STEP_W + g)),
            pl.BlockSpec((seq, LANES), lambda b, g: (b, COL_SK // LANES + g)),
            pl.BlockSpec((seq, LANES), lambda b, g: (b, COL_SV // LANES + g)),
        ],
        out_specs=pl.BlockSpec((seq, SWA_Q_STEP_W), lambda b, g: (b, g)),
        out_shape=jax.ShapeDtypeStruct((T, SWA_Q_W), bf16),
        scratch_shapes=[k_scr, k_scr, v_scr, v_scr],
        compiler_params=_params(("parallel", "parallel")),
        name="swa",
    )(sinks, bias, proj, proj, proj)


def _mix_kernel(ret_ref, swa_ref, *refs):
    n_gate = MIX_TN // MIX_GATE_W
    gr_refs, gs_refs = refs[:n_gate], refs[n_gate:2 * n_gate]
    wr_ref, ws_ref, wo_ref, x_ref, g_ref, o_ref = refs[2 * n_gate:]
    j = pl.program_id(1)

    def step(first):
        part = None
        for c in range(n_gate):
            cs = slice(c * MIX_GATE_W, (c + 1) * MIX_GATE_W)
            ret_out = jnp.dot(ret_ref[...], wr_ref[:, cs], preferred_element_type=f32)
            swa_out = jnp.dot(swa_ref[...], ws_ref[:, cs], preferred_element_type=f32)
            mixed = gr_refs[c][...].astype(f32) * ret_out + gs_refs[c][...].astype(f32) * swa_out
            d = jnp.dot(mixed.astype(bf16), wo_ref[cs, :], preferred_element_type=f32)
            part = d if part is None else part + d
        if first:
            o_ref[...] = part
        else:
            o_ref[...] += part

    pl.when(j == 0)(functools.partial(step, True))
    pl.when(j > 0)(functools.partial(step, False))

    @pl.when(j == pl.num_programs(1) - 1)
    def _():
        _residual_norm_rows(x_ref, o_ref, g_ref, o_ref, o_ref.shape[0])


def _mix(ret, swa, proj, w_ret_o, w_swa_o, w_out, x2, g):
    T = x2.shape[0]
    tm, tn = min(MIX_TM, T), MIX_TN
    nj = D_MODEL // tn
    gw = MIX_GATE_W
    n_gate = tn // gw

    def gate_specs(col0):
        return [pl.BlockSpec((tm, gw), functools.partial(lambda i, j, c: (i, col0 // gw + j * n_gate + c), c=c))
                for c in range(n_gate)]

    return pl.pallas_call(
        _mix_kernel,
        grid=(T // tm, nj),
        in_specs=[
            pl.BlockSpec((tm, RET_V_W), lambda i, j: (i, 0)),
            pl.BlockSpec((tm, SWA_Q_W), lambda i, j: (i, 0)),
            *gate_specs(COL_GR),
            *gate_specs(COL_GS),
            pl.BlockSpec((RET_V_W, tn), lambda i, j: (0, j)),
            pl.BlockSpec((SWA_Q_W, tn), lambda i, j: (0, j)),
            pl.BlockSpec((tn, D_MODEL), lambda i, j: (j, 0)),
            pl.BlockSpec((tm, D_MODEL), lambda i, j: (i, 0)),
            pl.BlockSpec((1, D_MODEL), lambda i, j: (0, 0)),
        ],
        out_specs=pl.BlockSpec((tm, D_MODEL), lambda i, j: (i, 0)),
        out_shape=jax.ShapeDtypeStruct((T, D_MODEL), f32),
        compiler_params=_params(("parallel", "arbitrary")),
        name="mix",
    )(ret, swa, *([proj] * (2 * n_gate)), w_ret_o, w_swa_o, w_out, x2, g)


def _gelu_tanh(x):
    return x * (0.5 * (1.0 + jnp.tanh(0.7978845608028654 * (x + 0.044715 * (x * x * x)))))


def _ffn_kernel(x_ref, halo_ref, gpre_ref, wv_ref, wg_ref, cwv_ref, cwg_ref, cbv_ref, cbg_ref, wd_ref, gpost_ref,
                o_ref, h_scr, *u_scrs, tiles_per_seq):
    i = pl.program_id(0)
    j = pl.program_id(1)
    tm = x_ref.shape[0]
    tn = wv_ref.shape[1]
    H = FFN_HALO

    @pl.when(j == 0)
    def _():
        keep = jnp.where(i % tiles_per_seq == 0, 0.0, 1.0)
        h_scr[0:H, :] = (_rmsnorm(halo_ref[...], gpre_ref[...]) * keep).astype(bf16)
        _norm_rows_to(x_ref, gpre_ref, h_scr, H, tm)
        o_ref[...] = jnp.zeros_like(o_ref)

    def conv(u_scr, row0, nrows, u_cols, w_cols, cw_ref, cb_ref):
        y = cb_ref[:, w_cols]
        for kk in range(CONV_WIDTH):
            y = y + u_scr[pl.ds(H - (CONV_WIDTH - 1) + kk + row0, nrows), u_cols] * cw_ref[kk:kk + 1, w_cols]
        return y

    h = h_scr[...]
    hm = tm // FFN_ROW_SPLIT
    parts = [None] * FFN_ROW_SPLIT
    uv = slice(0, MXU_COLS)
    ug = slice(MXU_COLS, 2 * MXU_COLS)
    n_chunks = len(u_scrs)
    for c, u_scr in enumerate(u_scrs):
        cs = slice(c * MXU_COLS, (c + 1) * MXU_COLS)
        if c < n_chunks - 1:
            u_scr[:, uv] = jnp.dot(h, wv_ref[:, cs], preferred_element_type=f32)
            u_scr[:, ug] = jnp.dot(h, wg_ref[:, cs], preferred_element_type=f32)
        else:
            wv, wg = wv_ref[:, cs], wg_ref[:, cs]
            r0 = 0
            for m in range(FFN_ROW_SPLIT):
                r1 = H + (m + 1) * hm
                u_scr[r0:r1, uv] = jnp.dot(h_scr[r0:r1, :], wv, preferred_element_type=f32)
                u_scr[r0:r1, ug] = jnp.dot(h_scr[r0:r1, :], wg, preferred_element_type=f32)
                r0 = r1
        for m in range(FFN_ROW_SPLIT):
            val = conv(u_scr, m * hm, hm, uv, cs, cwv_ref, cbv_ref)
            gate = conv(u_scr, m * hm, hm, ug, cs, cwg_ref, cbg_ref)
            a = (_gelu_tanh(gate) * val).astype(bf16)
            d = jnp.dot(a, wd_ref[cs, :], preferred_element_type=f32)
            parts[m] = d if parts[m] is None else parts[m] + d
    for m in range(FFN_ROW_SPLIT):
        o_ref[m * hm:(m + 1) * hm, :] += parts[m]

    @pl.when(j == pl.num_programs(1) - 1)
    def _():
        _residual_norm_rows(x_ref, o_ref, gpost_ref, o_ref, tm)


def _ffn(x1, g_pre, w_up, conv_w, conv_b, w_down, g_post, seq):
    T = x1.shape[0]
    tm, tn = min(FFN_TM, seq), FFN_TN
    nj = D_FF // tn
    hb = tm // FFN_HALO
    kern = functools.partial(_ffn_kernel, tiles_per_seq=seq // tm)
    return pl.pallas_call(
        kern,
        grid=(T // tm, nj),
        in_specs=[
            pl.BlockSpec((tm, D_MODEL), lambda i, j: (i, 0)),
            pl.BlockSpec((FFN_HALO, D_MODEL), lambda i, j: (jnp.maximum(i * hb - 1, 0), 0)),
            pl.BlockSpec((1, D_MODEL), lambda i, j: (0, 0)),
            pl.BlockSpec((D_MODEL, tn), lambda i, j: (0, j)),
            pl.BlockSpec((D_MODEL, tn), lambda i, j: (0, nj + j)),
            pl.BlockSpec((CONV_WIDTH, tn), lambda i, j: (0, j)),
            pl.BlockSpec((CONV_WIDTH, tn), lambda i, j: (0, nj + j)),
            pl.BlockSpec((1, tn), lambda i, j: (0, j)),
            pl.BlockSpec((1, tn), lambda i, j: (0, nj + j)),
            pl.BlockSpec((tn, D_MODEL), lambda i, j: (j, 0)),
            pl.BlockSpec((1, D_MODEL), lambda i, j: (0, 0)),
        ],
        out_specs=pl.BlockSpec((tm, D_MODEL), lambda i, j: (i, 0)),
        out_shape=jax.ShapeDtypeStruct((T, D_MODEL), f32),
        scratch_shapes=[
            pltpu.VMEM((tm + FFN_HALO, D_MODEL), bf16),
        ] + [pltpu.VMEM((tm + FFN_HALO, 2 * MXU_COLS), f32)] * (tn // MXU_COLS),
        compiler_params=_params(("parallel", "arbitrary")),
        name="ffn",
    )(x1, x1, g_pre, w_up, w_up, conv_w, conv_w, conv_b, conv_b, w_down, g_post)


def _rotary_tables(seq):
    d = RET_DK
    inv = 1.0 / (ROPE_BASE ** (jnp.arange(0, d, 2, dtype=f32) / d))
    ang = jnp.arange(seq, dtype=jnp.int32).astype(f32)[:, None] * inv[None, :]
    cos, sin = jnp.cos(ang), jnp.sin(ang)
    tab = jnp.concatenate([cos, cos, -sin, sin], axis=-1)
    return jnp.stack([tab, tab * (d ** -0.5)])


def kernel(x, g_pre_mix, w_in, w_ret_o, w_swa_o, w_out, swa_sinks, g_post_mix, g_pre_ffn, w_up, conv_w, conv_b,
           w_down, g_post_ffn):
    B, S, D = x.shape
    depth = w_in.shape[0]
    rot = _rotary_tables(S)
    x2 = x.reshape(B * S, D)
    for l in range(depth):
        proj = _inproj(x2, g_pre_mix[l][None], w_in[l].astype(bf16), rot, S)
        ret = _retention(proj, B, S)
        swa = _swa(proj, swa_sinks[l].astype(f32), B, S)
        x2 = _mix(ret, swa, proj, w_ret_o[l].astype(bf16), w_swa_o[l].astype(bf16), w_out[l].astype(bf16), x2,
                  g_post_mix[l][None])
        x2 = _ffn(x2, g_pre_ffn[l][None], w_up[l].astype(bf16), conv_w[l], conv_b[l][None], w_down[l].astype(bf16),
                  g_post_ffn[l][None], S)
    return x2.reshape(B, S, D)
```

```python
import functools

import jax
import jax.numpy as jnp
from jax import lax
from jax.experimental import pallas as pl
from jax.experimental.pallas import tpu as pltpu

D_MODEL = 2048
RET_HEADS = 8
RET_DK = 128
RET_DV = 256
RET_CHUNK = 256
ROPE_BASE = 10000.0
SWA_Q_HEADS = 16
SWA_KV_HEADS = 4
SWA_HEAD_DIM = 64
SWA_BLOCK = 128
D_FF = 5632
CONV_WIDTH = 3
RMS_EPS = 1e-6

RET_QK_W = RET_HEADS * RET_DK
RET_V_W = RET_HEADS * RET_DV
SWA_Q_W = SWA_Q_HEADS * SWA_HEAD_DIM
SWA_KV_W = SWA_KV_HEADS * SWA_HEAD_DIM

COL_RQ = 0
COL_RK = COL_RQ + RET_QK_W
COL_RV = COL_RK + RET_QK_W
COL_RG = COL_RV + RET_V_W
COL_SQ = COL_RG + RET_V_W
COL_SK = COL_SQ + SWA_Q_W
COL_SV = COL_SK + SWA_KV_W
COL_GR = COL_SV + SWA_KV_W
COL_GS = COL_GR + D_MODEL
IN_WIDTH = COL_GS + D_MODEL

V7X_VMEM_LIMIT_BYTES = 60 * 1024 * 1024
LANES = 128
BF16_SUBLANES = 16
MXU_COLS = 256

NORM_ROWS = 32
NORM_UNROLL = 4
INPROJ_TM = 1024
INPROJ_TN = 1024
INPROJ_ROW_SPLIT = 4
MIX_TM = 512
MIX_TN = 1024
MIX_GATE_W = 512
FFN_TM = 1024
FFN_TN = 512
FFN_HALO = BF16_SUBLANES
FFN_ROW_SPLIT = 4
RET_HEADS_PER_STEP = 2
SWA_KV_PER_STEP = 2
SWA_UNROLL = 8

f32 = jnp.float32
bf16 = jnp.bfloat16


def _params(semantics):
    return pltpu.CompilerParams(dimension_semantics=semantics, vmem_limit_bytes=V7X_VMEM_LIMIT_BYTES)


def _rmsnorm(x, g):
    ms = jnp.mean(x * x, axis=-1, keepdims=True)
    return x * lax.rsqrt(ms + RMS_EPS) * g


def _norm_rows_to(x_ref, g_ref, dst_ref, dst_off, rows):
    g = g_ref[...]
    group = NORM_ROWS * NORM_UNROLL

    def step(c, carry):
        base = pl.multiple_of(c * group, group)
        for k in range(NORM_UNROLL):
            r = base + k * NORM_ROWS
            dst_ref[pl.ds(dst_off + r, NORM_ROWS), :] = (
                _rmsnorm(x_ref[pl.ds(r, NORM_ROWS), :], g).astype(dst_ref.dtype))
        return carry

    lax.fori_loop(0, rows // group, step, 0)


def _residual_norm_rows(x_ref, y_ref, g_ref, o_ref, rows):
    g = g_ref[...]
    group = NORM_ROWS * NORM_UNROLL

    def step(c, carry):
        base = pl.multiple_of(c * group, group)
        rows_k = [pl.ds(base + k * NORM_ROWS, NORM_ROWS) for k in range(NORM_UNROLL)]
        scales = []
        for r in rows_k:
            y = y_ref[r, :]
            scales.append(lax.rsqrt(jnp.mean(y * y, axis=-1, keepdims=True) + RMS_EPS))
        for r, s in zip(rows_k, scales):
            o_ref[r, :] = x_ref[r, :] + y_ref[r, :] * s * g
        return carry

    lax.fori_loop(0, rows // group, step, 0)


def _inproj_tile_kinds():
    starts = ((COL_RQ, "rot"), (COL_RK, "rot"), (COL_RV, "cast"), (COL_RG, "silu"), (COL_SQ, "cast"),
              (COL_GR, "sig"))
    chunk_kinds = [[k for c0, k in starts if c0 <= col][-1] for col in range(0, IN_WIDTH, MXU_COLS)]
    per_tile = INPROJ_TN // MXU_COLS
    return [tuple(chunk_kinds[t:t + per_tile]) for t in range(0, len(chunk_kinds), per_tile)]


_INPROJ_TILE_KINDS = _inproj_tile_kinds()
_T_RK = COL_RK // INPROJ_TN
assert COL_RK % INPROJ_TN == 0 and COL_RV % INPROJ_TN == 0


def _inproj_kernel(x_ref, g_ref, w_ref, rot_ref, o_ref, h_scr):
    j = pl.program_id(1)
    tm = x_ref.shape[0]

    @pl.when(j == 0)
    def _():
        _norm_rows_to(x_ref, g_ref, h_scr, 0, tm)

    def rotary(acc, rows):
        cos = rot_ref[rows, 0:RET_DK]
        sin = rot_ref[rows, RET_DK:2 * RET_DK]
        heads = []
        for hh in range(acc.shape[1] // RET_DK):
            xh = acc[:, hh * RET_DK:(hh + 1) * RET_DK]
            heads.append(xh * cos + pltpu.roll(xh, RET_DK // 2, axis=1) * sin)
        return jnp.concatenate(heads, axis=1)

    epilogues = {
        "rot": rotary,
        "cast": lambda acc, rows: acc,
        "silu": lambda acc, rows: acc * jax.nn.sigmoid(acc),
        "sig": lambda acc, rows: jax.nn.sigmoid(acc),
    }

    def tile(kinds):
        hm = tm // INPROJ_ROW_SPLIT
        for c, kind in enumerate(kinds):
            cs = slice(c * MXU_COLS, (c + 1) * MXU_COLS)
            w = w_ref[:, cs]
            for m in range(INPROJ_ROW_SPLIT):
                rows = slice(m * hm, (m + 1) * hm)
                acc = jnp.dot(h_scr[rows, :], w, preferred_element_type=f32)
                o_ref[rows, cs] = epilogues[kind](acc, rows).astype(o_ref.dtype)

    for kinds in sorted(set(_INPROJ_TILE_KINDS)):
        tiles = [t for t, k in enumerate(_INPROJ_TILE_KINDS) if k == kinds]
        cond = functools.reduce(lambda a, b: a | b, [j == t for t in tiles])
        pl.when(cond)(functools.partial(tile, kinds))


def _inproj(x2, g, w_in, rot, seq):
    T = x2.shape[0]
    tm, tn = INPROJ_TM, INPROJ_TN
    tm = min(tm, seq)
    spt = seq // tm
    grid = (T // tm, len(_INPROJ_TILE_KINDS))
    return pl.pallas_call(
        _inproj_kernel,
        grid=grid,
        in_specs=[
            pl.BlockSpec((tm, D_MODEL), lambda i, j: (i, 0)),
            pl.BlockSpec((1, D_MODEL), lambda i, j: (0, 0)),
            pl.BlockSpec((D_MODEL, tn), lambda i, j: (0, j)),
            pl.BlockSpec((None, tm, 2 * RET_DK), lambda i, j: (jnp.where(j >= _T_RK, 1, 0), i % spt, 0)),
        ],
        out_specs=pl.BlockSpec((tm, tn), lambda i, j: (i, j)),
        out_shape=jax.ShapeDtypeStruct((T, IN_WIDTH), bf16),
        scratch_shapes=[pltpu.VMEM((tm, D_MODEL), bf16)],
        compiler_params=_params(("parallel", "arbitrary")),
        name="inproj",
    )(x2, g, w_in, rot)


def _retention_kernel(q_ref, k_ref, v_ref, gate_ref, dmat_ref, xi_ref, zeta_ref, cdec_ref, o_ref, kv_scr, st_scr):
    S = q_ref.shape[0]
    C = RET_CHUNK
    N = S // C
    heads = range(RET_HEADS_PER_STEP)
    qk = [slice(hh * RET_DK, (hh + 1) * RET_DK) for hh in heads]
    vv = [slice(hh * RET_DV, (hh + 1) * RET_DV) for hh in heads]
    q3 = [q_ref[:, qk[hh]].reshape(N, C, RET_DK) for hh in heads]
    k3 = [k_ref[:, qk[hh]].reshape(N, C, RET_DK) for hh in heads]
    v3 = [v_ref[:, vv[hh]].reshape(N, C, RET_DV) for hh in heads]

    for hh in heads:
        kz = (k3[hh].astype(f32) * zeta_ref[hh][None]).astype(bf16)
        kv_scr[hh] = jnp.einsum("ndk,nkv->ndv", jnp.swapaxes(kz, 1, 2), v3[hh], preferred_element_type=f32)

    cdec = [cdec_ref[hh] for hh in heads]

    def scan(n, states):
        for hh in heads:
            st_scr[hh, n] = states[hh].astype(bf16)
        return tuple(states[hh] * cdec[hh] + kv_scr[hh, n] for hh in heads)

    lax.fori_loop(0, N, scan, tuple(jnp.zeros((RET_DK, RET_DV), f32) for _ in heads))

    for hh in heads:
        sc = jnp.einsum("nqd,nkd->nqk", q3[hh], k3[hh], preferred_element_type=f32) * dmat_ref[hh][None]
        inner = jnp.einsum("nqk,nkv->nqv", sc.astype(bf16), v3[hh], preferred_element_type=f32)
        cross = jnp.einsum("nqd,ndv->nqv", q3[hh], st_scr[hh], preferred_element_type=f32) * xi_ref[hh][None]
        o = inner + cross
        o = o * lax.rsqrt(jnp.mean(o * o, axis=-1, keepdims=True) + RMS_EPS)
        o = gate_ref[:, vv[hh]].astype(f32).reshape(N, C, RET_DV) * o
        o_ref[:, vv[hh]] = o.reshape(S, RET_DV).astype(o_ref.dtype)


def _retention_tables():
    C = RET_CHUNK
    log_gamma = jnp.log(1.0 - 2.0 ** (-5.0 - jnp.arange(RET_HEADS, dtype=f32)))
    idx = jnp.arange(C, dtype=f32)
    rel = idx[:, None] - idx[None, :]
    dmat = jnp.where(rel[None] >= 0, jnp.exp(log_gamma[:, None, None] * jnp.maximum(rel, 0.0)[None]), 0.0)
    xi = jnp.exp(log_gamma[:, None] * (idx + 1.0))
    zeta = jnp.exp(log_gamma[:, None] * (C - 1.0 - idx))
    cdec = jnp.exp(log_gamma * C)
    xi_b = jnp.broadcast_to(xi[:, :, None], (RET_HEADS, C, RET_DV))
    zeta_b = jnp.broadcast_to(zeta[:, :, None], (RET_HEADS, C, RET_DK))
    cdec_b = jnp.broadcast_to(cdec[:, None, None], (RET_HEADS, 1, RET_DV))
    return dmat.astype(f32), xi_b.astype(f32), zeta_b.astype(f32), cdec_b.astype(f32)


def _retention(proj, batch, seq):
    T = proj.shape[0]
    C = RET_CHUNK
    N = seq // C
    dmat, xi_b, zeta_b, cdec_b = _retention_tables()
    hp = RET_HEADS_PER_STEP
    qk_w, v_w = hp * RET_DK, hp * RET_DV
    return pl.pallas_call(
        _retention_kernel,
        grid=(batch, RET_HEADS // hp),
        in_specs=[
            pl.BlockSpec((seq, qk_w), lambda b, h: (b, COL_RQ // qk_w + h)),
            pl.BlockSpec((seq, qk_w), lambda b, h: (b, COL_RK // qk_w + h)),
            pl.BlockSpec((seq, v_w), lambda b, h: (b, COL_RV // v_w + h)),
            pl.BlockSpec((seq, v_w), lambda b, h: (b, COL_RG // v_w + h)),
            pl.BlockSpec((hp, C, C), lambda b, h: (h, 0, 0)),
            pl.BlockSpec((hp, C, RET_DV), lambda b, h: (h, 0, 0)),
            pl.BlockSpec((hp, C, RET_DK), lambda b, h: (h, 0, 0)),
            pl.BlockSpec((hp, 1, RET_DV), lambda b, h: (h, 0, 0)),
        ],
        out_specs=pl.BlockSpec((seq, v_w), lambda b, h: (b, h)),
        out_shape=jax.ShapeDtypeStruct((T, RET_V_W), bf16),
        scratch_shapes=[pltpu.VMEM((hp, N, RET_DK, RET_DV), f32), pltpu.VMEM((hp, N, RET_DK, RET_DV), bf16)],
        compiler_params=_params(("parallel", "parallel")),
        name="retention",
    )(proj, proj, proj, proj, dmat, xi_b, zeta_b, cdec_b)


KV_PER_STEP = SWA_KV_PER_STEP
Q_PER_KV = SWA_Q_HEADS // SWA_KV_HEADS
SWA_Q_STEP_W = KV_PER_STEP * Q_PER_KV * SWA_HEAD_DIM
assert KV_PER_STEP * SWA_HEAD_DIM == LANES and Q_PER_KV * SWA_HEAD_DIM == 2 * LANES


def _swa_kernel(sink_ref, bias_ref, q_ref, k_ref, v_ref, o_ref, klo, khi, vlo, vhi):
    S = q_ref.shape[0]
    C = SWA_BLOCK
    gp = pl.program_id(1)
    half = SWA_HEAD_DIM

    lane_s = lax.broadcasted_iota(jnp.int32, (S, LANES), 1)
    low_s = lane_s < half

    def prep(src_ref, lo_scr, hi_scr, scale, with_ones):
        w = src_ref[...].astype(f32) * scale
        r = pltpu.roll(w, half, axis=1)
        zeros = jnp.zeros((C, lo_scr.shape[2]), bf16)
        for t in range(KV_PER_STEP):
            lo_scr[t, 0:C, :] = zeros
            hi_scr[t, 0:C, :] = zeros
        lo_scr[0, C:C + S, 0:LANES] = jnp.where(low_s, w, 0.0).astype(bf16)
        hi_scr[0, C:C + S, 0:LANES] = jnp.where(low_s, 0.0, r).astype(bf16)
        lo_scr[1, C:C + S, 0:LANES] = jnp.where(low_s, r, 0.0).astype(bf16)
        hi_scr[1, C:C + S, 0:LANES] = jnp.where(low_s, 0.0, w).astype(bf16)
        if with_ones:
            for t in range(KV_PER_STEP):
                lo_scr[t, C:C + S, LANES:2 * LANES] = jnp.where(low_s, 1.0, 0.0).astype(bf16)
                hi_scr[t, C:C + S, LANES:2 * LANES] = jnp.where(low_s, 0.0, 1.0).astype(bf16)

    prep(k_ref, klo, khi, SWA_HEAD_DIM ** -0.5, False)
    prep(v_ref, vlo, vhi, 1.0, True)

    rows = 2 * C
    win = 2 * C
    first_pair = lax.broadcasted_iota(jnp.int32, (rows, 1), 0) < C
    low_o = lax.broadcasted_iota(jnp.int32, (rows, LANES), 1) < half
    nt = (((1,), (1,)), ((), ()))

    def softmax_parts(s, sink, bias):
        s = s + bias
        m = jnp.maximum(jnp.max(s, axis=-1, keepdims=True), sink)
        return jnp.exp(s - m).astype(bf16), jnp.exp(sink - m)

    def block(n, carry):
        r0 = pl.multiple_of(n * C, C)
        bias = bias_ref[jnp.minimum(n, 1)]
        for t in range(KV_PER_STEP):
            c0 = t * Q_PER_KV * SWA_HEAD_DIM
            hbase = gp * (KV_PER_STEP * Q_PER_KV) + t * Q_PER_KV
            qs = jnp.concatenate([q_ref[pl.ds(r0, C), c0:c0 + LANES],
                                  q_ref[pl.ds(r0, C), c0 + LANES:c0 + 2 * LANES]], axis=0)
            s_e = lax.dot_general(qs, klo[t, pl.ds(r0, win), :], nt, preferred_element_type=f32)
            s_o = lax.dot_general(qs, khi[t, pl.ds(r0, win), :], nt, preferred_element_type=f32)
            sink_e = jnp.where(first_pair, sink_ref[hbase + 0], sink_ref[hbase + 2])
            sink_o = jnp.where(first_pair, sink_ref[hbase + 1], sink_ref[hbase + 3])
            p_e, z_e = softmax_parts(s_e, sink_e, bias)
            p_o, z_o = softmax_parts(s_o, sink_o, bias)
            pv = (jnp.dot(p_e, vlo[t, pl.ds(r0, win), :], preferred_element_type=f32)
                  + jnp.dot(p_o, vhi[t, pl.ds(r0, win), :], preferred_element_type=f32))
            den = pv[:, LANES:2 * LANES] + jnp.where(low_o, z_e, z_o)
            o = pv[:, 0:LANES] / den
            o_ref[pl.ds(r0, C), c0:c0 + LANES] = o[0:C].astype(o_ref.dtype)
            o_ref[pl.ds(r0, C), c0 + LANES:c0 + 2 * LANES] = o[C:2 * C].astype(o_ref.dtype)
        return carry

    lax.fori_loop(0, S // C, block, 0, unroll=SWA_UNROLL)


def _swa_bias():
    C = SWA_BLOCK
    row_i = jnp.arange(2 * C)[:, None] % C
    col_j = jnp.arange(2 * C)[None, :]
    band = (col_j > row_i) & (col_j <= row_i + C)
    first = band & (col_j >= C)
    return jnp.where(jnp.stack([first, band]), 0.0, -jnp.inf).astype(f32)


def _swa(proj, sinks, batch, seq):
    T = proj.shape[0]
    steps = SWA_KV_HEADS // KV_PER_STEP
    k_scr = pltpu.VMEM((KV_PER_STEP, seq + SWA_BLOCK, LANES), bf16)
    v_scr = pltpu.VMEM((KV_PER_STEP, seq + SWA_BLOCK, 2 * LANES), bf16)
    bias = _swa_bias()
    return pl.pallas_call(
        _swa_kernel,
        grid=(batch, steps),
        in_specs=[
            pl.BlockSpec(memory_space=pltpu.SMEM),
            pl.BlockSpec(bias.shape, lambda b, g: (0, 0, 0)),
            pl.BlockSpec((seq, SWA_Q_STEP_W), lambda b, g: (b, COL_SQ // SWA_Q_STEP_W + g)),
            pl.BlockSpec((seq, LANES), lambda b, g: (b, COL_SK // LANES + g)),
            pl.BlockSpec((seq, LANES), lambda b, g: (b, COL_SV // LANES + g)),
        ],
        out_specs=pl.BlockSpec((seq, SWA_Q_STEP_W), lambda b, g: (b, g)),
        out_shape=jax.ShapeDtypeStruct((T, SWA_Q_W), bf16),
        scratch_shapes=[k_scr, k_scr, v_scr, v_scr],
        compiler_params=_params(("parallel", "parallel")),
        name="swa",
    )(sinks, bias, proj, proj, proj)


def _mix_kernel(ret_ref, swa_ref, *refs):
    n_gate = MIX_TN // MIX_GATE_W
    gr_refs, gs_refs = refs[:n_gate], refs[n_gate:2 * n_gate]
    wr_ref, ws_ref, wo_ref, x_ref, g_ref, o_ref = refs[2 * n_gate:]
    j = pl.program_id(1)

    @pl.when(j == 0)
    def _():
        o_ref[...] = jnp.zeros_like(o_ref)

    part = None
    for c in range(n_gate):
        cs = slice(c * MIX_GATE_W, (c + 1) * MIX_GATE_W)
        ret_out = jnp.dot(ret_ref[...], wr_ref[:, cs], preferred_element_type=f32)
        swa_out = jnp.dot(swa_ref[...], ws_ref[:, cs], preferred_element_type=f32)
        mixed = gr_refs[c][...].astype(f32) * ret_out + gs_refs[c][...].astype(f32) * swa_out
        d = jnp.dot(mixed.astype(bf16), wo_ref[cs, :], preferred_element_type=f32)
        part = d if part is None else part + d
    o_ref[...] += part

    @pl.when(j == pl.num_programs(1) - 1)
    def _():
        _residual_norm_rows(x_ref, o_ref, g_ref, o_ref, o_ref.shape[0])


def _mix(ret, swa, proj, w_ret_o, w_swa_o, w_out, x2, g):
    T = x2.shape[0]
    tm, tn = min(MIX_TM, T), MIX_TN
    nj = D_MODEL // tn
    gw = MIX_GATE_W
    n_gate = tn // gw

    def gate_specs(col0):
        return [pl.BlockSpec((tm, gw), functools.partial(lambda i, j, c: (i, col0 // gw + j * n_gate + c), c=c))
                for c in range(n_gate)]

    return pl.pallas_call(
        _mix_kernel,
        grid=(T // tm, nj),
        in_specs=[
            pl.BlockSpec((tm, RET_V_W), lambda i, j: (i, 0)),
            pl.BlockSpec((tm, SWA_Q_W), lambda i, j: (i, 0)),
            *gate_specs(COL_GR),
            *gate_specs(COL_GS),
            pl.BlockSpec((RET_V_W, tn), lambda i, j: (0, j)),
            pl.BlockSpec((SWA_Q_W, tn), lambda i, j: (0, j)),
            pl.BlockSpec((tn, D_MODEL), lambda i, j: (j, 0)),
            pl.BlockSpec((tm, D_MODEL), lambda i, j: (i, 0)),
            pl.BlockSpec((1, D_MODEL), lambda i, j: (0, 0)),
        ],
        out_specs=pl.BlockSpec((tm, D_MODEL), lambda i, j: (i, 0)),
        out_shape=jax.ShapeDtypeStruct((T, D_MODEL), f32),
        compiler_params=_params(("parallel", "arbitrary")),
        name="mix",
    )(ret, swa, *([proj] * (2 * n_gate)), w_ret_o, w_swa_o, w_out, x2, g)


def _gelu_tanh(x):
    return x * (0.5 * (1.0 + jnp.tanh(0.7978845608028654 * (x + 0.044715 * (x * x * x)))))


def _ffn_kernel(x_ref, halo_ref, gpre_ref, wv_ref, wg_ref, cwv_ref, cwg_ref, cbv_ref, cbg_ref, wd_ref, gpost_ref,
                o_ref, h_scr, *u_scrs, tiles_per_seq):
    i = pl.program_id(0)
    j = pl.program_id(1)
    tm = x_ref.shape[0]
    tn = wv_ref.shape[1]
    H = FFN_HALO

    @pl.when(j == 0)
    def _():
        halo = jnp.where(i % tiles_per_seq == 0, 0.0, _rmsnorm(halo_ref[...], gpre_ref[...]))
        h_scr[0:H, :] = halo.astype(bf16)
        _norm_rows_to(x_ref, gpre_ref, h_scr, H, tm)
        o_ref[...] = jnp.zeros_like(o_ref)

    def conv(u_scr, row0, nrows, u_cols, w_cols, cw_ref, cb_ref):
        y = cb_ref[:, w_cols]
        for kk in range(CONV_WIDTH):
            y = y + u_scr[pl.ds(H - (CONV_WIDTH - 1) + kk + row0, nrows), u_cols] * cw_ref[kk:kk + 1, w_cols]
        return y

    h = h_scr[...]
    hm = tm // FFN_ROW_SPLIT
    parts = [None] * FFN_ROW_SPLIT
    uv = slice(0, MXU_COLS)
    ug = slice(MXU_COLS, 2 * MXU_COLS)
    n_chunks = len(u_scrs)
    for c, u_scr in enumerate(u_scrs):
        cs = slice(c * MXU_COLS, (c + 1) * MXU_COLS)
        if c < n_chunks - 1:
            u_scr[:, uv] = jnp.dot(h, wv_ref[:, cs], preferred_element_type=f32)
            u_scr[:, ug] = jnp.dot(h, wg_ref[:, cs], preferred_element_type=f32)
        else:
            wv, wg = wv_ref[:, cs], wg_ref[:, cs]
            r0 = 0
            for m in range(FFN_ROW_SPLIT):
                r1 = H + (m + 1) * hm
                u_scr[r0:r1, uv] = jnp.dot(h_scr[r0:r1, :], wv, preferred_element_type=f32)
                u_scr[r0:r1, ug] = jnp.dot(h_scr[r0:r1, :], wg, preferred_element_type=f32)
                r0 = r1
        for m in range(FFN_ROW_SPLIT):
            val = conv(u_scr, m * hm, hm, uv, cs, cwv_ref, cbv_ref)
            gate = conv(u_scr, m * hm, hm, ug, cs, cwg_ref, cbg_ref)
            a = (_gelu_tanh(gate) * val).astype(bf16)
            d = jnp.dot(a, wd_ref[cs, :], preferred_element_type=f32)
            parts[m] = d if parts[m] is None else parts[m] + d
    for m in range(FFN_ROW_SPLIT):
        o_ref[m * hm:(m + 1) * hm, :] += parts[m]

    @pl.when(j == pl.num_programs(1) - 1)
    def _():
        _residual_norm_rows(x_ref, o_ref, gpost_ref, o_ref, tm)


def _ffn(x1, g_pre, w_up, conv_w, conv_b, w_down, g_post, seq):
    T = x1.shape[0]
    tm, tn = min(FFN_TM, seq), FFN_TN
    nj = D_FF // tn
    hb = tm // FFN_HALO
    kern = functools.partial(_ffn_kernel, tiles_per_seq=seq // tm)
    return pl.pallas_call(
        kern,
        grid=(T // tm, nj),
        in_specs=[
            pl.BlockSpec((tm, D_MODEL), lambda i, j: (i, 0)),
            pl.BlockSpec((FFN_HALO, D_MODEL), lambda i, j: (jnp.maximum(i * hb - 1, 0), 0)),
            pl.BlockSpec((1, D_MODEL), lambda i, j: (0, 0)),
            pl.BlockSpec((D_MODEL, tn), lambda i, j: (0, j)),
            pl.BlockSpec((D_MODEL, tn), lambda i, j: (0, nj + j)),
            pl.BlockSpec((CONV_WIDTH, tn), lambda i, j: (0, j)),
            pl.BlockSpec((CONV_WIDTH, tn), lambda i, j: (0, nj + j)),
            pl.BlockSpec((1, tn), lambda i, j: (0, j)),
            pl.BlockSpec((1, tn), lambda i, j: (0, nj + j)),
            pl.BlockSpec((tn, D_MODEL), lambda i, j: (j, 0)),
            pl.BlockSpec((1, D_MODEL), lambda i, j: (0, 0)),
        ],
        out_specs=pl.BlockSpec((tm, D_MODEL), lambda i, j: (i, 0)),
        out_shape=jax.ShapeDtypeStruct((T, D_MODEL), f32),
        scratch_shapes=[
            pltpu.VMEM((tm + FFN_HALO, D_MODEL), bf16),
        ] + [pltpu.VMEM((tm + FFN_HALO, 2 * MXU_COLS), f32)] * (tn // MXU_COLS),
        compiler_params=_params(("parallel", "arbitrary")),
        name="ffn",
    )(x1, x1, g_pre, w_up, w_up, conv_w, conv_w, conv_b, conv_b, w_down, g_post)


def _rotary_tables(seq):
    d = RET_DK
    inv = 1.0 / (ROPE_BASE ** (jnp.arange(0, d, 2, dtype=f32) / d))
    ang = jnp.arange(seq, dtype=jnp.int32).astype(f32)[:, None] * inv[None, :]
    cos, sin = jnp.cos(ang), jnp.sin(ang)
    tab = jnp.concatenate([cos, cos, -sin, sin], axis=-1)
    return jnp.stack([tab, tab * (d ** -0.5)])


def kernel(x, g_pre_mix, w_in, w_ret_o, w_swa_o, w_out, swa_sinks, g_post_mix, g_pre_ffn, w_up, conv_w, conv_b,
           w_down, g_post_ffn):
    B, S, D = x.shape
    depth = w_in.shape[0]
    assert D == D_MODEL and w_in.shape[1:] == (D_MODEL, IN_WIDTH) and w_up.shape[1:] == (D_MODEL, 2 * D_FF)
    assert S % RET_CHUNK == 0 and S % SWA_BLOCK == 0
    assert S % min(INPROJ_TM, S) == 0 and S % min(FFN_TM, S) == 0 and (B * S) % min(MIX_TM, B * S) == 0
    rot = _rotary_tables(S)
    x2 = x.reshape(B * S, D)
    for l in range(depth):
        proj = _inproj(x2, g_pre_mix[l][None], w_in[l].astype(bf16), rot, S)
        ret = _retention(proj, B, S)
        swa = _swa(proj, swa_sinks[l].astype(f32), B, S)
        x2 = _mix(ret, swa, proj, w_ret_o[l].astype(bf16), w_swa_o[l].astype(bf16), w_out[l].astype(bf16), x2,
                  g_post_mix[l][None])
        x2 = _ffn(x2, g_pre_ffn[l][None], w_up[l].astype(bf16), conv_w[l], conv_b[l][None], w_down[l].astype(bf16),
                  g_post_ffn[l][None], S)
    return x2.reshape(B, S, D)
```

```python
import functools

import jax
import jax.numpy as jnp
from jax import lax
from jax.experimental import pallas as pl
from jax.experimental.pallas import tpu as pltpu

D_MODEL = 2048
RET_HEADS = 8
RET_DK = 128
RET_DV = 256
RET_CHUNK = 256
ROPE_BASE = 10000.0
SWA_Q_HEADS = 16
SWA_KV_HEADS = 4
SWA_HEAD_DIM = 64
SWA_BLOCK = 128
D_FF = 5632
CONV_WIDTH = 3
RMS_EPS = 1e-6

RET_QK_W = RET_HEADS * RET_DK
RET_V_W = RET_HEADS * RET_DV
SWA_Q_W = SWA_Q_HEADS * SWA_HEAD_DIM
SWA_KV_W = SWA_KV_HEADS * SWA_HEAD_DIM

COL_RQ = 0
COL_RK = COL_RQ + RET_QK_W
COL_RV = COL_RK + RET_QK_W
COL_RG = COL_RV + RET_V_W
COL_SQ = COL_RG + RET_V_W
COL_SK = COL_SQ + SWA_Q_W
COL_SV = COL_SK + SWA_KV_W
COL_GR = COL_SV + SWA_KV_W
COL_GS = COL_GR + D_MODEL
IN_WIDTH = COL_GS + D_MODEL

V7X_VMEM_LIMIT_BYTES = 60 * 1024 * 1024
LANES = 128
BF16_SUBLANES = 16
MXU_COLS = 256

NORM_ROWS = 32
NORM_UNROLL = 4
INPROJ_TM = 1024
INPROJ_TN = 1024
INPROJ_ROW_SPLIT = 4
MIX_TM = 512
MIX_TN = 1024
MIX_GATE_W = 512
FFN_TM = 1024
FFN_TN = 512
FFN_HALO = BF16_SUBLANES
FFN_ROW_SPLIT = 4
RET_HEADS_PER_STEP = 2
SWA_KV_PER_STEP = 2
SWA_UNROLL = 8

f32 = jnp.float32
bf16 = jnp.bfloat16


def _params(semantics):
    return pltpu.CompilerParams(dimension_semantics=semantics, vmem_limit_bytes=V7X_VMEM_LIMIT_BYTES)


def _rmsnorm(x, g):
    ms = jnp.mean(x * x, axis=-1, keepdims=True)
    return x * lax.rsqrt(ms + RMS_EPS) * g


def _norm_rows_to(x_ref, g_ref, dst_ref, dst_off, rows):
    g = g_ref[...]
    group = NORM_ROWS * NORM_UNROLL

    def step(c, carry):
        base = pl.multiple_of(c * group, group)
        for k in range(NORM_UNROLL):
            r = base + k * NORM_ROWS
            dst_ref[pl.ds(dst_off + r, NORM_ROWS), :] = (
                _rmsnorm(x_ref[pl.ds(r, NORM_ROWS), :], g).astype(dst_ref.dtype))
        return carry

    lax.fori_loop(0, rows // group, step, 0)


def _residual_norm_rows(x_ref, y_ref, g_ref, o_ref, rows):
    g = g_ref[...]
    group = NORM_ROWS * NORM_UNROLL

    def step(c, carry):
        base = pl.multiple_of(c * group, group)
        rows_k = [pl.ds(base + k * NORM_ROWS, NORM_ROWS) for k in range(NORM_UNROLL)]
        scales = []
        for r in rows_k:
            y = y_ref[r, :]
            scales.append(lax.rsqrt(jnp.mean(y * y, axis=-1, keepdims=True) + RMS_EPS))
        for r, s in zip(rows_k, scales):
            o_ref[r, :] = x_ref[r, :] + y_ref[r, :] * s * g
        return carry

    lax.fori_loop(0, rows // group, step, 0)


def _inproj_tile_kinds():
    starts = ((COL_RQ, "rot"), (COL_RK, "rot"), (COL_RV, "cast"), (COL_RG, "silu"), (COL_SQ, "cast"),
              (COL_GR, "sig"))
    chunk_kinds = [[k for c0, k in starts if c0 <= col][-1] for col in range(0, IN_WIDTH, MXU_COLS)]
    per_tile = INPROJ_TN // MXU_COLS
    return [tuple(chunk_kinds[t:t + per_tile]) for t in range(0, len(chunk_kinds), per_tile)]


_INPROJ_TILE_KINDS = _inproj_tile_kinds()
_T_RK = COL_RK // INPROJ_TN
assert COL_RK % INPROJ_TN == 0 and COL_RV % INPROJ_TN == 0


def _inproj_kernel(x_ref, g_ref, w_ref, rot_ref, o_ref, h_scr):
    j = pl.program_id(1)
    tm = x_ref.shape[0]

    @pl.when(j == 0)
    def _():
        _norm_rows_to(x_ref, g_ref, h_scr, 0, tm)

    def rotary(acc, rows):
        cos = rot_ref[rows, 0:RET_DK]
        sin = rot_ref[rows, RET_DK:2 * RET_DK]
        heads = []
        for hh in range(acc.shape[1] // RET_DK):
            xh = acc[:, hh * RET_DK:(hh + 1) * RET_DK]
            heads.append(xh * cos + pltpu.roll(xh, RET_DK // 2, axis=1) * sin)
        return jnp.concatenate(heads, axis=1)

    epilogues = {
        "rot": rotary,
        "cast": lambda acc, rows: acc,
        "silu": lambda acc, rows: acc * jax.nn.sigmoid(acc),
        "sig": lambda acc, rows: jax.nn.sigmoid(acc),
    }

    def tile(kinds):
        hm = tm // INPROJ_ROW_SPLIT
        for c, kind in enumerate(kinds):
            cs = slice(c * MXU_COLS, (c + 1) * MXU_COLS)
            w = w_ref[:, cs]
            for m in range(INPROJ_ROW_SPLIT):
                rows = slice(m * hm, (m + 1) * hm)
                acc = jnp.dot(h_scr[rows, :], w, preferred_element_type=f32)
                o_ref[rows, cs] = epilogues[kind](acc, rows).astype(o_ref.dtype)

    for kinds in sorted(set(_INPROJ_TILE_KINDS)):
        tiles = [t for t, k in enumerate(_INPROJ_TILE_KINDS) if k == kinds]
        cond = functools.reduce(lambda a, b: a | b, [j == t for t in tiles])
        pl.when(cond)(functools.partial(tile, kinds))


def _inproj(x2, g, w_in, rot, seq):
    T = x2.shape[0]
    tm, tn = INPROJ_TM, INPROJ_TN
    tm = min(tm, seq)
    spt = seq // tm
    grid = (T // tm, len(_INPROJ_TILE_KINDS))
    return pl.pallas_call(
        _inproj_kernel,
        grid=grid,
        in_specs=[
            pl.BlockSpec((tm, D_MODEL), lambda i, j: (i, 0)),
            pl.BlockSpec((1, D_MODEL), lambda i, j: (0, 0)),
            pl.BlockSpec((D_MODEL, tn), lambda i, j: (0, j)),
            pl.BlockSpec((None, tm, 2 * RET_DK), lambda i, j: (jnp.where(j >= _T_RK, 1, 0), i % spt, 0)),
        ],
        out_specs=pl.BlockSpec((tm, tn), lambda i, j: (i, j)),
        out_shape=jax.ShapeDtypeStruct((T, IN_WIDTH), bf16),
        scratch_shapes=[pltpu.VMEM((tm, D_MODEL), bf16)],
        compiler_params=_params(("parallel", "arbitrary")),
        name="inproj",
    )(x2, g, w_in, rot)


def _retention_kernel(q_ref, k_ref, v_ref, gate_ref, dmat_ref, xi_ref, zeta_ref, cdec_ref, o_ref, kv_scr, st_scr):
    S = q_ref.shape[0]
    C = RET_CHUNK
    N = S // C
    heads = range(RET_HEADS_PER_STEP)
    qk = [slice(hh * RET_DK, (hh + 1) * RET_DK) for hh in heads]
    vv = [slice(hh * RET_DV, (hh + 1) * RET_DV) for hh in heads]
    q3 = [q_ref[:, qk[hh]].reshape(N, C, RET_DK) for hh in heads]
    k3 = [k_ref[:, qk[hh]].reshape(N, C, RET_DK) for hh in heads]
    v3 = [v_ref[:, vv[hh]].reshape(N, C, RET_DV) for hh in heads]

    for hh in heads:
        kz = (k3[hh].astype(f32) * zeta_ref[hh][None]).astype(bf16)
        kv_scr[hh] = jnp.einsum("ndk,nkv->ndv", jnp.swapaxes(kz, 1, 2), v3[hh], preferred_element_type=f32)

    cdec = [cdec_ref[hh] for hh in heads]

    def scan(n, states):
        for hh in heads:
            st_scr[hh, n] = states[hh].astype(bf16)
        return tuple(states[hh] * cdec[hh] + kv_scr[hh, n] for hh in heads)

    lax.fori_loop(0, N, scan, tuple(jnp.zeros((RET_DK, RET_DV), f32) for _ in heads))

    for hh in heads:
        sc = jnp.einsum("nqd,nkd->nqk", q3[hh], k3[hh], preferred_element_type=f32) * dmat_ref[hh][None]
        inner = jnp.einsum("nqk,nkv->nqv", sc.astype(bf16), v3[hh], preferred_element_type=f32)
        cross = jnp.einsum("nqd,ndv->nqv", q3[hh], st_scr[hh], preferred_element_type=f32) * xi_ref[hh][None]
        o = inner + cross
        o = o * lax.rsqrt(jnp.mean(o * o, axis=-1, keepdims=True) + RMS_EPS)
        o = gate_ref[:, vv[hh]].astype(f32).reshape(N, C, RET_DV) * o
        o_ref[:, vv[hh]] = o.reshape(S, RET_DV).astype(o_ref.dtype)


def _retention_tables():
    C = RET_CHUNK
    log_gamma = jnp.log(1.0 - 2.0 ** (-5.0 - jnp.arange(RET_HEADS, dtype=f32)))
    idx = jnp.arange(C, dtype=f32)
    rel = idx[:, None] - idx[None, :]
    dmat = jnp.where(rel[None] >= 0, jnp.exp(log_gamma[:, None, None] * jnp.maximum(rel, 0.0)[None]), 0.0)
    xi = jnp.exp(log_gamma[:, None] * (idx + 1.0))
    zeta = jnp.exp(log_gamma[:, None] * (C - 1.0 - idx))
    cdec = jnp.exp(log_gamma * C)
    xi_b = jnp.broadcast_to(xi[:, :, None], (RET_HEADS, C, RET_DV))
    zeta_b = jnp.broadcast_to(zeta[:, :, None], (RET_HEADS, C, RET_DK))
    cdec_b = jnp.broadcast_to(cdec[:, None, None], (RET_HEADS, 1, RET_DV))
    return dmat.astype(f32), xi_b.astype(f32), zeta_b.astype(f32), cdec_b.astype(f32)


def _retention(proj, batch, seq):
    T = proj.shape[0]
    C = RET_CHUNK
    N = seq // C
    dmat, xi_b, zeta_b, cdec_b = _retention_tables()
    hp = RET_HEADS_PER_STEP
    qk_w, v_w = hp * RET_DK, hp * RET_DV
    return pl.pallas_call(
        _retention_kernel,
        grid=(batch, RET_HEADS // hp),
        in_specs=[
            pl.BlockSpec((seq, qk_w), lambda b, h: (b, COL_RQ // qk_w + h)),
            pl.BlockSpec((seq, qk_w), lambda b, h: (b, COL_RK // qk_w + h)),
            pl.BlockSpec((seq, v_w), lambda b, h: (b, COL_RV // v_w + h)),
            pl.BlockSpec((seq, v_w), lambda b, h: (b, COL_RG // v_w + h)),
            pl.BlockSpec((hp, C, C), lambda b, h: (h, 0, 0)),
            pl.BlockSpec((hp, C, RET_DV), lambda b, h: (h, 0, 0)),
            pl.BlockSpec((hp, C, RET_DK), lambda b, h: (h, 0, 0)),
            pl.BlockSpec((hp, 1, RET_DV), lambda b, h: (h, 0, 0)),
        ],
        out_specs=pl.BlockSpec((seq, v_w), lambda b, h: (b, h)),
        out_shape=jax.ShapeDtypeStruct((T, RET_V_W), bf16),
        scratch_shapes=[pltpu.VMEM((hp, N, RET_DK, RET_DV), f32), pltpu.VMEM((hp, N, RET_DK, RET_DV), bf16)],
        compiler_params=_params(("parallel", "parallel")),
        name="retention",
    )(proj, proj, proj, proj, dmat, xi_b, zeta_b, cdec_b)


KV_PER_STEP = SWA_KV_PER_STEP
Q_PER_KV = SWA_Q_HEADS // SWA_KV_HEADS
SWA_Q_STEP_W = KV_PER_STEP * Q_PER_KV * SWA_HEAD_DIM
assert KV_PER_STEP * SWA_HEAD_DIM == LANES and Q_PER_KV * SWA_HEAD_DIM == 2 * LANES


def _swa_kernel(sink_ref, bias_ref, q_ref, k_ref, v_ref, o_ref, klo, khi, vlo, vhi):
    S = q_ref.shape[0]
    C = SWA_BLOCK
    gp = pl.program_id(1)
    half = SWA_HEAD_DIM

    lane_s = lax.broadcasted_iota(jnp.int32, (S, LANES), 1)
    low_s = lane_s < half

    def prep(src_ref, lo_scr, hi_scr, scale, with_ones):
        w = src_ref[...].astype(f32) * scale
        r = pltpu.roll(w, half, axis=1)
        zeros = jnp.zeros((C, lo_scr.shape[2]), bf16)
        for t in range(KV_PER_STEP):
            lo_scr[t, 0:C, :] = zeros
            hi_scr[t, 0:C, :] = zeros
        lo_scr[0, C:C + S, 0:LANES] = jnp.where(low_s, w, 0.0).astype(bf16)
        hi_scr[0, C:C + S, 0:LANES] = jnp.where(low_s, 0.0, r).astype(bf16)
        lo_scr[1, C:C + S, 0:LANES] = jnp.where(low_s, r, 0.0).astype(bf16)
        hi_scr[1, C:C + S, 0:LANES] = jnp.where(low_s, 0.0, w).astype(bf16)
        if with_ones:
            for t in range(KV_PER_STEP):
                lo_scr[t, C:C + S, LANES:2 * LANES] = jnp.where(low_s, 1.0, 0.0).astype(bf16)
                hi_scr[t, C:C + S, LANES:2 * LANES] = jnp.where(low_s, 0.0, 1.0).astype(bf16)

    prep(k_ref, klo, khi, SWA_HEAD_DIM ** -0.5, False)
    prep(v_ref, vlo, vhi, 1.0, True)

    rows = 2 * C
    win = 2 * C
    first_pair = lax.broadcasted_iota(jnp.int32, (rows, 1), 0) < C
    low_o = lax.broadcasted_iota(jnp.int32, (rows, LANES), 1) < half
    nt = (((1,), (1,)), ((), ()))

    def softmax_parts(s, sink, bias):
        s = s + bias
        m = jnp.maximum(jnp.max(s, axis=-1, keepdims=True), sink)
        return jnp.exp(s - m).astype(bf16), jnp.exp(sink - m)

    def block(n, carry):
        r0 = pl.multiple_of(n * C, C)
        bias = bias_ref[jnp.minimum(n, 1)]
        for t in range(KV_PER_STEP):
            c0 = t * Q_PER_KV * SWA_HEAD_DIM
            hbase = gp * (KV_PER_STEP * Q_PER_KV) + t * Q_PER_KV
            qs = jnp.concatenate([q_ref[pl.ds(r0, C), c0:c0 + LANES],
                                  q_ref[pl.ds(r0, C), c0 + LANES:c0 + 2 * LANES]], axis=0)
            s_e = lax.dot_general(qs, klo[t, pl.ds(r0, win), :], nt, preferred_element_type=f32)
            s_o = lax.dot_general(qs, khi[t, pl.ds(r0, win), :], nt, preferred_element_type=f32)
            sink_e = jnp.where(first_pair, sink_ref[hbase + 0], sink_ref[hbase + 2])
            sink_o = jnp.where(first_pair, sink_ref[hbase + 1], sink_ref[hbase + 3])
            p_e, z_e = softmax_parts(s_e, sink_e, bias)
            p_o, z_o = softmax_parts(s_o, sink_o, bias)
            pv = (jnp.dot(p_e, vlo[t, pl.ds(r0, win), :], preferred_element_type=f32)
                  + jnp.dot(p_o, vhi[t, pl.ds(r0, win), :], preferred_element_type=f32))
            den = pv[:, LANES:2 * LANES] + jnp.where(low_o, z_e, z_o)
            o = pv[:, 0:LANES] / den
            o_ref[pl.ds(r0, C), c0:c0 + LANES] = o[0:C].astype(o_ref.dtype)
            o_ref[pl.ds(r0, C), c0 + LANES:c0 + 2 * LANES] = o[C:2 * C].astype(o_ref.dtype)
        return carry

    lax.fori_loop(0, S // C, block, 0, unroll=SWA_UNROLL)


def _swa_bias():
    C = SWA_BLOCK
    row_i = jnp.arange(2 * C)[:, None] % C
    col_j = jnp.arange(2 * C)[None, :]
    band = (col_j > row_i) & (col_j <= row_i + C)
    first = band & (col_j >= C)
    return jnp.where(jnp.stack([first, band]), 0.0, -jnp.inf).astype(f32)


def _swa(proj, sinks, batch, seq):
    T = proj.shape[0]
    steps = SWA_KV_HEADS // KV_PER_STEP
    k_scr = pltpu.VMEM((KV_PER_STEP, seq + SWA_BLOCK, LANES), bf16)
    v_scr = pltpu.VMEM((KV_PER_STEP, seq + SWA_BLOCK, 2 * LANES), bf16)
    bias = _swa_bias()
    return pl.pallas_call(
        _swa_kernel,
        grid=(batch, steps),
        in_specs=[
            pl.BlockSpec(memory_space=pltpu.SMEM),
            pl.BlockSpec(bias.shape, lambda b, g: (0, 0, 0)),
            pl.BlockSpec((seq, SWA_Q_STEP_W), lambda b, g: (b, COL_SQ // SWA_Q_STEP_W + g)),
            pl.BlockSpec((seq, LANES), lambda b, g: (b, COL_SK // LANES + g)),
            pl.BlockSpec((seq, LANES), lambda b, g: (b, COL_SV // LANES + g)),
        ],
        out_specs=pl.BlockSpec((seq, SWA_Q_STEP_W), lambda b, g: (b, g)),
        out_shape=jax.ShapeDtypeStruct((T, SWA_Q_W), bf16),
        scratch_shapes=[k_scr, k_scr, v_scr, v_scr],
        compiler_params=_params(("parallel", "parallel")),
        name="swa",
    )(sinks, bias, proj, proj, proj)


def _mix_kernel(ret_ref, swa_ref, *refs):
    n_gate = MIX_TN // MIX_GATE_W
    gr_refs, gs_refs = refs[:n_gate], refs[n_gate:2 * n_gate]
    wr_ref, ws_ref, wo_ref, x_ref, g_ref, o_ref = refs[2 * n_gate:]
    j = pl.program_id(1)

    @pl.when(j == 0)
    def _():
        o_ref[...] = jnp.zeros_like(o_ref)

    part = None
    for c in range(n_gate):
        cs = slice(c * MIX_GATE_W, (c + 1) * MIX_GATE_W)
        ret_out = jnp.dot(ret_ref[...], wr_ref[:, cs], preferred_element_type=f32)
        swa_out = jnp.dot(swa_ref[...], ws_ref[:, cs], preferred_element_type=f32)
        mixed = gr_refs[c][...].astype(f32) * ret_out + gs_refs[c][...].astype(f32) * swa_out
        d = jnp.dot(mixed.astype(bf16), wo_ref[cs, :], preferred_element_type=f32)
        part = d if part is None else part + d
    o_ref[...] += part

    @pl.when(j == pl.num_programs(1) - 1)
    def _():
        _residual_norm_rows(x_ref, o_ref, g_ref, o_ref, o_ref.shape[0])


def _mix(ret, swa, proj, w_ret_o, w_swa_o, w_out, x2, g):
    T = x2.shape[0]
    tm, tn = min(MIX_TM, T), MIX_TN
    nj = D_MODEL // tn
    gw = MIX_GATE_W
    n_gate = tn // gw

    def gate_specs(col0):
        return [pl.BlockSpec((tm, gw), functools.partial(lambda i, j, c: (i, col0 // gw + j * n_gate + c), c=c))
                for c in range(n_gate)]

    return pl.pallas_call(
        _mix_kernel,
        grid=(T // tm, nj),
        in_specs=[
            pl.BlockSpec((tm, RET_V_W), lambda i, j: (i, 0)),
            pl.BlockSpec((tm, SWA_Q_W), lambda i, j: (i, 0)),
            *gate_specs(COL_GR),
            *gate_specs(COL_GS),
            pl.BlockSpec((RET_V_W, tn), lambda i, j: (0, j)),
            pl.BlockSpec((SWA_Q_W, tn), lambda i, j: (0, j)),
            pl.BlockSpec((tn, D_MODEL), lambda i, j: (j, 0)),
            pl.BlockSpec((tm, D_MODEL), lambda i, j: (i, 0)),
            pl.BlockSpec((1, D_MODEL), lambda i, j: (0, 0)),
        ],
        out_specs=pl.BlockSpec((tm, D_MODEL), lambda i, j: (i, 0)),
        out_shape=jax.ShapeDtypeStruct((T, D_MODEL), f32),
        compiler_params=_params(("parallel", "arbitrary")),
        name="mix",
    )(ret, swa, *([proj] * (2 * n_gate)), w_ret_o, w_swa_o, w_out, x2, g)


def _gelu_tanh(x):
    return x * (0.5 * (1.0 + jnp.tanh(0.7978845608028654 * (x + 0.044715 * (x * x * x)))))


def _ffn_kernel(x_ref, halo_ref, gpre_ref, wv_ref, wg_ref, cw_ref, cb_ref, wd_ref, gpost_ref,
                o_ref, h_scr, *u_scrs, tiles_per_seq):
    i = pl.program_id(0)
    j = pl.program_id(1)
    tm = x_ref.shape[0]
    tn = wv_ref.shape[1]
    H = FFN_HALO

    @pl.when(j == 0)
    def _():
        halo = jnp.where(i % tiles_per_seq == 0, 0.0, _rmsnorm(halo_ref[...], gpre_ref[...]))
        h_scr[0:H, :] = halo.astype(bf16)
        _norm_rows_to(x_ref, gpre_ref, h_scr, H, tm)
        o_ref[...] = jnp.zeros_like(o_ref)

    def conv(u_scr, row0, nrows, u_cols, col0):
        w_cols = pl.ds(pl.multiple_of(col0, MXU_COLS), MXU_COLS)
        y = cb_ref[:, w_cols]
        for kk in range(CONV_WIDTH):
            y = y + u_scr[pl.ds(H - (CONV_WIDTH - 1) + kk + row0, nrows), u_cols] * cw_ref[kk:kk + 1, w_cols]
        return y

    h = h_scr[...]
    hm = tm // FFN_ROW_SPLIT
    parts = [None] * FFN_ROW_SPLIT
    uv = slice(0, MXU_COLS)
    ug = slice(MXU_COLS, 2 * MXU_COLS)
    n_chunks = len(u_scrs)
    for c, u_scr in enumerate(u_scrs):
        cs = slice(c * MXU_COLS, (c + 1) * MXU_COLS)
        if c < n_chunks - 1:
            u_scr[:, uv] = jnp.dot(h, wv_ref[:, cs], preferred_element_type=f32)
            u_scr[:, ug] = jnp.dot(h, wg_ref[:, cs], preferred_element_type=f32)
        else:
            wv, wg = wv_ref[:, cs], wg_ref[:, cs]
            r0 = 0
            for m in range(FFN_ROW_SPLIT):
                r1 = H + (m + 1) * hm
                u_scr[r0:r1, uv] = jnp.dot(h_scr[r0:r1, :], wv, preferred_element_type=f32)
                u_scr[r0:r1, ug] = jnp.dot(h_scr[r0:r1, :], wg, preferred_element_type=f32)
                r0 = r1
        for m in range(FFN_ROW_SPLIT):
            val = conv(u_scr, m * hm, hm, uv, j * tn + c * MXU_COLS)
            gate = conv(u_scr, m * hm, hm, ug, D_FF + j * tn + c * MXU_COLS)
            a = (_gelu_tanh(gate) * val).astype(bf16)
            d = jnp.dot(a, wd_ref[cs, :], preferred_element_type=f32)
            parts[m] = d if parts[m] is None else parts[m] + d
    for m in range(FFN_ROW_SPLIT):
        o_ref[m * hm:(m + 1) * hm, :] += parts[m]

    @pl.when(j == pl.num_programs(1) - 1)
    def _():
        _residual_norm_rows(x_ref, o_ref, gpost_ref, o_ref, tm)


def _ffn(x1, g_pre, w_up, conv_w, conv_b, w_down, g_post, seq):
    T = x1.shape[0]
    tm, tn = min(FFN_TM, seq), FFN_TN
    nj = D_FF // tn
    hb = tm // FFN_HALO
    kern = functools.partial(_ffn_kernel, tiles_per_seq=seq // tm)
    return pl.pallas_call(
        kern,
        grid=(T // tm, nj),
        in_specs=[
            pl.BlockSpec((tm, D_MODEL), lambda i, j: (i, 0)),
            pl.BlockSpec((FFN_HALO, D_MODEL), lambda i, j: (jnp.maximum(i * hb - 1, 0), 0)),
            pl.BlockSpec((1, D_MODEL), lambda i, j: (0, 0)),
            pl.BlockSpec((D_MODEL, tn), lambda i, j: (0, j)),
            pl.BlockSpec((D_MODEL, tn), lambda i, j: (0, nj + j)),
            pl.BlockSpec((CONV_WIDTH, 2 * D_FF), lambda i, j: (0, 0)),
            pl.BlockSpec((1, 2 * D_FF), lambda i, j: (0, 0)),
            pl.BlockSpec((tn, D_MODEL), lambda i, j: (j, 0)),
            pl.BlockSpec((1, D_MODEL), lambda i, j: (0, 0)),
        ],
        out_specs=pl.BlockSpec((tm, D_MODEL), lambda i, j: (i, 0)),
        out_shape=jax.ShapeDtypeStruct((T, D_MODEL), f32),
        scratch_shapes=[
            pltpu.VMEM((tm + FFN_HALO, D_MODEL), bf16),
        ] + [pltpu.VMEM((tm + FFN_HALO, 2 * MXU_COLS), f32)] * (tn // MXU_COLS),
        compiler_params=_params(("parallel", "arbitrary")),
        name="ffn",
    )(x1, x1, g_pre, w_up, w_up, conv_w, conv_b, w_down, g_post)


def _rotary_tables(seq):
    d = RET_DK
    inv = 1.0 / (ROPE_BASE ** (jnp.arange(0, d, 2, dtype=f32) / d))
    ang = jnp.arange(seq, dtype=jnp.int32).astype(f32)[:, None] * inv[None, :]
    cos, sin = jnp.cos(ang), jnp.sin(ang)
    tab = jnp.concatenate([cos, cos, -sin, sin], axis=-1)
    return jnp.stack([tab, tab * (d ** -0.5)])


def kernel(x, g_pre_mix, w_in, w_ret_o, w_swa_o, w_out, swa_sinks, g_post_mix, g_pre_ffn, w_up, conv_w, conv_b,
           w_down, g_post_ffn):
    B, S, D = x.shape
    depth = w_in.shape[0]
    assert D == D_MODEL and w_in.shape[1:] == (D_MODEL, IN_WIDTH) and w_up.shape[1:] == (D_MODEL, 2 * D_FF)
    assert S % RET_CHUNK == 0 and S % SWA_BLOCK == 0
    assert S % min(INPROJ_TM, S) == 0 and S % min(FFN_TM, S) == 0 and (B * S) % min(MIX_TM, B * S) == 0
    rot = _rotary_tables(S)
    x2 = x.reshape(B * S, D)
    for l in range(depth):
        proj = _inproj(x2, g_pre_mix[l][None], w_in[l].astype(bf16), rot, S)
        ret = _retention(proj, B, S)
        swa = _swa(proj, swa_sinks[l].astype(f32), B, S)
        x2 = _mix(ret, swa, proj, w_ret_o[l].astype(bf16), w_swa_o[l].astype(bf16), w_out[l].astype(bf16), x2,
                  g_post_mix[l][None])
        x2 = _ffn(x2, g_pre_ffn[l][None], w_up[l].astype(bf16), conv_w[l], conv_b[l][None], w_down[l].astype(bf16),
                  g_post_ffn[l][None], S)
    return x2.reshape(B, S, D)
```

```python
import functools

import jax
import jax.numpy as jnp
from jax import lax
from jax.experimental import pallas as pl
from jax.experimental.pallas import tpu as pltpu

D_MODEL = 2048
RET_HEADS = 8
RET_DK = 128
RET_DV = 256
RET_CHUNK = 256
ROPE_BASE = 10000.0
SWA_Q_HEADS = 16
SWA_KV_HEADS = 4
SWA_HEAD_DIM = 64
SWA_BLOCK = 128
D_FF = 5632
CONV_WIDTH = 3
RMS_EPS = 1e-6

RET_QK_W = RET_HEADS * RET_DK
RET_V_W = RET_HEADS * RET_DV
SWA_Q_W = SWA_Q_HEADS * SWA_HEAD_DIM
SWA_KV_W = SWA_KV_HEADS * SWA_HEAD_DIM

COL_RQ = 0
COL_RK = COL_RQ + RET_QK_W
COL_RV = COL_RK + RET_QK_W
COL_RG = COL_RV + RET_V_W
COL_SQ = COL_RG + RET_V_W
COL_SK = COL_SQ + SWA_Q_W
COL_SV = COL_SK + SWA_KV_W
COL_GR = COL_SV + SWA_KV_W
COL_GS = COL_GR + D_MODEL
IN_WIDTH = COL_GS + D_MODEL

V7X_VMEM_LIMIT_BYTES = 60 * 1024 * 1024
LANES = 128
BF16_SUBLANES = 16
MXU_COLS = 256

NORM_ROWS = 32
NORM_UNROLL = 4
INPROJ_TM = 1024
INPROJ_TN = 1024
INPROJ_ROW_SPLIT = 4
MIX_TM = 512
MIX_TN = 1024
MIX_GATE_W = 512
FFN_TM = 1024
FFN_TN = 512
FFN_HALO = BF16_SUBLANES
FFN_ROW_SPLIT = 4
RET_HEADS_PER_STEP = 2
SWA_KV_PER_STEP = 2
SWA_UNROLL = 8

f32 = jnp.float32
bf16 = jnp.bfloat16


def _params(semantics):
    return pltpu.CompilerParams(dimension_semantics=semantics, vmem_limit_bytes=V7X_VMEM_LIMIT_BYTES)


def _rmsnorm(x, g):
    ms = jnp.mean(x * x, axis=-1, keepdims=True)
    return x * lax.rsqrt(ms + RMS_EPS) * g


def _norm_rows_to(x_ref, g_ref, dst_ref, dst_off, rows):
    g = g_ref[...]
    group = NORM_ROWS * NORM_UNROLL

    def step(c, carry):
        base = pl.multiple_of(c * group, group)
        for k in range(NORM_UNROLL):
            r = base + k * NORM_ROWS
            dst_ref[pl.ds(dst_off + r, NORM_ROWS), :] = (
                _rmsnorm(x_ref[pl.ds(r, NORM_ROWS), :], g).astype(dst_ref.dtype))
        return carry

    lax.fori_loop(0, rows // group, step, 0)


def _residual_norm_rows(x_ref, y_ref, g_ref, o_ref, rows):
    g = g_ref[...]
    group = NORM_ROWS * NORM_UNROLL

    def step(c, carry):
        base = pl.multiple_of(c * group, group)
        rows_k = [pl.ds(base + k * NORM_ROWS, NORM_ROWS) for k in range(NORM_UNROLL)]
        scales = []
        for r in rows_k:
            y = y_ref[r, :]
            scales.append(lax.rsqrt(jnp.mean(y * y, axis=-1, keepdims=True) + RMS_EPS))
        for r, s in zip(rows_k, scales):
            o_ref[r, :] = x_ref[r, :] + y_ref[r, :] * s * g
        return carry

    lax.fori_loop(0, rows // group, step, 0)


def _inproj_tile_kinds():
    starts = ((COL_RQ, "rot"), (COL_RK, "rot"), (COL_RV, "cast"), (COL_RG, "silu"), (COL_SQ, "cast"),
              (COL_GR, "sig"))
    chunk_kinds = [[k for c0, k in starts if c0 <= col][-1] for col in range(0, IN_WIDTH, MXU_COLS)]
    per_tile = INPROJ_TN // MXU_COLS
    return [tuple(chunk_kinds[t:t + per_tile]) for t in range(0, len(chunk_kinds), per_tile)]


_INPROJ_TILE_KINDS = _inproj_tile_kinds()
_T_RK = COL_RK // INPROJ_TN
assert COL_RK % INPROJ_TN == 0 and COL_RV % INPROJ_TN == 0


def _inproj_kernel(x_ref, g_ref, w_ref, rot_ref, o_ref, h_scr):
    j = pl.program_id(1)
    tm = x_ref.shape[0]

    @pl.when(j == 0)
    def _():
        _norm_rows_to(x_ref, g_ref, h_scr, 0, tm)

    def rotary(acc, rows):
        cos = rot_ref[rows, 0:RET_DK]
        sin = rot_ref[rows, RET_DK:2 * RET_DK]
        heads = []
        for hh in range(acc.shape[1] // RET_DK):
            xh = acc[:, hh * RET_DK:(hh + 1) * RET_DK]
            heads.append(xh * cos + pltpu.roll(xh, RET_DK // 2, axis=1) * sin)
        return jnp.concatenate(heads, axis=1)

    epilogues = {
        "rot": rotary,
        "cast": lambda acc, rows: acc,
        "silu": lambda acc, rows: acc * jax.nn.sigmoid(acc),
        "sig": lambda acc, rows: jax.nn.sigmoid(acc),
    }

    def tile(kinds):
        hm = tm // INPROJ_ROW_SPLIT
        for c, kind in enumerate(kinds):
            cs = slice(c * MXU_COLS, (c + 1) * MXU_COLS)
            w = w_ref[:, cs]
            for m in range(INPROJ_ROW_SPLIT):
                rows = slice(m * hm, (m + 1) * hm)
                acc = jnp.dot(h_scr[rows, :], w, preferred_element_type=f32)
                o_ref[rows, cs] = epilogues[kind](acc, rows).astype(o_ref.dtype)

    for kinds in sorted(set(_INPROJ_TILE_KINDS)):
        tiles = [t for t, k in enumerate(_INPROJ_TILE_KINDS) if k == kinds]
        cond = functools.reduce(lambda a, b: a | b, [j == t for t in tiles])
        pl.when(cond)(functools.partial(tile, kinds))


def _inproj(x2, g, w_in, rot, seq):
    T = x2.shape[0]
    tm, tn = INPROJ_TM, INPROJ_TN
    tm = min(tm, seq)
    spt = seq // tm
    grid = (T // tm, len(_INPROJ_TILE_KINDS))
    return pl.pallas_call(
        _inproj_kernel,
        grid=grid,
        in_specs=[
            pl.BlockSpec((tm, D_MODEL), lambda i, j: (i, 0)),
            pl.BlockSpec((1, D_MODEL), lambda i, j: (0, 0)),
            pl.BlockSpec((D_MODEL, tn), lambda i, j: (0, j)),
            pl.BlockSpec((None, tm, 2 * RET_DK), lambda i, j: (jnp.where(j >= _T_RK, 1, 0), i % spt, 0)),
        ],
        out_specs=pl.BlockSpec((tm, tn), lambda i, j: (i, j)),
        out_shape=jax.ShapeDtypeStruct((T, IN_WIDTH), bf16),
        scratch_shapes=[pltpu.VMEM((tm, D_MODEL), bf16)],
        compiler_params=_params(("parallel", "arbitrary")),
        name="inproj",
    )(x2, g, w_in, rot)


def _retention_kernel(q_ref, k_ref, v_ref, gate_ref, dmat_ref, xi_ref, zeta_ref, cdec_ref, o_ref, kv_scr, st_scr):
    S = q_ref.shape[0]
    C = RET_CHUNK
    N = S // C
    heads = range(RET_HEADS_PER_STEP)
    qk = [slice(hh * RET_DK, (hh + 1) * RET_DK) for hh in heads]
    vv = [slice(hh * RET_DV, (hh + 1) * RET_DV) for hh in heads]
    q3 = [q_ref[:, qk[hh]].reshape(N, C, RET_DK) for hh in heads]
    k3 = [k_ref[:, qk[hh]].reshape(N, C, RET_DK) for hh in heads]
    v3 = [v_ref[:, vv[hh]].reshape(N, C, RET_DV) for hh in heads]

    for hh in heads:
        kz = (k3[hh].astype(f32) * zeta_ref[hh][None]).astype(bf16)
        kv_scr[hh] = jnp.einsum("ndk,nkv->ndv", jnp.swapaxes(kz, 1, 2), v3[hh], preferred_element_type=f32)

    cdec = [cdec_ref[hh] for hh in heads]

    def scan(n, states):
        for hh in heads:
            st_scr[hh, n] = states[hh].astype(bf16)
        return tuple(states[hh] * cdec[hh] + kv_scr[hh, n] for hh in heads)

    lax.fori_loop(0, N, scan, tuple(jnp.zeros((RET_DK, RET_DV), f32) for _ in heads))

    for hh in heads:
        sc = jnp.einsum("nqd,nkd->nqk", q3[hh], k3[hh], preferred_element_type=f32) * dmat_ref[hh][None]
        qx = (q3[hh].astype(f32) * xi_ref[hh][:, 0:RET_DK][None]).astype(bf16)
        lhs = jnp.concatenate([sc.astype(bf16), qx], axis=2)
        rhs = jnp.concatenate([v3[hh], st_scr[hh]], axis=1)
        o = jnp.einsum("nqk,nkv->nqv", lhs, rhs, preferred_element_type=f32)
        o = o * lax.rsqrt(jnp.mean(o * o, axis=-1, keepdims=True) + RMS_EPS)
        o = gate_ref[:, vv[hh]].astype(f32).reshape(N, C, RET_DV) * o
        o_ref[:, vv[hh]] = o.reshape(S, RET_DV).astype(o_ref.dtype)


def _retention_tables():
    C = RET_CHUNK
    log_gamma = jnp.log(1.0 - 2.0 ** (-5.0 - jnp.arange(RET_HEADS, dtype=f32)))
    idx = jnp.arange(C, dtype=f32)
    rel = idx[:, None] - idx[None, :]
    dmat = jnp.where(rel[None] >= 0, jnp.exp(log_gamma[:, None, None] * jnp.maximum(rel, 0.0)[None]), 0.0)
    xi = jnp.exp(log_gamma[:, None] * (idx + 1.0))
    zeta = jnp.exp(log_gamma[:, None] * (C - 1.0 - idx))
    cdec = jnp.exp(log_gamma * C)
    xi_b = jnp.broadcast_to(xi[:, :, None], (RET_HEADS, C, RET_DV))
    zeta_b = jnp.broadcast_to(zeta[:, :, None], (RET_HEADS, C, RET_DK))
    cdec_b = jnp.broadcast_to(cdec[:, None, None], (RET_HEADS, 1, RET_DV))
    return dmat.astype(f32), xi_b.astype(f32), zeta_b.astype(f32), cdec_b.astype(f32)


def _retention(proj, batch, seq):
    T = proj.shape[0]
    C = RET_CHUNK
    N = seq // C
    dmat, xi_b, zeta_b, cdec_b = _retention_tables()
    hp = RET_HEADS_PER_STEP
    qk_w, v_w = hp * RET_DK, hp * RET_DV
    return pl.pallas_call(
        _retention_kernel,
        grid=(batch, RET_HEADS // hp),
        in_specs=[
            pl.BlockSpec((seq, qk_w), lambda b, h: (b, COL_RQ // qk_w + h)),
            pl.BlockSpec((seq, qk_w), lambda b, h: (b, COL_RK // qk_w + h)),
            pl.BlockSpec((seq, v_w), lambda b, h: (b, COL_RV // v_w + h)),
            pl.BlockSpec((seq, v_w), lambda b, h: (b, COL_RG // v_w + h)),
            pl.BlockSpec((hp, C, C), lambda b, h: (h, 0, 0)),
            pl.BlockSpec((hp, C, RET_DV), lambda b, h: (h, 0, 0)),
            pl.BlockSpec((hp, C, RET_DK), lambda b, h: (h, 0, 0)),
            pl.BlockSpec((hp, 1, RET_DV), lambda b, h: (h, 0, 0)),
        ],
        out_specs=pl.BlockSpec((seq, v_w), lambda b, h: (b, h)),
        out_shape=jax.ShapeDtypeStruct((T, RET_V_W), bf16),
        scratch_shapes=[pltpu.VMEM((hp, N, RET_DK, RET_DV), f32), pltpu.VMEM((hp, N, RET_DK, RET_DV), bf16)],
        compiler_params=_params(("parallel", "parallel")),
        name="retention",
    )(proj, proj, proj, proj, dmat, xi_b, zeta_b, cdec_b)


KV_PER_STEP = SWA_KV_PER_STEP
Q_PER_KV = SWA_Q_HEADS // SWA_KV_HEADS
SWA_Q_STEP_W = KV_PER_STEP * Q_PER_KV * SWA_HEAD_DIM
assert KV_PER_STEP * SWA_HEAD_DIM == LANES and Q_PER_KV * SWA_HEAD_DIM == 2 * LANES


def _swa_kernel(sink_ref, bias_ref, q_ref, k_ref, v_ref, o_ref, klo, khi, vlo, vhi):
    S = q_ref.shape[0]
    C = SWA_BLOCK
    gp = pl.program_id(1)
    half = SWA_HEAD_DIM

    lane_s = lax.broadcasted_iota(jnp.int32, (S, LANES), 1)
    low_s = lane_s < half

    def prep(src_ref, lo_scr, hi_scr, scale, with_ones):
        w = src_ref[...].astype(f32) * scale
        r = pltpu.roll(w, half, axis=1)
        zeros = jnp.zeros((C, lo_scr.shape[2]), bf16)
        for t in range(KV_PER_STEP):
            lo_scr[t, 0:C, :] = zeros
            hi_scr[t, 0:C, :] = zeros
        lo_scr[0, C:C + S, 0:LANES] = jnp.where(low_s, w, 0.0).astype(bf16)
        hi_scr[0, C:C + S, 0:LANES] = jnp.where(low_s, 0.0, r).astype(bf16)
        lo_scr[1, C:C + S, 0:LANES] = jnp.where(low_s, r, 0.0).astype(bf16)
        hi_scr[1, C:C + S, 0:LANES] = jnp.where(low_s, 0.0, w).astype(bf16)
        if with_ones:
            for t in range(KV_PER_STEP):
                lo_scr[t, C:C + S, LANES:2 * LANES] = jnp.where(low_s, 1.0, 0.0).astype(bf16)
                hi_scr[t, C:C + S, LANES:2 * LANES] = jnp.where(low_s, 0.0, 1.0).astype(bf16)

    prep(k_ref, klo, khi, SWA_HEAD_DIM ** -0.5, False)
    prep(v_ref, vlo, vhi, 1.0, True)

    rows = 2 * C
    win = 2 * C
    first_pair = lax.broadcasted_iota(jnp.int32, (rows, 1), 0) < C
    low_o = lax.broadcasted_iota(jnp.int32, (rows, LANES), 1) < half
    nt = (((1,), (1,)), ((), ()))

    def softmax_parts(s, sink, bias):
        s = s + bias
        m = jnp.maximum(jnp.max(s, axis=-1, keepdims=True), sink)
        return jnp.exp(s - m).astype(bf16), jnp.exp(sink - m)

    def block(n, carry):
        r0 = pl.multiple_of(n * C, C)
        bias = bias_ref[jnp.minimum(n, 1)]
        for t in range(KV_PER_STEP):
            c0 = t * Q_PER_KV * SWA_HEAD_DIM
            hbase = gp * (KV_PER_STEP * Q_PER_KV) + t * Q_PER_KV
            qs = jnp.concatenate([q_ref[pl.ds(r0, C), c0:c0 + LANES],
                                  q_ref[pl.ds(r0, C), c0 + LANES:c0 + 2 * LANES]], axis=0)
            s_e = lax.dot_general(qs, klo[t, pl.ds(r0, win), :], nt, preferred_element_type=f32)
            s_o = lax.dot_general(qs, khi[t, pl.ds(r0, win), :], nt, preferred_element_type=f32)
            sink_e = jnp.where(first_pair, sink_ref[hbase + 0], sink_ref[hbase + 2])
            sink_o = jnp.where(first_pair, sink_ref[hbase + 1], sink_ref[hbase + 3])
            p_e, z_e = softmax_parts(s_e, sink_e, bias)
            p_o, z_o = softmax_parts(s_o, sink_o, bias)
            pv = (jnp.dot(p_e, vlo[t, pl.ds(r0, win), :], preferred_element_type=f32)
                  + jnp.dot(p_o, vhi[t, pl.ds(r0, win), :], preferred_element_type=f32))
            den = pv[:, LANES:2 * LANES] + jnp.where(low_o, z_e, z_o)
            o = pv[:, 0:LANES] / den
            o_ref[pl.ds(r0, C), c0:c0 + LANES] = o[0:C].astype(o_ref.dtype)
            o_ref[pl.ds(r0, C), c0 + LANES:c0 + 2 * LANES] = o[C:2 * C].astype(o_ref.dtype)
        return carry

    lax.fori_loop(0, S // C, block, 0, unroll=SWA_UNROLL)


def _swa_bias():
    C = SWA_BLOCK
    row_i = jnp.arange(2 * C)[:, None] % C
    col_j = jnp.arange(2 * C)[None, :]
    band = (col_j > row_i) & (col_j <= row_i + C)
    first = band & (col_j >= C)
    return jnp.where(jnp.stack([first, band]), 0.0, -jnp.inf).astype(f32)


def _swa(proj, sinks, batch, seq):
    T = proj.shape[0]
    steps = SWA_KV_HEADS // KV_PER_STEP
    k_scr = pltpu.VMEM((KV_PER_STEP, seq + SWA_BLOCK, LANES), bf16)
    v_scr = pltpu.VMEM((KV_PER_STEP, seq + SWA_BLOCK, 2 * LANES), bf16)
    bias = _swa_bias()
    return pl.pallas_call(
        _swa_kernel,
        grid=(batch, steps),
        in_specs=[
            pl.BlockSpec(memory_space=pltpu.SMEM),
            pl.BlockSpec(bias.shape, lambda b, g: (0, 0, 0)),
            pl.BlockSpec((seq, SWA_Q_STEP_W), lambda b, g: (b, COL_SQ // SWA_Q_STEP_W + g)),
            pl.BlockSpec((seq, LANES), lambda b, g: (b, COL_SK // LANES + g)),
            pl.BlockSpec((seq, LANES), lambda b, g: (b, COL_SV // LANES + g)),
        ],
        out_specs=pl.BlockSpec((seq, SWA_Q_STEP_W), lambda b, g: (b, g)),
        out_shape=jax.ShapeDtypeStruct((T, SWA_Q_W), bf16),
        scratch_shapes=[k_scr, k_scr, v_scr, v_scr],
        compiler_params=_params(("parallel", "parallel")),
        name="swa",
    )(sinks, bias, proj, proj, proj)


def _mix_kernel(ret_ref, swa_ref, *refs):
    n_gate = MIX_TN // MIX_GATE_W
    gr_refs, gs_refs = refs[:n_gate], refs[n_gate:2 * n_gate]
    wr_ref, ws_ref, wo_ref, x_ref, g_ref, o_ref = refs[2 * n_gate:]
    j = pl.program_id(1)

    @pl.when(j == 0)
    def _():
        o_ref[...] = jnp.zeros_like(o_ref)

    part = None
    for c in range(n_gate):
        cs = slice(c * MIX_GATE_W, (c + 1) * MIX_GATE_W)
        ret_out = jnp.dot(ret_ref[...], wr_ref[:, cs], preferred_element_type=f32)
        swa_out = jnp.dot(swa_ref[...], ws_ref[:, cs], preferred_element_type=f32)
        mixed = gr_refs[c][...].astype(f32) * ret_out + gs_refs[c][...].astype(f32) * swa_out
        d = jnp.dot(mixed.astype(bf16), wo_ref[cs, :], preferred_element_type=f32)
        part = d if part is None else part + d
    o_ref[...] += part

    @pl.when(j == pl.num_programs(1) - 1)
    def _():
        _residual_norm_rows(x_ref, o_ref, g_ref, o_ref, o_ref.shape[0])


def _mix(ret, swa, proj, w_ret_o, w_swa_o, w_out, x2, g):
    T = x2.shape[0]
    tm, tn = min(MIX_TM, T), MIX_TN
    nj = D_MODEL // tn
    gw = MIX_GATE_W
    n_gate = tn // gw

    def gate_specs(col0):
        return [pl.BlockSpec((tm, gw), functools.partial(lambda i, j, c: (i, col0 // gw + j * n_gate + c), c=c))
                for c in range(n_gate)]

    return pl.pallas_call(
        _mix_kernel,
        grid=(T // tm, nj),
        in_specs=[
            pl.BlockSpec((tm, RET_V_W), lambda i, j: (i, 0)),
            pl.BlockSpec((tm, SWA_Q_W), lambda i, j: (i, 0)),
            *gate_specs(COL_GR),
            *gate_specs(COL_GS),
            pl.BlockSpec((RET_V_W, tn), lambda i, j: (0, j)),
            pl.BlockSpec((SWA_Q_W, tn), lambda i, j: (0, j)),
            pl.BlockSpec((tn, D_MODEL), lambda i, j: (j, 0)),
            pl.BlockSpec((tm, D_MODEL), lambda i, j: (i, 0)),
            pl.BlockSpec((1, D_MODEL), lambda i, j: (0, 0)),
        ],
        out_specs=pl.BlockSpec((tm, D_MODEL), lambda i, j: (i, 0)),
        out_shape=jax.ShapeDtypeStruct((T, D_MODEL), f32),
        compiler_params=_params(("parallel", "arbitrary")),
        name="mix",
    )(ret, swa, *([proj] * (2 * n_gate)), w_ret_o, w_swa_o, w_out, x2, g)


def _gelu_tanh(x):
    return x * (0.5 * (1.0 + jnp.tanh(0.7978845608028654 * (x + 0.044715 * (x * x * x)))))


def _ffn_kernel(x_ref, halo_ref, gpre_ref, wv_ref, wg_ref, cwv_ref, cwg_ref, cbv_ref, cbg_ref, wd_ref, gpost_ref,
                o_ref, h_scr, *u_scrs, tiles_per_seq):
    i = pl.program_id(0)
    j = pl.program_id(1)
    tm = x_ref.shape[0]
    tn = wv_ref.shape[1]
    H = FFN_HALO

    @pl.when(j == 0)
    def _():
        halo = jnp.where(i % tiles_per_seq == 0, 0.0, _rmsnorm(halo_ref[...], gpre_ref[...]))
        h_scr[0:H, :] = halo.astype(bf16)
        _norm_rows_to(x_ref, gpre_ref, h_scr, H, tm)
        o_ref[...] = jnp.zeros_like(o_ref)

    def conv(u_scr, row0, nrows, u_cols, w_cols, cw_ref, cb_ref):
        y = cb_ref[:, w_cols]
        for kk in range(CONV_WIDTH):
            y = y + u_scr[pl.ds(H - (CONV_WIDTH - 1) + kk + row0, nrows), u_cols] * cw_ref[kk:kk + 1, w_cols]
        return y

    h = h_scr[...]
    hm = tm // FFN_ROW_SPLIT
    parts = [None] * FFN_ROW_SPLIT
    uv = slice(0, MXU_COLS)
    ug = slice(MXU_COLS, 2 * MXU_COLS)
    n_chunks = len(u_scrs)
    for c, u_scr in enumerate(u_scrs):
        cs = slice(c * MXU_COLS, (c + 1) * MXU_COLS)
        if c < n_chunks - 1:
            u_scr[:, uv] = jnp.dot(h, wv_ref[:, cs], preferred_element_type=f32)
            u_scr[:, ug] = jnp.dot(h, wg_ref[:, cs], preferred_element_type=f32)
        else:
            wv, wg = wv_ref[:, cs], wg_ref[:, cs]
            r0 = 0
            for m in range(FFN_ROW_SPLIT):
                r1 = H + (m + 1) * hm
                u_scr[r0:r1, uv] = jnp.dot(h_scr[r0:r1, :], wv, preferred_element_type=f32)
                u_scr[r0:r1, ug] = jnp.dot(h_scr[r0:r1, :], wg, preferred_element_type=f32)
                r0 = r1
        for m in range(FFN_ROW_SPLIT):
            val = conv(u_scr, m * hm, hm, uv, cs, cwv_ref, cbv_ref)
            gate = conv(u_scr, m * hm, hm, ug, cs, cwg_ref, cbg_ref)
            a = (_gelu_tanh(gate) * val).astype(bf16)
            d = jnp.dot(a, wd_ref[cs, :], preferred_element_type=f32)
            parts[m] = d if parts[m] is None else parts[m] + d
    for m in range(FFN_ROW_SPLIT):
        o_ref[m * hm:(m + 1) * hm, :] += parts[m]

    @pl.when(j == pl.num_programs(1) - 1)
    def _():
        _residual_norm_rows(x_ref, o_ref, gpost_ref, o_ref, tm)


def _ffn(x1, g_pre, w_up, conv_w, conv_b, w_down, g_post, seq):
    T = x1.shape[0]
    tm, tn = min(FFN_TM, seq), FFN_TN
    nj = D_FF // tn
    hb = tm // FFN_HALO
    kern = functools.partial(_ffn_kernel, tiles_per_seq=seq // tm)
    return pl.pallas_call(
        kern,
        grid=(T // tm, nj),
        in_specs=[
            pl.BlockSpec((tm, D_MODEL), lambda i, j: (i, 0)),
            pl.BlockSpec((FFN_HALO, D_MODEL), lambda i, j: (jnp.maximum(i * hb - 1, 0), 0)),
            pl.BlockSpec((1, D_MODEL), lambda i, j: (0, 0)),
            pl.BlockSpec((D_MODEL, tn), lambda i, j: (0, j)),
            pl.BlockSpec((D_MODEL, tn), lambda i, j: (0, nj + j)),
            pl.BlockSpec((CONV_WIDTH, tn), lambda i, j: (0, j)),
            pl.BlockSpec((CONV_WIDTH, tn), lambda i, j: (0, nj + j)),
            pl.BlockSpec((1, tn), lambda i, j: (0, j)),
            pl.BlockSpec((1, tn), lambda i, j: (0, nj + j)),
            pl.BlockSpec((tn, D_MODEL), lambda i, j: (j, 0)),
            pl.BlockSpec((1, D_MODEL), lambda i, j: (0, 0)),
        ],
        out_specs=pl.BlockSpec((tm, D_MODEL), lambda i, j: (i, 0)),
        out_shape=jax.ShapeDtypeStruct((T, D_MODEL), f32),
        scratch_shapes=[
            pltpu.VMEM((tm + FFN_HALO, D_MODEL), bf16),
        ] + [pltpu.VMEM((tm + FFN_HALO, 2 * MXU_COLS), f32)] * (tn // MXU_COLS),
        compiler_params=_params(("parallel", "arbitrary")),
        name="ffn",
    )(x1, x1, g_pre, w_up, w_up, conv_w, conv_w, conv_b, conv_b, w_down, g_post)


def _rotary_tables(seq):
    d = RET_DK
    inv = 1.0 / (ROPE_BASE ** (jnp.arange(0, d, 2, dtype=f32) / d))
    ang = jnp.arange(seq, dtype=jnp.int32).astype(f32)[:, None] * inv[None, :]
    cos, sin = jnp.cos(ang), jnp.sin(ang)
    tab = jnp.concatenate([cos, cos, -sin, sin], axis=-1)
    return jnp.stack([tab, tab * (d ** -0.5)])


def kernel(x, g_pre_mix, w_in, w_ret_o, w_swa_o, w_out, swa_sinks, g_post_mix, g_pre_ffn, w_up, conv_w, conv_b,
           w_down, g_post_ffn):
    B, S, D = x.shape
    depth = w_in.shape[0]
    assert D == D_MODEL and w_in.shape[1:] == (D_MODEL, IN_WIDTH) and w_up.shape[1:] == (D_MODEL, 2 * D_FF)
    assert S % RET_CHUNK == 0 and S % SWA_BLOCK == 0
    assert S % min(INPROJ_TM, S) == 0 and S % min(FFN_TM, S) == 0 and (B * S) % min(MIX_TM, B * S) == 0
    rot = _rotary_tables(S)
    x2 = x.reshape(B * S, D)
    for l in range(depth):
        proj = _inproj(x2, g_pre_mix[l][None], w_in[l].astype(bf16), rot, S)
        ret = _retention(proj, B, S)
        swa = _swa(proj, swa_sinks[l].astype(f32), B, S)
        x2 = _mix(ret, swa, proj, w_ret_o[l].astype(bf16), w_swa_o[l].astype(bf16), w_out[l].astype(bf16), x2,
                  g_post_mix[l][None])
        x2 = _ffn(x2, g_pre_ffn[l][None], w_up[l].astype(bf16), conv_w[l], conv_b[l][None], w_down[l].astype(bf16),
                  g_post_ffn[l][None], S)
    return x2.reshape(B, S, D)
```

```python
import functools

import jax
import jax.numpy as jnp
from jax import lax
from jax.experimental import pallas as pl
from jax.experimental.pallas import tpu as pltpu

D_MODEL = 2048
RET_HEADS = 8
RET_DK = 128
RET_DV = 256
RET_CHUNK = 256
ROPE_BASE = 10000.0
SWA_Q_HEADS = 16
SWA_KV_HEADS = 4
SWA_HEAD_DIM = 64
SWA_BLOCK = 128
D_FF = 5632
CONV_WIDTH = 3
RMS_EPS = 1e-6

RET_QK_W = RET_HEADS * RET_DK
RET_V_W = RET_HEADS * RET_DV
SWA_Q_W = SWA_Q_HEADS * SWA_HEAD_DIM
SWA_KV_W = SWA_KV_HEADS * SWA_HEAD_DIM

COL_RQ = 0
COL_RK = COL_RQ + RET_QK_W
COL_RV = COL_RK + RET_QK_W
COL_RG = COL_RV + RET_V_W
COL_SQ = COL_RG + RET_V_W
COL_SK = COL_SQ + SWA_Q_W
COL_SV = COL_SK + SWA_KV_W
COL_GR = COL_SV + SWA_KV_W
COL_GS = COL_GR + D_MODEL
IN_WIDTH = COL_GS + D_MODEL

V7X_VMEM_LIMIT_BYTES = 60 * 1024 * 1024
LANES = 128
BF16_SUBLANES = 16
MXU_COLS = 256

NORM_ROWS = 32
NORM_UNROLL = 4
INPROJ_TM = 1024
INPROJ_TN = 1024
INPROJ_ROW_SPLIT = 4
MIX_TM = 512
MIX_TN = 1024
MIX_GATE_W = 512
FFN_TM = 1024
FFN_TN = 512
FFN_HALO = BF16_SUBLANES
FFN_ROW_SPLIT = 4
RET_HEADS_PER_STEP = 2
SWA_KV_PER_STEP = 2
SWA_UNROLL = 8

f32 = jnp.float32
bf16 = jnp.bfloat16


def _params(semantics):
    return pltpu.CompilerParams(dimension_semantics=semantics, vmem_limit_bytes=V7X_VMEM_LIMIT_BYTES)


def _rmsnorm(x, g):
    ms = jnp.mean(x * x, axis=-1, keepdims=True)
    return x * lax.rsqrt(ms + RMS_EPS) * g


def _norm_rows_to(x_ref, g_ref, dst_ref, dst_off, rows):
    g = g_ref[...]
    group = NORM_ROWS * NORM_UNROLL

    def step(c, carry):
        base = pl.multiple_of(c * group, group)
        for k in range(NORM_UNROLL):
            r = base + k * NORM_ROWS
            dst_ref[pl.ds(dst_off + r, NORM_ROWS), 0:x_ref.shape[1]] = (
                _rmsnorm(x_ref[pl.ds(r, NORM_ROWS), :], g).astype(dst_ref.dtype))
        return carry

    lax.fori_loop(0, rows // group, step, 0)


def _residual_norm_rows(x_ref, y_ref, g_ref, o_ref, rows):
    g = g_ref[...]
    group = NORM_ROWS * NORM_UNROLL

    def step(c, carry):
        base = pl.multiple_of(c * group, group)
        rows_k = [pl.ds(base + k * NORM_ROWS, NORM_ROWS) for k in range(NORM_UNROLL)]
        scales = []
        for r in rows_k:
            y = y_ref[r, :]
            scales.append(lax.rsqrt(jnp.mean(y * y, axis=-1, keepdims=True) + RMS_EPS))
        for r, s in zip(rows_k, scales):
            o_ref[r, :] = x_ref[r, :] + y_ref[r, :] * s * g
        return carry

    lax.fori_loop(0, rows // group, step, 0)


def _inproj_tile_kinds():
    starts = ((COL_RQ, "rot"), (COL_RK, "rot"), (COL_RV, "cast"), (COL_RG, "silu"), (COL_SQ, "cast"),
              (COL_GR, "sig"))
    chunk_kinds = [[k for c0, k in starts if c0 <= col][-1] for col in range(0, IN_WIDTH, MXU_COLS)]
    per_tile = INPROJ_TN // MXU_COLS
    return [tuple(chunk_kinds[t:t + per_tile]) for t in range(0, len(chunk_kinds), per_tile)]


_INPROJ_TILE_KINDS = _inproj_tile_kinds()
_T_RK = COL_RK // INPROJ_TN
assert COL_RK % INPROJ_TN == 0 and COL_RV % INPROJ_TN == 0


def _inproj_kernel(x_ref, g_ref, w_ref, rot_ref, o_ref, h_scr):
    j = pl.program_id(1)
    tm = x_ref.shape[0]

    @pl.when(j == 0)
    def _():
        _norm_rows_to(x_ref, g_ref, h_scr, 0, tm)

    def rotary(acc, rows):
        cos = rot_ref[rows, 0:RET_DK]
        sin = rot_ref[rows, RET_DK:2 * RET_DK]
        heads = []
        for hh in range(acc.shape[1] // RET_DK):
            xh = acc[:, hh * RET_DK:(hh + 1) * RET_DK]
            heads.append(xh * cos + pltpu.roll(xh, RET_DK // 2, axis=1) * sin)
        return jnp.concatenate(heads, axis=1)

    epilogues = {
        "rot": rotary,
        "cast": lambda acc, rows: acc,
        "silu": lambda acc, rows: acc * jax.nn.sigmoid(acc),
        "sig": lambda acc, rows: jax.nn.sigmoid(acc),
    }

    def tile(kinds):
        hm = tm // INPROJ_ROW_SPLIT
        for c, kind in enumerate(kinds):
            cs = slice(c * MXU_COLS, (c + 1) * MXU_COLS)
            w = w_ref[:, cs]
            for m in range(INPROJ_ROW_SPLIT):
                rows = slice(m * hm, (m + 1) * hm)
                acc = jnp.dot(h_scr[rows, 0:D_MODEL], w, preferred_element_type=f32)
                o_ref[rows, cs] = epilogues[kind](acc, rows).astype(o_ref.dtype)

    for kinds in sorted(set(_INPROJ_TILE_KINDS)):
        tiles = [t for t, k in enumerate(_INPROJ_TILE_KINDS) if k == kinds]
        cond = functools.reduce(lambda a, b: a | b, [j == t for t in tiles])
        pl.when(cond)(functools.partial(tile, kinds))


def _inproj(x2, g, w_in, rot, seq):
    T = x2.shape[0]
    tm, tn = INPROJ_TM, INPROJ_TN
    tm = min(tm, seq)
    spt = seq // tm
    grid = (T // tm, len(_INPROJ_TILE_KINDS))
    return pl.pallas_call(
        _inproj_kernel,
        grid=grid,
        in_specs=[
            pl.BlockSpec((tm, D_MODEL), lambda i, j: (i, 0)),
            pl.BlockSpec((1, D_MODEL), lambda i, j: (0, 0)),
            pl.BlockSpec((D_MODEL, tn), lambda i, j: (0, j)),
            pl.BlockSpec((None, tm, 2 * RET_DK), lambda i, j: (jnp.where(j >= _T_RK, 1, 0), i % spt, 0)),
        ],
        out_specs=pl.BlockSpec((tm, tn), lambda i, j: (i, j)),
        out_shape=jax.ShapeDtypeStruct((T, IN_WIDTH), bf16),
        scratch_shapes=[pltpu.VMEM((tm, D_MODEL + LANES), bf16)],
        compiler_params=_params(("parallel", "arbitrary")),
        name="inproj",
    )(x2, g, w_in, rot)


def _retention_kernel(q_ref, k_ref, v_ref, gate_ref, dmat_ref, xi_ref, zeta_ref, cdec_ref, o_ref, kv_scr, st_scr):
    S = q_ref.shape[0]
    C = RET_CHUNK
    N = S // C
    heads = range(RET_HEADS_PER_STEP)
    qk = [slice(hh * RET_DK, (hh + 1) * RET_DK) for hh in heads]
    vv = [slice(hh * RET_DV, (hh + 1) * RET_DV) for hh in heads]
    q3 = [q_ref[:, qk[hh]].reshape(N, C, RET_DK) for hh in heads]
    k3 = [k_ref[:, qk[hh]].reshape(N, C, RET_DK) for hh in heads]
    v3 = [v_ref[:, vv[hh]].reshape(N, C, RET_DV) for hh in heads]

    for hh in heads:
        kz = (k3[hh].astype(f32) * zeta_ref[hh][None]).astype(bf16)
        kv_scr[hh] = jnp.einsum("ndk,nkv->ndv", jnp.swapaxes(kz, 1, 2), v3[hh], preferred_element_type=f32)

    cdec = [cdec_ref[hh] for hh in heads]

    def scan(n, states):
        for hh in heads:
            st_scr[hh, n] = states[hh].astype(bf16)
        return tuple(states[hh] * cdec[hh] + kv_scr[hh, n] for hh in heads)

    lax.fori_loop(0, N, scan, tuple(jnp.zeros((RET_DK, RET_DV), f32) for _ in heads))

    for hh in heads:
        sc = jnp.einsum("nqd,nkd->nqk", q3[hh], k3[hh], preferred_element_type=f32) * dmat_ref[hh][None]
        inner = jnp.einsum("nqk,nkv->nqv", sc.astype(bf16), v3[hh], preferred_element_type=f32)
        cross = jnp.einsum("nqd,ndv->nqv", q3[hh], st_scr[hh], preferred_element_type=f32) * xi_ref[hh][None]
        o = inner + cross
        o = o * lax.rsqrt(jnp.mean(o * o, axis=-1, keepdims=True) + RMS_EPS)
        o = gate_ref[:, vv[hh]].astype(f32).reshape(N, C, RET_DV) * o
        o_ref[:, vv[hh]] = o.reshape(S, RET_DV).astype(o_ref.dtype)


def _retention_tables():
    C = RET_CHUNK
    log_gamma = jnp.log(1.0 - 2.0 ** (-5.0 - jnp.arange(RET_HEADS, dtype=f32)))
    idx = jnp.arange(C, dtype=f32)
    rel = idx[:, None] - idx[None, :]
    dmat = jnp.where(rel[None] >= 0, jnp.exp(log_gamma[:, None, None] * jnp.maximum(rel, 0.0)[None]), 0.0)
    xi = jnp.exp(log_gamma[:, None] * (idx + 1.0))
    zeta = jnp.exp(log_gamma[:, None] * (C - 1.0 - idx))
    cdec = jnp.exp(log_gamma * C)
    xi_b = jnp.broadcast_to(xi[:, :, None], (RET_HEADS, C, RET_DV))
    zeta_b = jnp.broadcast_to(zeta[:, :, None], (RET_HEADS, C, RET_DK))
    cdec_b = jnp.broadcast_to(cdec[:, None, None], (RET_HEADS, 1, RET_DV))
    return dmat.astype(f32), xi_b.astype(f32), zeta_b.astype(f32), cdec_b.astype(f32)


def _retention(proj, batch, seq):
    T = proj.shape[0]
    C = RET_CHUNK
    N = seq // C
    dmat, xi_b, zeta_b, cdec_b = _retention_tables()
    hp = RET_HEADS_PER_STEP
    qk_w, v_w = hp * RET_DK, hp * RET_DV
    return pl.pallas_call(
        _retention_kernel,
        grid=(batch, RET_HEADS // hp),
        in_specs=[
            pl.BlockSpec((seq, qk_w), lambda b, h: (b, COL_RQ // qk_w + h)),
            pl.BlockSpec((seq, qk_w), lambda b, h: (b, COL_RK // qk_w + h)),
            pl.BlockSpec((seq, v_w), lambda b, h: (b, COL_RV // v_w + h)),
            pl.BlockSpec((seq, v_w), lambda b, h: (b, COL_RG // v_w + h)),
            pl.BlockSpec((hp, C, C), lambda b, h: (h, 0, 0)),
            pl.BlockSpec((hp, C, RET_DV), lambda b, h: (h, 0, 0)),
            pl.BlockSpec((hp, C, RET_DK), lambda b, h: (h, 0, 0)),
            pl.BlockSpec((hp, 1, RET_DV), lambda b, h: (h, 0, 0)),
        ],
        out_specs=pl.BlockSpec((seq, v_w), lambda b, h: (b, h)),
        out_shape=jax.ShapeDtypeStruct((T, RET_V_W), bf16),
        scratch_shapes=[pltpu.VMEM((hp, N, RET_DK, RET_DV), f32), pltpu.VMEM((hp, N, RET_DK, RET_DV), bf16)],
        compiler_params=_params(("parallel", "parallel")),
        name="retention",
    )(proj, proj, proj, proj, dmat, xi_b, zeta_b, cdec_b)


KV_PER_STEP = SWA_KV_PER_STEP
Q_PER_KV = SWA_Q_HEADS // SWA_KV_HEADS
SWA_Q_STEP_W = KV_PER_STEP * Q_PER_KV * SWA_HEAD_DIM
assert KV_PER_STEP * SWA_HEAD_DIM == LANES and Q_PER_KV * SWA_HEAD_DIM == 2 * LANES


def _swa_kernel(sink_ref, bias_ref, q_ref, k_ref, v_ref, o_ref, klo, khi, vlo, vhi):
    S = q_ref.shape[0]
    C = SWA_BLOCK
    gp = pl.program_id(1)
    half = SWA_HEAD_DIM

    lane_s = lax.broadcasted_iota(jnp.int32, (S, LANES), 1)
    low_s = lane_s < half

    def prep(src_ref, lo_scr, hi_scr, scale, with_ones):
        w = src_ref[...].astype(f32) * scale
        r = pltpu.roll(w, half, axis=1)
        zeros = jnp.zeros((C, lo_scr.shape[2]), bf16)
        for t in range(KV_PER_STEP):
            lo_scr[t, 0:C, :] = zeros
            hi_scr[t, 0:C, :] = zeros
        lo_scr[0, C:C + S, 0:LANES] = jnp.where(low_s, w, 0.0).astype(bf16)
        hi_scr[0, C:C + S, 0:LANES] = jnp.where(low_s, 0.0, r).astype(bf16)
        lo_scr[1, C:C + S, 0:LANES] = jnp.where(low_s, r, 0.0).astype(bf16)
        hi_scr[1, C:C + S, 0:LANES] = jnp.where(low_s, 0.0, w).astype(bf16)
        if with_ones:
            for t in range(KV_PER_STEP):
                lo_scr[t, C:C + S, LANES:2 * LANES] = jnp.where(low_s, 1.0, 0.0).astype(bf16)
                hi_scr[t, C:C + S, LANES:2 * LANES] = jnp.where(low_s, 0.0, 1.0).astype(bf16)

    prep(k_ref, klo, khi, SWA_HEAD_DIM ** -0.5, False)
    prep(v_ref, vlo, vhi, 1.0, True)

    rows = 2 * C
    win = 2 * C
    first_pair = lax.broadcasted_iota(jnp.int32, (rows, 1), 0) < C
    low_o = lax.broadcasted_iota(jnp.int32, (rows, LANES), 1) < half
    nt = (((1,), (1,)), ((), ()))

    def softmax_parts(s, sink, bias):
        s = s + bias
        m = jnp.maximum(jnp.max(s, axis=-1, keepdims=True), sink)
        return jnp.exp(s - m).astype(bf16), jnp.exp(sink - m)

    def block(n, carry):
        r0 = pl.multiple_of(n * C, C)
        bias = bias_ref[jnp.minimum(n, 1)]
        for t in range(KV_PER_STEP):
            c0 = t * Q_PER_KV * SWA_HEAD_DIM
            hbase = gp * (KV_PER_STEP * Q_PER_KV) + t * Q_PER_KV
            qs = jnp.concatenate([q_ref[pl.ds(r0, C), c0:c0 + LANES],
                                  q_ref[pl.ds(r0, C), c0 + LANES:c0 + 2 * LANES]], axis=0)
            s_e = lax.dot_general(qs, klo[t, pl.ds(r0, win), :], nt, preferred_element_type=f32)
            s_o = lax.dot_general(qs, khi[t, pl.ds(r0, win), :], nt, preferred_element_type=f32)
            sink_e = jnp.where(first_pair, sink_ref[hbase + 0], sink_ref[hbase + 2])
            sink_o = jnp.where(first_pair, sink_ref[hbase + 1], sink_ref[hbase + 3])
            p_e, z_e = softmax_parts(s_e, sink_e, bias)
            p_o, z_o = softmax_parts(s_o, sink_o, bias)
            pv = (jnp.dot(p_e, vlo[t, pl.ds(r0, win), :], preferred_element_type=f32)
                  + jnp.dot(p_o, vhi[t, pl.ds(r0, win), :], preferred_element_type=f32))
            den = pv[:, LANES:2 * LANES] + jnp.where(low_o, z_e, z_o)
            o = pv[:, 0:LANES] / den
            o_ref[pl.ds(r0, C), c0:c0 + LANES] = o[0:C].astype(o_ref.dtype)
            o_ref[pl.ds(r0, C), c0 + LANES:c0 + 2 * LANES] = o[C:2 * C].astype(o_ref.dtype)
        return carry

    lax.fori_loop(0, S // C, block, 0, unroll=SWA_UNROLL)


def _swa_bias():
    C = SWA_BLOCK
    row_i = jnp.arange(2 * C)[:, None] % C
    col_j = jnp.arange(2 * C)[None, :]
    band = (col_j > row_i) & (col_j <= row_i + C)
    first = band & (col_j >= C)
    return jnp.where(jnp.stack([first, band]), 0.0, -jnp.inf).astype(f32)


def _swa(proj, sinks, batch, seq):
    T = proj.shape[0]
    steps = SWA_KV_HEADS // KV_PER_STEP
    k_scr = pltpu.VMEM((KV_PER_STEP, seq + SWA_BLOCK, LANES), bf16)
    v_scr = pltpu.VMEM((KV_PER_STEP, seq + SWA_BLOCK, 2 * LANES), bf16)
    bias = _swa_bias()
    return pl.pallas_call(
        _swa_kernel,
        grid=(batch, steps),
        in_specs=[
            pl.BlockSpec(memory_space=pltpu.SMEM),
            pl.BlockSpec(bias.shape, lambda b, g: (0, 0, 0)),
            pl.BlockSpec((seq, SWA_Q_STEP_W), lambda b, g: (b, COL_SQ // SWA_Q_STEP_W + g)),
            pl.BlockSpec((seq, LANES), lambda b, g: (b, COL_SK // LANES + g)),
            pl.BlockSpec((seq, LANES), lambda b, g: (b, COL_SV // LANES + g)),
        ],
        out_specs=pl.BlockSpec((seq, SWA_Q_STEP_W), lambda b, g: (b, g)),
        out_shape=jax.ShapeDtypeStruct((T, SWA_Q_W), bf16),
        scratch_shapes=[k_scr, k_scr, v_scr, v_scr],
        compiler_params=_params(("parallel", "parallel")),
        name="swa",
    )(sinks, bias, proj, proj, proj)


def _mix_kernel(ret_ref, swa_ref, *refs):
    n_gate = MIX_TN // MIX_GATE_W
    gr_refs, gs_refs = refs[:n_gate], refs[n_gate:2 * n_gate]
    wr_ref, ws_ref, wo_ref, x_ref, g_ref, o_ref = refs[2 * n_gate:]
    j = pl.program_id(1)

    @pl.when(j == 0)
    def _():
        o_ref[...] = jnp.zeros_like(o_ref)

    part = None
    for c in range(n_gate):
        cs = slice(c * MIX_GATE_W, (c + 1) * MIX_GATE_W)
        ret_out = jnp.dot(ret_ref[...], wr_ref[:, cs], preferred_element_type=f32)
        swa_out = jnp.dot(swa_ref[...], ws_ref[:, cs], preferred_element_type=f32)
        mixed = gr_refs[c][...].astype(f32) * ret_out + gs_refs[c][...].astype(f32) * swa_out
        d = jnp.dot(mixed.astype(bf16), wo_ref[cs, :], preferred_element_type=f32)
        part = d if part is None else part + d
    o_ref[...] += part

    @pl.when(j == pl.num_programs(1) - 1)
    def _():
        _residual_norm_rows(x_ref, o_ref, g_ref, o_ref, o_ref.shape[0])


def _mix(ret, swa, proj, w_ret_o, w_swa_o, w_out, x2, g):
    T = x2.shape[0]
    tm, tn = min(MIX_TM, T), MIX_TN
    nj = D_MODEL // tn
    gw = MIX_GATE_W
    n_gate = tn // gw

    def gate_specs(col0):
        return [pl.BlockSpec((tm, gw), functools.partial(lambda i, j, c: (i, col0 // gw + j * n_gate + c), c=c))
                for c in range(n_gate)]

    return pl.pallas_call(
        _mix_kernel,
        grid=(T // tm, nj),
        in_specs=[
            pl.BlockSpec((tm, RET_V_W), lambda i, j: (i, 0)),
            pl.BlockSpec((tm, SWA_Q_W), lambda i, j: (i, 0)),
            *gate_specs(COL_GR),
            *gate_specs(COL_GS),
            pl.BlockSpec((RET_V_W, tn), lambda i, j: (0, j)),
            pl.BlockSpec((SWA_Q_W, tn), lambda i, j: (0, j)),
            pl.BlockSpec((tn, D_MODEL), lambda i, j: (j, 0)),
            pl.BlockSpec((tm, D_MODEL), lambda i, j: (i, 0)),
            pl.BlockSpec((1, D_MODEL), lambda i, j: (0, 0)),
        ],
        out_specs=pl.BlockSpec((tm, D_MODEL), lambda i, j: (i, 0)),
        out_shape=jax.ShapeDtypeStruct((T, D_MODEL), f32),
        compiler_params=_params(("parallel", "arbitrary")),
        name="mix",
    )(ret, swa, *([proj] * (2 * n_gate)), w_ret_o, w_swa_o, w_out, x2, g)


def _gelu_tanh(x):
    return x * (0.5 * (1.0 + jnp.tanh(0.7978845608028654 * (x + 0.044715 * (x * x * x)))))


def _ffn_kernel(x_ref, halo_ref, gpre_ref, wv_ref, wg_ref, cwv_ref, cwg_ref, cbv_ref, cbg_ref, wd_ref, gpost_ref,
                o_ref, h_scr, *u_scrs, tiles_per_seq):
    i = pl.program_id(0)
    j = pl.program_id(1)
    tm = x_ref.shape[0]
    tn = wv_ref.shape[1]
    H = FFN_HALO

    @pl.when(j == 0)
    def _():
        halo = jnp.where(i % tiles_per_seq == 0, 0.0, _rmsnorm(halo_ref[...], gpre_ref[...]))
        h_scr[0:H, :] = halo.astype(bf16)
        _norm_rows_to(x_ref, gpre_ref, h_scr, H, tm)
        o_ref[...] = jnp.zeros_like(o_ref)

    def conv(u_scr, row0, nrows, u_cols, w_cols, cw_ref, cb_ref):
        y = cb_ref[:, w_cols]
        for kk in range(CONV_WIDTH):
            y = y + u_scr[pl.ds(H - (CONV_WIDTH - 1) + kk + row0, nrows), u_cols] * cw_ref[kk:kk + 1, w_cols]
        return y

    h = h_scr[...]
    hm = tm // FFN_ROW_SPLIT
    parts = [None] * FFN_ROW_SPLIT
    uv = slice(0, MXU_COLS)
    ug = slice(MXU_COLS, 2 * MXU_COLS)
    n_chunks = len(u_scrs)
    for c, u_scr in enumerate(u_scrs):
        cs = slice(c * MXU_COLS, (c + 1) * MXU_COLS)
        if c < n_chunks - 1:
            u_scr[:, uv] = jnp.dot(h, wv_ref[:, cs], preferred_element_type=f32)
            u_scr[:, ug] = jnp.dot(h, wg_ref[:, cs], preferred_element_type=f32)
        else:
            wv, wg = wv_ref[:, cs], wg_ref[:, cs]
            r0 = 0
            for m in range(FFN_ROW_SPLIT):
                r1 = H + (m + 1) * hm
                u_scr[r0:r1, uv] = jnp.dot(h_scr[r0:r1, :], wv, preferred_element_type=f32)
                u_scr[r0:r1, ug] = jnp.dot(h_scr[r0:r1, :], wg, preferred_element_type=f32)
                r0 = r1
        for m in range(FFN_ROW_SPLIT):
            val = conv(u_scr, m * hm, hm, uv, cs, cwv_ref, cbv_ref)
            gate = conv(u_scr, m * hm, hm, ug, cs, cwg_ref, cbg_ref)
            a = (_gelu_tanh(gate) * val).astype(bf16)
            d = jnp.dot(a, wd_ref[cs, :], preferred_element_type=f32)
            parts[m] = d if parts[m] is None else parts[m] + d
    for m in range(FFN_ROW_SPLIT):
        o_ref[m * hm:(m + 1) * hm, :] += parts[m]

    @pl.when(j == pl.num_programs(1) - 1)
    def _():
        _residual_norm_rows(x_ref, o_ref, gpost_ref, o_ref, tm)


def _ffn(x1, g_pre, w_up, conv_w, conv_b, w_down, g_post, seq):
    T = x1.shape[0]
    tm, tn = min(FFN_TM, seq), FFN_TN
    nj = D_FF // tn
    hb = tm // FFN_HALO
    kern = functools.partial(_ffn_kernel, tiles_per_seq=seq // tm)
    return pl.pallas_call(
        kern,
        grid=(T // tm, nj),
        in_specs=[
            pl.BlockSpec((tm, D_MODEL), lambda i, j: (i, 0)),
            pl.BlockSpec((FFN_HALO, D_MODEL), lambda i, j: (jnp.maximum(i * hb - 1, 0), 0)),
            pl.BlockSpec((1, D_MODEL), lambda i, j: (0, 0)),
            pl.BlockSpec((D_MODEL, tn), lambda i, j: (0, j)),
            pl.BlockSpec((D_MODEL, tn), lambda i, j: (0, nj + j)),
            pl.BlockSpec((CONV_WIDTH, tn), lambda i, j: (0, j)),
            pl.BlockSpec((CONV_WIDTH, tn), lambda i, j: (0, nj + j)),
            pl.BlockSpec((1, tn), lambda i, j: (0, j)),
            pl.BlockSpec((1, tn), lambda i, j: (0, nj + j)),
            pl.BlockSpec((tn, D_MODEL), lambda i, j: (j, 0)),
            pl.BlockSpec((1, D_MODEL), lambda i, j: (0, 0)),
        ],
        out_specs=pl.BlockSpec((tm, D_MODEL), lambda i, j: (i, 0)),
        out_shape=jax.ShapeDtypeStruct((T, D_MODEL), f32),
        scratch_shapes=[
            pltpu.VMEM((tm + FFN_HALO, D_MODEL), bf16),
        ] + [pltpu.VMEM((tm + FFN_HALO, 2 * MXU_COLS), f32)] * (tn // MXU_COLS),
        compiler_params=_params(("parallel", "arbitrary")),
        name="ffn",
    )(x1, x1, g_pre, w_up, w_up, conv_w, conv_w, conv_b, conv_b, w_down, g_post)


def _rotary_tables(seq):
    d = RET_DK
    inv = 1.0 / (ROPE_BASE ** (jnp.arange(0, d, 2, dtype=f32) / d))
    ang = jnp.arange(seq, dtype=jnp.int32).astype(f32)[:, None] * inv[None, :]
    cos, sin = jnp.cos(ang), jnp.sin(ang)
    tab = jnp.concatenate([cos, cos, -sin, sin], axis=-1)
    return jnp.stack([tab, tab * (d ** -0.5)])


def kernel(x, g_pre_mix, w_in, w_ret_o, w_swa_o, w_out, swa_sinks, g_post_mix, g_pre_ffn, w_up, conv_w, conv_b,
           w_down, g_post_ffn):
    B, S, D = x.shape
    depth = w_in.shape[0]
    assert D == D_MODEL and w_in.shape[1:] == (D_MODEL, IN_WIDTH) and w_up.shape[1:] == (D_MODEL, 2 * D_FF)
    assert S % RET_CHUNK == 0 and S % SWA_BLOCK == 0
    assert S % min(INPROJ_TM, S) == 0 and S % min(FFN_TM, S) == 0 and (B * S) % min(MIX_TM, B * S) == 0
    rot = _rotary_tables(S)
    x2 = x.reshape(B * S, D)
    for l in range(depth):
        proj = _inproj(x2, g_pre_mix[l][None], w_in[l].astype(bf16), rot, S)
        ret = _retention(proj, B, S)
        swa = _swa(proj, swa_sinks[l].astype(f32), B, S)
        x2 = _mix(ret, swa, proj, w_ret_o[l].astype(bf16), w_swa_o[l].astype(bf16), w_out[l].astype(bf16), x2,
                  g_post_mix[l][None])
        x2 = _ffn(x2, g_pre_ffn[l][None], w_up[l].astype(bf16), conv_w[l], conv_b[l][None], w_down[l].astype(bf16),
                  g_post_ffn[l][None], S)
    return x2.reshape(B, S, D)
```

```python
import functools

import jax
import jax.numpy as jnp
from jax import lax
from jax.experimental import pallas as pl
from jax.experimental.pallas import tpu as pltpu

D_MODEL = 2048
RET_HEADS = 8
RET_DK = 128
RET_DV = 256
RET_CHUNK = 256
ROPE_BASE = 10000.0
SWA_Q_HEADS = 16
SWA_KV_HEADS = 4
SWA_HEAD_DIM = 64
SWA_BLOCK = 128
D_FF = 5632
CONV_WIDTH = 3
RMS_EPS = 1e-6

RET_QK_W = RET_HEADS * RET_DK
RET_V_W = RET_HEADS * RET_DV
SWA_Q_W = SWA_Q_HEADS * SWA_HEAD_DIM
SWA_KV_W = SWA_KV_HEADS * SWA_HEAD_DIM

COL_RQ = 0
COL_RK = COL_RQ + RET_QK_W
COL_RV = COL_RK + RET_QK_W
COL_RG = COL_RV + RET_V_W
COL_SQ = COL_RG + RET_V_W
COL_SK = COL_SQ + SWA_Q_W
COL_SV = COL_SK + SWA_KV_W
COL_GR = COL_SV + SWA_KV_W
COL_GS = COL_GR + D_MODEL
IN_WIDTH = COL_GS + D_MODEL

V7X_VMEM_LIMIT_BYTES = 60 * 1024 * 1024
LANES = 128
BF16_SUBLANES = 16
MXU_COLS = 256

NORM_ROWS = 32
NORM_UNROLL = 4
INPROJ_TM = 1024
INPROJ_TN = 1024
INPROJ_ROW_SPLIT = 1
MIX_TM = 512
MIX_TN = 1024
MIX_GATE_W = 512
FFN_TM = 1024
FFN_TN = 512
FFN_HALO = BF16_SUBLANES
FFN_ROW_SPLIT = 4
RET_HEADS_PER_STEP = 2
SWA_KV_PER_STEP = 2
SWA_UNROLL = 8

f32 = jnp.float32
bf16 = jnp.bfloat16


def _params(semantics):
    return pltpu.CompilerParams(dimension_semantics=semantics, vmem_limit_bytes=V7X_VMEM_LIMIT_BYTES)


def _rmsnorm(x, g):
    ms = jnp.mean(x * x, axis=-1, keepdims=True)
    return x * lax.rsqrt(ms + RMS_EPS) * g


def _norm_rows_to(x_ref, g_ref, dst_ref, dst_off, rows):
    g = g_ref[...]
    group = NORM_ROWS * NORM_UNROLL

    def step(c, carry):
        base = pl.multiple_of(c * group, group)
        for k in range(NORM_UNROLL):
            r = base + k * NORM_ROWS
            dst_ref[pl.ds(dst_off + r, NORM_ROWS), :] = (
                _rmsnorm(x_ref[pl.ds(r, NORM_ROWS), :], g).astype(dst_ref.dtype))
        return carry

    lax.fori_loop(0, rows // group, step, 0)


def _residual_norm_rows(x_ref, y_ref, g_ref, o_ref, rows):
    g = g_ref[...]
    group = NORM_ROWS * NORM_UNROLL

    def step(c, carry):
        base = pl.multiple_of(c * group, group)
        rows_k = [pl.ds(base + k * NORM_ROWS, NORM_ROWS) for k in range(NORM_UNROLL)]
        scales = []
        for r in rows_k:
            y = y_ref[r, :]
            scales.append(lax.rsqrt(jnp.mean(y * y, axis=-1, keepdims=True) + RMS_EPS))
        for r, s in zip(rows_k, scales):
            o_ref[r, :] = x_ref[r, :] + y_ref[r, :] * s * g
        return carry

    lax.fori_loop(0, rows // group, step, 0)


def _inproj_tile_kinds():
    starts = ((COL_RQ, "rot"), (COL_RK, "rot"), (COL_RV, "cast"), (COL_RG, "silu"), (COL_SQ, "cast"),
              (COL_GR, "sig"))
    chunk_kinds = [[k for c0, k in starts if c0 <= col][-1] for col in range(0, IN_WIDTH, MXU_COLS)]
    per_tile = INPROJ_TN // MXU_COLS
    return [tuple(chunk_kinds[t:t + per_tile]) for t in range(0, len(chunk_kinds), per_tile)]


_INPROJ_TILE_KINDS = _inproj_tile_kinds()
_T_RK = COL_RK // INPROJ_TN
assert COL_RK % INPROJ_TN == 0 and COL_RV % INPROJ_TN == 0


def _inproj_kernel(x_ref, g_ref, w_ref, rot_ref, o_ref, h_scr):
    j = pl.program_id(1)
    tm = x_ref.shape[0]

    @pl.when(j == 0)
    def _():
        _norm_rows_to(x_ref, g_ref, h_scr, 0, tm)

    def rotary(acc, rows):
        cos = rot_ref[rows, 0:RET_DK]
        sin = rot_ref[rows, RET_DK:2 * RET_DK]
        heads = []
        for hh in range(acc.shape[1] // RET_DK):
            xh = acc[:, hh * RET_DK:(hh + 1) * RET_DK]
            heads.append(xh * cos + pltpu.roll(xh, RET_DK // 2, axis=1) * sin)
        return jnp.concatenate(heads, axis=1)

    epilogues = {
        "rot": rotary,
        "cast": lambda acc, rows: acc,
        "silu": lambda acc, rows: acc * jax.nn.sigmoid(acc),
        "sig": lambda acc, rows: jax.nn.sigmoid(acc),
    }

    def tile(kinds):
        hm = tm // INPROJ_ROW_SPLIT
        for c, kind in enumerate(kinds):
            cs = slice(c * MXU_COLS, (c + 1) * MXU_COLS)
            w = w_ref[:, cs]
            for m in range(INPROJ_ROW_SPLIT):
                rows = slice(m * hm, (m + 1) * hm)
                acc = jnp.dot(h_scr[rows, :], w, preferred_element_type=f32)
                o_ref[rows, cs] = epilogues[kind](acc, rows).astype(o_ref.dtype)

    for kinds in sorted(set(_INPROJ_TILE_KINDS)):
        tiles = [t for t, k in enumerate(_INPROJ_TILE_KINDS) if k == kinds]
        cond = functools.reduce(lambda a, b: a | b, [j == t for t in tiles])
        pl.when(cond)(functools.partial(tile, kinds))


def _inproj(x2, g, w_in, rot, seq):
    T = x2.shape[0]
    tm, tn = INPROJ_TM, INPROJ_TN
    tm = min(tm, seq)
    spt = seq // tm
    grid = (T // tm, len(_INPROJ_TILE_KINDS))
    return pl.pallas_call(
        _inproj_kernel,
        grid=grid,
        in_specs=[
            pl.BlockSpec((tm, D_MODEL), lambda i, j: (i, 0)),
            pl.BlockSpec((1, D_MODEL), lambda i, j: (0, 0)),
            pl.BlockSpec((D_MODEL, tn), lambda i, j: (0, j)),
            pl.BlockSpec((None, tm, 2 * RET_DK), lambda i, j: (jnp.where(j >= _T_RK, 1, 0), i % spt, 0)),
        ],
        out_specs=pl.BlockSpec((tm, tn), lambda i, j: (i, j)),
        out_shape=jax.ShapeDtypeStruct((T, IN_WIDTH), bf16),
        scratch_shapes=[pltpu.VMEM((tm, D_MODEL), bf16)],
        compiler_params=_params(("parallel", "arbitrary")),
        name="inproj",
    )(x2, g, w_in, rot)


def _retention_kernel(q_ref, k_ref, v_ref, gate_ref, dmat_ref, xi_ref, zeta_ref, cdec_ref, o_ref, kv_scr, st_scr):
    S = q_ref.shape[0]
    C = RET_CHUNK
    N = S // C
    heads = range(RET_HEADS_PER_STEP)
    qk = [slice(hh * RET_DK, (hh + 1) * RET_DK) for hh in heads]
    vv = [slice(hh * RET_DV, (hh + 1) * RET_DV) for hh in heads]
    q3 = [q_ref[:, qk[hh]].reshape(N, C, RET_DK) for hh in heads]
    k3 = [k_ref[:, qk[hh]].reshape(N, C, RET_DK) for hh in heads]
    v3 = [v_ref[:, vv[hh]].reshape(N, C, RET_DV) for hh in heads]

    for hh in heads:
        kz = (k3[hh].astype(f32) * zeta_ref[hh][None]).astype(bf16)
        kv_scr[hh] = jnp.einsum("ndk,nkv->ndv", jnp.swapaxes(kz, 1, 2), v3[hh], preferred_element_type=f32)

    cdec = [cdec_ref[hh] for hh in heads]

    def scan(n, states):
        for hh in heads:
            st_scr[hh, n] = states[hh].astype(bf16)
        return tuple(states[hh] * cdec[hh] + kv_scr[hh, n] for hh in heads)

    lax.fori_loop(0, N, scan, tuple(jnp.zeros((RET_DK, RET_DV), f32) for _ in heads))

    for hh in heads:
        sc = jnp.einsum("nqd,nkd->nqk", q3[hh], k3[hh], preferred_element_type=f32) * dmat_ref[hh][None]
        inner = jnp.einsum("nqk,nkv->nqv", sc.astype(bf16), v3[hh], preferred_element_type=f32)
        cross = jnp.einsum("nqd,ndv->nqv", q3[hh], st_scr[hh], preferred_element_type=f32) * xi_ref[hh][None]
        o = inner + cross
        o = o * lax.rsqrt(jnp.mean(o * o, axis=-1, keepdims=True) + RMS_EPS)
        o = gate_ref[:, vv[hh]].astype(f32).reshape(N, C, RET_DV) * o
        o_ref[:, vv[hh]] = o.reshape(S, RET_DV).astype(o_ref.dtype)


def _retention_tables():
    C = RET_CHUNK
    log_gamma = jnp.log(1.0 - 2.0 ** (-5.0 - jnp.arange(RET_HEADS, dtype=f32)))
    idx = jnp.arange(C, dtype=f32)
    rel = idx[:, None] - idx[None, :]
    dmat = jnp.where(rel[None] >= 0, jnp.exp(log_gamma[:, None, None] * jnp.maximum(rel, 0.0)[None]), 0.0)
    xi = jnp.exp(log_gamma[:, None] * (idx + 1.0))
    zeta = jnp.exp(log_gamma[:, None] * (C - 1.0 - idx))
    cdec = jnp.exp(log_gamma * C)
    xi_b = jnp.broadcast_to(xi[:, :, None], (RET_HEADS, C, RET_DV))
    zeta_b = jnp.broadcast_to(zeta[:, :, None], (RET_HEADS, C, RET_DK))
    cdec_b = jnp.broadcast_to(cdec[:, None, None], (RET_HEADS, 1, RET_DV))
    return dmat.astype(f32), xi_b.astype(f32), zeta_b.astype(f32), cdec_b.astype(f32)


def _retention(proj, batch, seq):
    T = proj.shape[0]
    C = RET_CHUNK
    N = seq // C
    dmat, xi_b, zeta_b, cdec_b = _retention_tables()
    hp = RET_HEADS_PER_STEP
    qk_w, v_w = hp * RET_DK, hp * RET_DV
    return pl.pallas_call(
        _retention_kernel,
        grid=(batch, RET_HEADS // hp),
        in_specs=[
            pl.BlockSpec((seq, qk_w), lambda b, h: (b, COL_RQ // qk_w + h)),
            pl.BlockSpec((seq, qk_w), lambda b, h: (b, COL_RK // qk_w + h)),
            pl.BlockSpec((seq, v_w), lambda b, h: (b, COL_RV // v_w + h)),
            pl.BlockSpec((seq, v_w), lambda b, h: (b, COL_RG // v_w + h)),
            pl.BlockSpec((hp, C, C), lambda b, h: (h, 0, 0)),
            pl.BlockSpec((hp, C, RET_DV), lambda b, h: (h, 0, 0)),
            pl.BlockSpec((hp, C, RET_DK), lambda b, h: (h, 0, 0)),
            pl.BlockSpec((hp, 1, RET_DV), lambda b, h: (h, 0, 0)),
        ],
        out_specs=pl.BlockSpec((seq, v_w), lambda b, h: (b, h)),
        out_shape=jax.ShapeDtypeStruct((T, RET_V_W), bf16),
        scratch_shapes=[pltpu.VMEM((hp, N, RET_DK, RET_DV), f32), pltpu.VMEM((hp, N, RET_DK, RET_DV), bf16)],
        compiler_params=_params(("parallel", "parallel")),
        name="retention",
    )(proj, proj, proj, proj, dmat, xi_b, zeta_b, cdec_b)


KV_PER_STEP = SWA_KV_PER_STEP
Q_PER_KV = SWA_Q_HEADS // SWA_KV_HEADS
SWA_Q_STEP_W = KV_PER_STEP * Q_PER_KV * SWA_HEAD_DIM
assert KV_PER_STEP * SWA_HEAD_DIM == LANES and Q_PER_KV * SWA_HEAD_DIM == 2 * LANES


def _swa_kernel(sink_ref, bias_ref, q_ref, k_ref, v_ref, o_ref, klo, khi, vlo, vhi):
    S = q_ref.shape[0]
    C = SWA_BLOCK
    gp = pl.program_id(1)
    half = SWA_HEAD_DIM

    lane_s = lax.broadcasted_iota(jnp.int32, (S, LANES), 1)
    low_s = lane_s < half

    def prep(src_ref, lo_scr, hi_scr, scale, with_ones):
        w = src_ref[...].astype(f32) * scale
        r = pltpu.roll(w, half, axis=1)
        zeros = jnp.zeros((C, lo_scr.shape[2]), bf16)
        for t in range(KV_PER_STEP):
            lo_scr[t, 0:C, :] = zeros
            hi_scr[t, 0:C, :] = zeros
        lo_scr[0, C:C + S, 0:LANES] = jnp.where(low_s, w, 0.0).astype(bf16)
        hi_scr[0, C:C + S, 0:LANES] = jnp.where(low_s, 0.0, r).astype(bf16)
        lo_scr[1, C:C + S, 0:LANES] = jnp.where(low_s, r, 0.0).astype(bf16)
        hi_scr[1, C:C + S, 0:LANES] = jnp.where(low_s, 0.0, w).astype(bf16)
        if with_ones:
            for t in range(KV_PER_STEP):
                lo_scr[t, C:C + S, LANES:2 * LANES] = jnp.where(low_s, 1.0, 0.0).astype(bf16)
                hi_scr[t, C:C + S, LANES:2 * LANES] = jnp.where(low_s, 0.0, 1.0).astype(bf16)

    prep(k_ref, klo, khi, SWA_HEAD_DIM ** -0.5, False)
    prep(v_ref, vlo, vhi, 1.0, True)

    rows = 2 * C
    win = 2 * C
    first_pair = lax.broadcasted_iota(jnp.int32, (rows, 1), 0) < C
    low_o = lax.broadcasted_iota(jnp.int32, (rows, LANES), 1) < half
    nt = (((1,), (1,)), ((), ()))

    def softmax_parts(s, sink, bias):
        s = s + bias
        m = jnp.maximum(jnp.max(s, axis=-1, keepdims=True), sink)
        return jnp.exp(s - m).astype(bf16), jnp.exp(sink - m)

    def block(n, carry):
        r0 = pl.multiple_of(n * C, C)
        bias = bias_ref[jnp.minimum(n, 1)]
        for t in range(KV_PER_STEP):
            c0 = t * Q_PER_KV * SWA_HEAD_DIM
            hbase = gp * (KV_PER_STEP * Q_PER_KV) + t * Q_PER_KV
            qs = jnp.concatenate([q_ref[pl.ds(r0, C), c0:c0 + LANES],
                                  q_ref[pl.ds(r0, C), c0 + LANES:c0 + 2 * LANES]], axis=0)
            s_e = lax.dot_general(qs, klo[t, pl.ds(r0, win), :], nt, preferred_element_type=f32)
            s_o = lax.dot_general(qs, khi[t, pl.ds(r0, win), :], nt, preferred_element_type=f32)
            sink_e = jnp.where(first_pair, sink_ref[hbase + 0], sink_ref[hbase + 2])
            sink_o = jnp.where(first_pair, sink_ref[hbase + 1], sink_ref[hbase + 3])
            p_e, z_e = softmax_parts(s_e, sink_e, bias)
            p_o, z_o = softmax_parts(s_o, sink_o, bias)
            pv = (jnp.dot(p_e, vlo[t, pl.ds(r0, win), :], preferred_element_type=f32)
                  + jnp.dot(p_o, vhi[t, pl.ds(r0, win), :], preferred_element_type=f32))
            den = pv[:, LANES:2 * LANES] + jnp.where(low_o, z_e, z_o)
            o = pv[:, 0:LANES] / den
            o_ref[pl.ds(r0, C), c0:c0 + LANES] = o[0:C].astype(o_ref.dtype)
            o_ref[pl.ds(r0, C), c0 + LANES:c0 + 2 * LANES] = o[C:2 * C].astype(o_ref.dtype)
        return carry

    lax.fori_loop(0, S // C, block, 0, unroll=SWA_UNROLL)


def _swa_bias():
    C = SWA_BLOCK
    row_i = jnp.arange(2 * C)[:, None] % C
    col_j = jnp.arange(2 * C)[None, :]
    band = (col_j > row_i) & (col_j <= row_i + C)
    first = band & (col_j >= C)
    return jnp.where(jnp.stack([first, band]), 0.0, -jnp.inf).astype(f32)


def _swa(proj, sinks, batch, seq):
    T = proj.shape[0]
    steps = SWA_KV_HEADS // KV_PER_STEP
    k_scr = pltpu.VMEM((KV_PER_STEP, seq + SWA_BLOCK, LANES), bf16)
    v_scr = pltpu.VMEM((KV_PER_STEP, seq + SWA_BLOCK, 2 * LANES), bf16)
    bias = _swa_bias()
    return pl.pallas_call(
        _swa_kernel,
        grid=(batch, steps),
        in_specs=[
            pl.BlockSpec(memory_space=pltpu.SMEM),
            pl.BlockSpec(bias.shape, lambda b, g: (0, 0, 0)),
            pl.BlockSpec((seq, SWA_Q_STEP_W), lambda b, g: (b, COL_SQ // SWA_Q_STEP_W + g)),
            pl.BlockSpec((seq, LANES), lambda b, g: (b, COL_SK // LANES + g)),
            pl.BlockSpec((seq, LANES), lambda b, g: (b, COL_SV // LANES + g)),
        ],
        out_specs=pl.BlockSpec((seq, SWA_Q_STEP_W), lambda b, g: (b, g)),
        out_shape=jax.ShapeDtypeStruct((T, SWA_Q_W), bf16),
        scratch_shapes=[k_scr, k_scr, v_scr, v_scr],
        compiler_params=_params(("parallel", "parallel")),
        name="swa",
    )(sinks, bias, proj, proj, proj)


def _mix_kernel(ret_ref, swa_ref, *refs):
    n_gate = MIX_TN // MIX_GATE_W
    gr_refs, gs_refs = refs[:n_gate], refs[n_gate:2 * n_gate]
    wr_ref, ws_ref, wo_ref, x_ref, g_ref, o_ref = refs[2 * n_gate:]
    j = pl.program_id(1)

    @pl.when(j == 0)
    def _():
        o_ref[...] = jnp.zeros_like(o_ref)

    part = None
    for c in range(n_gate):
        cs = slice(c * MIX_GATE_W, (c + 1) * MIX_GATE_W)
        ret_out = jnp.dot(ret_ref[...], wr_ref[:, cs], preferred_element_type=f32)
        swa_out = jnp.dot(swa_ref[...], ws_ref[:, cs], preferred_element_type=f32)
        mixed = gr_refs[c][...].astype(f32) * ret_out + gs_refs[c][...].astype(f32) * swa_out
        d = jnp.dot(mixed.astype(bf16), wo_ref[cs, :], preferred_element_type=f32)
        part = d if part is None else part + d
    o_ref[...] += part

    @pl.when(j == pl.num_programs(1) - 1)
    def _():
        _residual_norm_rows(x_ref, o_ref, g_ref, o_ref, o_ref.shape[0])


def _mix(ret, swa, proj, w_ret_o, w_swa_o, w_out, x2, g):
    T = x2.shape[0]
    tm, tn = min(MIX_TM, T), MIX_TN
    nj = D_MODEL // tn
    gw = MIX_GATE_W
    n_gate = tn // gw

    def gate_specs(col0):
        return [pl.BlockSpec((tm, gw), functools.partial(lambda i, j, c: (i, col0 // gw + j * n_gate + c), c=c))
                for c in range(n_gate)]

    return pl.pallas_call(
        _mix_kernel,
        grid=(T // tm, nj),
        in_specs=[
            pl.BlockSpec((tm, RET_V_W), lambda i, j: (i, 0)),
            pl.BlockSpec((tm, SWA_Q_W), lambda i, j: (i, 0)),
            *gate_specs(COL_GR),
            *gate_specs(COL_GS),
            pl.BlockSpec((RET_V_W, tn), lambda i, j: (0, j)),
            pl.BlockSpec((SWA_Q_W, tn), lambda i, j: (0, j)),
            pl.BlockSpec((tn, D_MODEL), lambda i, j: (j, 0)),
            pl.BlockSpec((tm, D_MODEL), lambda i, j: (i, 0)),
            pl.BlockSpec((1, D_MODEL), lambda i, j: (0, 0)),
        ],
        out_specs=pl.BlockSpec((tm, D_MODEL), lambda i, j: (i, 0)),
        out_shape=jax.ShapeDtypeStruct((T, D_MODEL), f32),
        compiler_params=_params(("parallel", "arbitrary")),
        name="mix",
    )(ret, swa, *([proj] * (2 * n_gate)), w_ret_o, w_swa_o, w_out, x2, g)


def _gelu_tanh(x):
    return x * (0.5 * (1.0 + jnp.tanh(0.7978845608028654 * (x + 0.044715 * (x * x * x)))))


def _ffn_kernel(x_ref, halo_ref, gpre_ref, wv_ref, wg_ref, cwv_ref, cwg_ref, cbv_ref, cbg_ref, wd_ref, gpost_ref,
                o_ref, h_scr, *u_scrs, tiles_per_seq):
    i = pl.program_id(0)
    j = pl.program_id(1)
    tm = x_ref.shape[0]
    tn = wv_ref.shape[1]
    H = FFN_HALO

    @pl.when(j == 0)
    def _():
        halo = jnp.where(i % tiles_per_seq == 0, 0.0, _rmsnorm(halo_ref[...], gpre_ref[...]))
        h_scr[0:H, :] = halo.astype(bf16)
        _norm_rows_to(x_ref, gpre_ref, h_scr, H, tm)
        o_ref[...] = jnp.zeros_like(o_ref)

    def conv(u_scr, row0, nrows, u_cols, w_cols, cw_ref, cb_ref):
        y = cb_ref[:, w_cols]
        for kk in range(CONV_WIDTH):
            y = y + u_scr[pl.ds(H - (CONV_WIDTH - 1) + kk + row0, nrows), u_cols] * cw_ref[kk:kk + 1, w_cols]
        return y

    h = h_scr[...]
    hm = tm // FFN_ROW_SPLIT
    parts = [None] * FFN_ROW_SPLIT
    uv = slice(0, MXU_COLS)
    ug = slice(MXU_COLS, 2 * MXU_COLS)
    n_chunks = len(u_scrs)
    for c, u_scr in enumerate(u_scrs):
        cs = slice(c * MXU_COLS, (c + 1) * MXU_COLS)
        if c < n_chunks - 1:
            u_scr[:, uv] = jnp.dot(h, wv_ref[:, cs], preferred_element_type=f32)
            u_scr[:, ug] = jnp.dot(h, wg_ref[:, cs], preferred_element_type=f32)
        else:
            wv, wg = wv_ref[:, cs], wg_ref[:, cs]
            r0 = 0
            for m in range(FFN_ROW_SPLIT):
                r1 = H + (m + 1) * hm
                u_scr[r0:r1, uv] = jnp.dot(h_scr[r0:r1, :], wv, preferred_element_type=f32)
                u_scr[r0:r1, ug] = jnp.dot(h_scr[r0:r1, :], wg, preferred_element_type=f32)
                r0 = r1
        for m in range(FFN_ROW_SPLIT):
            val = conv(u_scr, m * hm, hm, uv, cs, cwv_ref, cbv_ref)
            gate = conv(u_scr, m * hm, hm, ug, cs, cwg_ref, cbg_ref)
            a = (_gelu_tanh(gate) * val).astype(bf16)
            d = jnp.dot(a, wd_ref[cs, :], preferred_element_type=f32)
            parts[m] = d if parts[m] is None else parts[m] + d
    for m in range(FFN_ROW_SPLIT):
        o_ref[m * hm:(m + 1) * hm, :] += parts[m]

    @pl.when(j == pl.num_programs(1) - 1)
    def _():
        _residual_norm_rows(x_ref, o_ref, gpost_ref, o_ref, tm)


def _ffn(x1, g_pre, w_up, conv_w, conv_b, w_down, g_post, seq):
    T = x1.shape[0]
    tm, tn = min(FFN_TM, seq), FFN_TN
    nj = D_FF // tn
    hb = tm // FFN_HALO
    kern = functools.partial(_ffn_kernel, tiles_per_seq=seq // tm)
    return pl.pallas_call(
        kern,
        grid=(T // tm, nj),
        in_specs=[
            pl.BlockSpec((tm, D_MODEL), lambda i, j: (i, 0)),
            pl.BlockSpec((FFN_HALO, D_MODEL), lambda i, j: (jnp.maximum(i * hb - 1, 0), 0)),
            pl.BlockSpec((1, D_MODEL), lambda i, j: (0, 0)),
            pl.BlockSpec((D_MODEL, tn), lambda i, j: (0, j)),
            pl.BlockSpec((D_MODEL, tn), lambda i, j: (0, nj + j)),
            pl.BlockSpec((CONV_WIDTH, tn), lambda i, j: (0, j)),
            pl.BlockSpec((CONV_WIDTH, tn), lambda i, j: (0, nj + j)),
            pl.BlockSpec((1, tn), lambda i, j: (0, j)),
            pl.BlockSpec((1, tn), lambda i, j: (0, nj + j)),
            pl.BlockSpec((tn, D_MODEL), lambda i, j: (j, 0)),
            pl.BlockSpec((1, D_MODEL), lambda i, j: (0, 0)),
        ],
        out_specs=pl.BlockSpec((tm, D_MODEL), lambda i, j: (i, 0)),
        out_shape=jax.ShapeDtypeStruct((T, D_MODEL), f32),
        scratch_shapes=[
            pltpu.VMEM((tm + FFN_HALO, D_MODEL), bf16),
        ] + [pltpu.VMEM((tm + FFN_HALO, 2 * MXU_COLS), f32)] * (tn // MXU_COLS),
        compiler_params=_params(("parallel", "arbitrary")),
        name="ffn",
    )(x1, x1, g_pre, w_up, w_up, conv_w, conv_w, conv_b, conv_b, w_down, g_post)


def _rotary_tables(seq):
    d = RET_DK
    inv = 1.0 / (ROPE_BASE ** (jnp.arange(0, d, 2, dtype=f32) / d))
    ang = jnp.arange(seq, dtype=jnp.int32).astype(f32)[:, None] * inv[None, :]
    cos, sin = jnp.cos(ang), jnp.sin(ang)
    tab = jnp.concatenate([cos, cos, -sin, sin], axis=-1)
    return jnp.stack([tab, tab * (d ** -0.5)])


def kernel(x, g_pre_mix, w_in, w_ret_o, w_swa_o, w_out, swa_sinks, g_post_mix, g_pre_ffn, w_up, conv_w, conv_b,
           w_down, g_post_ffn):
    B, S, D = x.shape
    depth = w_in.shape[0]
    assert D == D_MODEL and w_in.shape[1:] == (D_MODEL, IN_WIDTH) and w_up.shape[1:] == (D_MODEL, 2 * D_FF)
    assert S % RET_CHUNK == 0 and S % SWA_BLOCK == 0
    assert S % min(INPROJ_TM, S) == 0 and S % min(FFN_TM, S) == 0 and (B * S) % min(MIX_TM, B * S) == 0
    rot = _rotary_tables(S)
    x2 = x.reshape(B * S, D)
    for l in range(depth):
        proj = _inproj(x2, g_pre_mix[l][None], w_in[l].astype(bf16), rot, S)
        ret = _retention(proj, B, S)
        swa = _swa(proj, swa_sinks[l].astype(f32), B, S)
        x2 = _mix(ret, swa, proj, w_ret_o[l].astype(bf16), w_swa_o[l].astype(bf16), w_out[l].astype(bf16), x2,
                  g_post_mix[l][None])
        x2 = _ffn(x2, g_pre_ffn[l][None], w_up[l].astype(bf16), conv_w[l], conv_b[l][None], w_down[l].astype(bf16),
                  g_post_ffn[l][None], S)
    return x2.reshape(B, S, D)
```

```python
import functools

import jax
import jax.numpy as jnp
from jax import lax
from jax.experimental import pallas as pl
from jax.experimental.pallas import tpu as pltpu

D_MODEL = 2048
RET_HEADS = 8
RET_DK = 128
RET_DV = 256
RET_CHUNK = 256
ROPE_BASE = 10000.0
SWA_Q_HEADS = 16
SWA_KV_HEADS = 4
SWA_HEAD_DIM = 64
SWA_BLOCK = 128
D_FF = 5632
CONV_WIDTH = 3
RMS_EPS = 1e-6

RET_QK_W = RET_HEADS * RET_DK
RET_V_W = RET_HEADS * RET_DV
SWA_Q_W = SWA_Q_HEADS * SWA_HEAD_DIM
SWA_KV_W = SWA_KV_HEADS * SWA_HEAD_DIM

COL_RQ = 0
COL_RK = COL_RQ + RET_QK_W
COL_RV = COL_RK + RET_QK_W
COL_RG = COL_RV + RET_V_W
COL_SQ = COL_RG + RET_V_W
COL_SK = COL_SQ + SWA_Q_W
COL_SV = COL_SK + SWA_KV_W
COL_GR = COL_SV + SWA_KV_W
COL_GS = COL_GR + D_MODEL
IN_WIDTH = COL_GS + D_MODEL

V7X_VMEM_LIMIT_BYTES = 60 * 1024 * 1024
LANES = 128
BF16_SUBLANES = 16
MXU_COLS = 256

NORM_ROWS = 32
NORM_UNROLL = 4
INPROJ_TM = 1024
INPROJ_TN = 1024
INPROJ_ROW_SPLIT = 4
MIX_TM = 512
MIX_TN = 1024
MIX_GATE_W = 512
FFN_TM = 1024
FFN_TN = 512
FFN_HALO = BF16_SUBLANES
FFN_ROW_SPLIT = 4
RET_HEADS_PER_STEP = 2
SWA_KV_PER_STEP = 2
SWA_UNROLL = 8

f32 = jnp.float32
bf16 = jnp.bfloat16


def _params(semantics):
    return pltpu.CompilerParams(dimension_semantics=semantics, vmem_limit_bytes=V7X_VMEM_LIMIT_BYTES)


def _rmsnorm(x, g):
    ms = jnp.mean(x * x, axis=-1, keepdims=True)
    return x * lax.rsqrt(ms + RMS_EPS) * g


def _norm_rows_to(x_ref, g_ref, dst_ref, dst_off, rows):
    g = g_ref[...]
    group = NORM_ROWS * NORM_UNROLL

    def step(c, carry):
        base = pl.multiple_of(c * group, group)
        for k in range(NORM_UNROLL):
            r = base + k * NORM_ROWS
            dst_ref[pl.ds(dst_off + r, NORM_ROWS), :] = (
                _rmsnorm(x_ref[pl.ds(r, NORM_ROWS), :], g).astype(dst_ref.dtype))
        return carry

    lax.fori_loop(0, rows // group, step, 0)


def _residual_norm_rows(x_ref, y_ref, g_ref, o_ref, rows):
    g = g_ref[...]
    group = NORM_ROWS * NORM_UNROLL

    def step(c, carry):
        base = pl.multiple_of(c * group, group)
        rows_k = [pl.ds(base + k * NORM_ROWS, NORM_ROWS) for k in range(NORM_UNROLL)]
        scales = []
        for r in rows_k:
            y = y_ref[r, :]
            scales.append(lax.rsqrt(jnp.mean(y * y, axis=-1, keepdims=True) + RMS_EPS))
        for r, s in zip(rows_k, scales):
            o_ref[r, :] = x_ref[r, :] + y_ref[r, :] * s * g
        return carry

    lax.fori_loop(0, rows // group, step, 0)


def _inproj_tile_kinds():
    starts = ((COL_RQ, "rot"), (COL_RK, "rot"), (COL_RV, "cast"), (COL_RG, "silu"), (COL_SQ, "cast"),
              (COL_GR, "sig"))
    chunk_kinds = [[k for c0, k in starts if c0 <= col][-1] for col in range(0, IN_WIDTH, MXU_COLS)]
    per_tile = INPROJ_TN // MXU_COLS
    return [tuple(chunk_kinds[t:t + per_tile]) for t in range(0, len(chunk_kinds), per_tile)]


_INPROJ_TILE_KINDS = _inproj_tile_kinds()
_T_RK = COL_RK // INPROJ_TN
assert COL_RK % INPROJ_TN == 0 and COL_RV % INPROJ_TN == 0


def _inproj_kernel(x_ref, g_ref, w_ref, rot_ref, o_ref, h_scr):
    j = pl.program_id(1)
    tm = x_ref.shape[0]

    @pl.when(j == 0)
    def _():
        _norm_rows_to(x_ref, g_ref, h_scr, 0, tm)

    def rotary(acc, rows):
        cos = rot_ref[rows, 0:RET_DK]
        sin = rot_ref[rows, RET_DK:2 * RET_DK]
        heads = []
        for hh in range(acc.shape[1] // RET_DK):
            xh = acc[:, hh * RET_DK:(hh + 1) * RET_DK]
            heads.append(xh * cos + pltpu.roll(xh, RET_DK // 2, axis=1) * sin)
        return jnp.concatenate(heads, axis=1)

    epilogues = {
        "rot": rotary,
        "cast": lambda acc, rows: acc,
        "silu": lambda acc, rows: acc * jax.nn.sigmoid(acc),
        "sig": lambda acc, rows: jax.nn.sigmoid(acc),
    }

    def tile(kinds):
        hm = tm // INPROJ_ROW_SPLIT
        for c, kind in enumerate(kinds):
            cs = slice(c * MXU_COLS, (c + 1) * MXU_COLS)
            w = w_ref[:, cs]
            for m in range(INPROJ_ROW_SPLIT):
                rows = slice(m * hm, (m + 1) * hm)
                acc = jnp.dot(h_scr[rows, :], w, preferred_element_type=f32)
                o_ref[rows, cs] = epilogues[kind](acc, rows).astype(o_ref.dtype)

    for kinds in sorted(set(_INPROJ_TILE_KINDS)):
        tiles = [t for t, k in enumerate(_INPROJ_TILE_KINDS) if k == kinds]
        cond = functools.reduce(lambda a, b: a | b, [j == t for t in tiles])
        pl.when(cond)(functools.partial(tile, kinds))


def _inproj(x2, g, w_in, rot, seq):
    T = x2.shape[0]
    tm, tn = INPROJ_TM, INPROJ_TN
    tm = min(tm, seq)
    spt = seq // tm
    grid = (T // tm, len(_INPROJ_TILE_KINDS))
    return pl.pallas_call(
        _inproj_kernel,
        grid=grid,
        in_specs=[
            pl.BlockSpec((tm, D_MODEL), lambda i, j: (i, 0)),
            pl.BlockSpec((1, D_MODEL), lambda i, j: (0, 0)),
            pl.BlockSpec((D_MODEL, tn), lambda i, j: (0, j)),
            pl.BlockSpec((None, tm, 2 * RET_DK), lambda i, j: (jnp.where(j >= _T_RK, 1, 0), i % spt, 0)),
        ],
        out_specs=pl.BlockSpec((tm, tn), lambda i, j: (i, j)),
        out_shape=jax.ShapeDtypeStruct((T, IN_WIDTH), bf16),
        scratch_shapes=[pltpu.VMEM((tm, D_MODEL), bf16)],
        compiler_params=_params(("parallel", "arbitrary")),
        name="inproj",
    )(x2, g, w_in, rot)


def _retention_kernel(q_ref, k_ref, v_ref, gate_ref, dmat_ref, xi_ref, zeta_ref, cdec_ref, o_ref, kv_scr, st_scr):
    S = q_ref.shape[0]
    C = RET_CHUNK
    N = S // C
    heads = range(RET_HEADS_PER_STEP)
    qk = [slice(hh * RET_DK, (hh + 1) * RET_DK) for hh in heads]
    vv = [slice(hh * RET_DV, (hh + 1) * RET_DV) for hh in heads]
    q3 = [q_ref[:, qk[hh]].reshape(N, C, RET_DK) for hh in heads]
    k3 = [k_ref[:, qk[hh]].reshape(N, C, RET_DK) for hh in heads]
    v3 = [v_ref[:, vv[hh]].reshape(N, C, RET_DV) for hh in heads]

    for hh in heads:
        kz = (k3[hh].astype(f32) * zeta_ref[hh][None]).astype(bf16)
        kv_scr[hh] = jnp.einsum("ndk,nkv->ndv", jnp.swapaxes(kz, 1, 2), v3[hh], preferred_element_type=f32)

    cdec = [cdec_ref[hh] for hh in heads]

    def scan(n, states):
        for hh in heads:
            st_scr[hh, n] = states[hh].astype(bf16)
        return tuple(states[hh] * cdec[hh] + kv_scr[hh, n] for hh in heads)

    lax.fori_loop(0, N, scan, tuple(jnp.zeros((RET_DK, RET_DV), f32) for _ in heads))

    for hh in heads:
        sc = jnp.einsum("nqd,nkd->nqk", q3[hh], k3[hh], preferred_element_type=f32) * dmat_ref[hh][None]
        inner = jnp.einsum("nqk,nkv->nqv", sc.astype(bf16), v3[hh], preferred_element_type=f32)
        cross = jnp.einsum("nqd,ndv->nqv", q3[hh], st_scr[hh], preferred_element_type=f32) * xi_ref[hh][None]
        o = inner + cross
        o = o * lax.rsqrt(jnp.mean(o * o, axis=-1, keepdims=True) + RMS_EPS)
        o = gate_ref[:, vv[hh]].astype(f32).reshape(N, C, RET_DV) * o
        o_ref[:, vv[hh]] = o.reshape(S, RET_DV).astype(o_ref.dtype)


def _retention_tables():
    C = RET_CHUNK
    log_gamma = jnp.log(1.0 - 2.0 ** (-5.0 - jnp.arange(RET_HEADS, dtype=f32)))
    idx = jnp.arange(C, dtype=f32)
    rel = idx[:, None] - idx[None, :]
    dmat = jnp.where(rel[None] >= 0, jnp.exp(log_gamma[:, None, None] * jnp.maximum(rel, 0.0)[None]), 0.0)
    xi = jnp.exp(log_gamma[:, None] * (idx + 1.0))
    zeta = jnp.exp(log_gamma[:, None] * (C - 1.0 - idx))
    cdec = jnp.exp(log_gamma * C)
    xi_b = jnp.broadcast_to(xi[:, :, None], (RET_HEADS, C, RET_DV))
    zeta_b = jnp.broadcast_to(zeta[:, :, None], (RET_HEADS, C, RET_DK))
    cdec_b = jnp.broadcast_to(cdec[:, None, None], (RET_HEADS, 1, RET_DV))
    return dmat.astype(f32), xi_b.astype(f32), zeta_b.astype(f32), cdec_b.astype(f32)


def _retention(proj, batch, seq):
    T = proj.shape[0]
    C = RET_CHUNK
    N = seq // C
    dmat, xi_b, zeta_b, cdec_b = _retention_tables()
    hp = RET_HEADS_PER_STEP
    qk_w, v_w = hp * RET_DK, hp * RET_DV
    return pl.pallas_call(
        _retention_kernel,
        grid=(batch, RET_HEADS // hp),
        in_specs=[
            pl.BlockSpec((seq, qk_w), lambda b, h: (b, COL_RQ // qk_w + h)),
            pl.BlockSpec((seq, qk_w), lambda b, h: (b, COL_RK // qk_w + h)),
            pl.BlockSpec((seq, v_w), lambda b, h: (b, COL_RV // v_w + h)),
            pl.BlockSpec((seq, v_w), lambda b, h: (b, COL_RG // v_w + h)),
            pl.BlockSpec((hp, C, C), lambda b, h: (h, 0, 0)),
            pl.BlockSpec((hp, C, RET_DV), lambda b, h: (h, 0, 0)),
            pl.BlockSpec((hp, C, RET_DK), lambda b, h: (h, 0, 0)),
            pl.BlockSpec((hp, 1, RET_DV), lambda b, h: (h, 0, 0)),
        ],
        out_specs=pl.BlockSpec((seq, v_w), lambda b, h: (b, h)),
        out_shape=jax.ShapeDtypeStruct((T, RET_V_W), bf16),
        scratch_shapes=[pltpu.VMEM((hp, N, RET_DK, RET_DV), f32), pltpu.VMEM((hp, N, RET_DK, RET_DV), bf16)],
        compiler_params=_params(("parallel", "parallel")),
        name="retention",
    )(proj, proj, proj, proj, dmat, xi_b, zeta_b, cdec_b)


KV_PER_STEP = SWA_KV_PER_STEP
Q_PER_KV = SWA_Q_HEADS // SWA_KV_HEADS
SWA_Q_STEP_W = KV_PER_STEP * Q_PER_KV * SWA_HEAD_DIM
assert KV_PER_STEP * SWA_HEAD_DIM == LANES and Q_PER_KV * SWA_HEAD_DIM == 2 * LANES


def _swa_kernel(sink_ref, bias_ref, q_ref, k_ref, v_ref, o_ref, klo, khi, vlo, vhi):
    S = q_ref.shape[0]
    C = SWA_BLOCK
    gp = pl.program_id(1)
    half = SWA_HEAD_DIM

    lane_s = lax.broadcasted_iota(jnp.int32, (S, LANES), 1)
    low_s = lane_s < half

    def prep(src_ref, lo_scr, hi_scr, scale, with_ones):
        w = src_ref[...].astype(f32) * scale
        r = pltpu.roll(w, half, axis=1)
        zeros = jnp.zeros((C, lo_scr.shape[2]), bf16)
        for t in range(KV_PER_STEP):
            lo_scr[t, 0:C, :] = zeros
            hi_scr[t, 0:C, :] = zeros
        lo_scr[0, C:C + S, 0:LANES] = jnp.where(low_s, w, 0.0).astype(bf16)
        hi_scr[0, C:C + S, 0:LANES] = jnp.where(low_s, 0.0, r).astype(bf16)
        lo_scr[1, C:C + S, 0:LANES] = jnp.where(low_s, r, 0.0).astype(bf16)
        hi_scr[1, C:C + S, 0:LANES] = jnp.where(low_s, 0.0, w).astype(bf16)
        if with_ones:
            for t in range(KV_PER_STEP):
                lo_scr[t, C:C + S, LANES:2 * LANES] = jnp.where(low_s, 1.0, 0.0).astype(bf16)
                hi_scr[t, C:C + S, LANES:2 * LANES] = jnp.where(low_s, 0.0, 1.0).astype(bf16)

    prep(k_ref, klo, khi, SWA_HEAD_DIM ** -0.5, False)
    prep(v_ref, vlo, vhi, 1.0, True)

    rows = 2 * C
    win = 2 * C
    first_pair = lax.broadcasted_iota(jnp.int32, (rows, 1), 0) < C
    low_o = lax.broadcasted_iota(jnp.int32, (rows, LANES), 1) < half
    nt = (((1,), (1,)), ((), ()))

    def softmax_parts(s, sink, bias):
        s = s + bias
        m = jnp.maximum(jnp.max(s, axis=-1, keepdims=True), sink)
        return jnp.exp(s - m).astype(bf16), jnp.exp(sink - m)

    def block(n, carry):
        r0 = pl.multiple_of(n * C, C)
        bias = bias_ref[jnp.minimum(n, 1)]
        for t in range(KV_PER_STEP):
            c0 = t * Q_PER_KV * SWA_HEAD_DIM
            hbase = gp * (KV_PER_STEP * Q_PER_KV) + t * Q_PER_KV
            qs = jnp.concatenate([q_ref[pl.ds(r0, C), c0:c0 + LANES],
                                  q_ref[pl.ds(r0, C), c0 + LANES:c0 + 2 * LANES]], axis=0)
            s_e = lax.dot_general(qs, klo[t, pl.ds(r0, win), :], nt, preferred_element_type=f32)
            s_o = lax.dot_general(qs, khi[t, pl.ds(r0, win), :], nt, preferred_element_type=f32)
            sink_e = jnp.where(first_pair, sink_ref[hbase + 0], sink_ref[hbase + 2])
            sink_o = jnp.where(first_pair, sink_ref[hbase + 1], sink_ref[hbase + 3])
            p_e, z_e = softmax_parts(s_e, sink_e, bias)
            p_o, z_o = softmax_parts(s_o, sink_o, bias)
            pv = (jnp.dot(p_e, vlo[t, pl.ds(r0, win), :], preferred_element_type=f32)
                  + jnp.dot(p_o, vhi[t, pl.ds(r0, win), :], preferred_element_type=f32))
            den = pv[:, LANES:2 * LANES] + jnp.where(low_o, z_e, z_o)
            o = pv[:, 0:LANES] / den
            o_ref[pl.ds(r0, C), c0:c0 + LANES] = o[0:C].astype(o_ref.dtype)
            o_ref[pl.ds(r0, C), c0 + LANES:c0 + 2 * LANES] = o[C:2 * C].astype(o_ref.dtype)
        return carry

    lax.fori_loop(0, S // C, block, 0, unroll=SWA_UNROLL)


def _swa_bias():
    C = SWA_BLOCK
    row_i = jnp.arange(2 * C)[:, None] % C
    col_j = jnp.arange(2 * C)[None, :]
    band = (col_j > row_i) & (col_j <= row_i + C)
    first = band & (col_j >= C)
    return jnp.where(jnp.stack([first, band]), 0.0, -jnp.inf).astype(f32)


def _swa(proj, sinks, batch, seq):
    T = proj.shape[0]
    steps = SWA_KV_HEADS // KV_PER_STEP
    k_scr = pltpu.VMEM((KV_PER_STEP, seq + SWA_BLOCK, LANES), bf16)
    v_scr = pltpu.VMEM((KV_PER_STEP, seq + SWA_BLOCK, 2 * LANES), bf16)
    bias = _swa_bias()
    return pl.pallas_call(
        _swa_kernel,
        grid=(batch, steps),
        in_specs=[
            pl.BlockSpec(memory_space=pltpu.SMEM),
            pl.BlockSpec(bias.shape, lambda b, g: (0, 0, 0)),
            pl.BlockSpec((seq, SWA_Q_STEP_W), lambda b, g: (b, COL_SQ // SWA_Q_STEP_W + g)),
            pl.BlockSpec((seq, LANES), lambda b, g: (b, COL_SK // LANES + g)),
            pl.BlockSpec((seq, LANES), lambda b, g: (b, COL_SV // LANES + g)),
        ],
        out_specs=pl.BlockSpec((seq, SWA_Q_STEP_W), lambda b, g: (b, g)),
        out_shape=jax.ShapeDtypeStruct((T, SWA_Q_W), bf16),
        scratch_shapes=[k_scr, k_scr, v_scr, v_scr],
        compiler_params=_params(("parallel", "parallel")),
        name="swa",
    )(sinks, bias, proj, proj, proj)


def _mix_kernel(ret_ref, swa_ref, *refs):
    n_gate = MIX_TN // MIX_GATE_W
    gr_refs, gs_refs = refs[:n_gate], refs[n_gate:2 * n_gate]
    wr_ref, ws_ref, wo_ref, x_ref, g_ref, o_ref = refs[2 * n_gate:]
    j = pl.program_id(1)

    @pl.when(j == 0)
    def _():
        o_ref[...] = jnp.zeros_like(o_ref)

    part = None
    for c in range(n_gate):
        cs = slice(c * MIX_GATE_W, (c + 1) * MIX_GATE_W)
        ret_out = jnp.dot(ret_ref[...], wr_ref[:, cs], preferred_element_type=f32)
        swa_out = jnp.dot(swa_ref[...], ws_ref[:, cs], preferred_element_type=f32)
        mixed = gr_refs[c][...].astype(f32) * ret_out + gs_refs[c][...].astype(f32) * swa_out
        d = jnp.dot(mixed.astype(bf16), wo_ref[cs, :], preferred_element_type=f32)
        part = d if part is None else part + d
    o_ref[...] += part

    @pl.when(j == pl.num_programs(1) - 1)
    def _():
        _residual_norm_rows(x_ref, o_ref, g_ref, o_ref, o_ref.shape[0])


def _mix(ret, swa, proj, w_ret_o, w_swa_o, w_out, x2, g):
    T = x2.shape[0]
    tm, tn = min(MIX_TM, T), MIX_TN
    nj = D_MODEL // tn
    gw = MIX_GATE_W
    n_gate = tn // gw

    def gate_specs(col0):
        return [pl.BlockSpec((tm, gw), functools.partial(lambda i, j, c: (i, col0 // gw + j * n_gate + c), c=c))
                for c in range(n_gate)]

    return pl.pallas_call(
        _mix_kernel,
        grid=(T // tm, nj),
        in_specs=[
            pl.BlockSpec((tm, RET_V_W), lambda i, j: (i, 0)),
            pl.BlockSpec((tm, SWA_Q_W), lambda i, j: (i, 0)),
            *gate_specs(COL_GR),
            *gate_specs(COL_GS),
            pl.BlockSpec((RET_V_W, tn), lambda i, j: (0, j)),
            pl.BlockSpec((SWA_Q_W, tn), lambda i, j: (0, j)),
            pl.BlockSpec((tn, D_MODEL), lambda i, j: (j, 0)),
            pl.BlockSpec((tm, D_MODEL), lambda i, j: (i, 0)),
            pl.BlockSpec((1, D_MODEL), lambda i, j: (0, 0)),
        ],
        out_specs=pl.BlockSpec((tm, D_MODEL), lambda i, j: (i, 0)),
        out_shape=jax.ShapeDtypeStruct((T, D_MODEL), f32),
        compiler_params=_params(("parallel", "arbitrary")),
        name="mix",
    )(ret, swa, *([proj] * (2 * n_gate)), w_ret_o, w_swa_o, w_out, x2, g)


def _gelu_tanh(x):
    return x * (0.5 * (1.0 + jnp.tanh(0.7978845608028654 * (x + 0.044715 * (x * x * x)))))


def _ffn_kernel(x_ref, halo_ref, gpre_ref, wv_ref, wg_ref, cwv_ref, cwg_ref, cbv_ref, cbg_ref, wd_ref, gpost_ref,
                o_ref, h_scr, *u_scrs, tiles_per_seq):
    i = pl.program_id(0)
    j = pl.program_id(1)
    tm = x_ref.shape[0]
    tn = wv_ref.shape[1]
    H = FFN_HALO

    @pl.when(j == 0)
    def _():
        halo = jnp.where(i % tiles_per_seq == 0, 0.0, _rmsnorm(halo_ref[...], gpre_ref[...]))
        h_scr[0:H, :] = halo.astype(bf16)
        _norm_rows_to(x_ref, gpre_ref, h_scr, H, tm)
        o_ref[...] = jnp.zeros_like(o_ref)

    def conv(u_scr, row0, nrows, u_cols, w_cols, cw_ref, cb_ref):
        y = cb_ref[:, w_cols]
        for kk in range(CONV_WIDTH):
            y = y + u_scr[pl.ds(H - (CONV_WIDTH - 1) + kk + row0, nrows), u_cols] * cw_ref[kk:kk + 1, w_cols]
        return y

    def step(last):
        h = h_scr[...]
        hm = tm // FFN_ROW_SPLIT
        parts = [None] * FFN_ROW_SPLIT
        uv = slice(0, MXU_COLS)
        ug = slice(MXU_COLS, 2 * MXU_COLS)
        n_chunks = len(u_scrs)
        for c, u_scr in enumerate(u_scrs):
            cs = slice(c * MXU_COLS, (c + 1) * MXU_COLS)
            if c < n_chunks - 1:
                u_scr[:, uv] = jnp.dot(h, wv_ref[:, cs], preferred_element_type=f32)
                u_scr[:, ug] = jnp.dot(h, wg_ref[:, cs], preferred_element_type=f32)
            else:
                wv, wg = wv_ref[:, cs], wg_ref[:, cs]
                r0 = 0
                for m in range(FFN_ROW_SPLIT):
                    r1 = H + (m + 1) * hm
                    u_scr[r0:r1, uv] = jnp.dot(h_scr[r0:r1, :], wv, preferred_element_type=f32)
                    u_scr[r0:r1, ug] = jnp.dot(h_scr[r0:r1, :], wg, preferred_element_type=f32)
                    r0 = r1
            for m in range(FFN_ROW_SPLIT):
                val = conv(u_scr, m * hm, hm, uv, cs, cwv_ref, cbv_ref)
                gate = conv(u_scr, m * hm, hm, ug, cs, cwg_ref, cbg_ref)
                a = (_gelu_tanh(gate) * val).astype(bf16)
                d = jnp.dot(a, wd_ref[cs, :], preferred_element_type=f32)
                parts[m] = d if parts[m] is None else parts[m] + d
        for m in range(FFN_ROW_SPLIT):
            rows = slice(m * hm, (m + 1) * hm)
            acc = o_ref[rows, :] + parts[m]
            if last:
                acc = x_ref[rows, :] + _rmsnorm(acc, gpost_ref[...])
            o_ref[rows, :] = acc

    pl.when(j < pl.num_programs(1) - 1)(functools.partial(step, False))
    pl.when(j == pl.num_programs(1) - 1)(functools.partial(step, True))


def _ffn(x1, g_pre, w_up, conv_w, conv_b, w_down, g_post, seq):
    T = x1.shape[0]
    tm, tn = min(FFN_TM, seq), FFN_TN
    nj = D_FF // tn
    hb = tm // FFN_HALO
    kern = functools.partial(_ffn_kernel, tiles_per_seq=seq // tm)
    return pl.pallas_call(
        kern,
        grid=(T // tm, nj),
        in_specs=[
            pl.BlockSpec((tm, D_MODEL), lambda i, j: (i, 0)),
            pl.BlockSpec((FFN_HALO, D_MODEL), lambda i, j: (jnp.maximum(i * hb - 1, 0), 0)),
            pl.BlockSpec((1, D_MODEL), lambda i, j: (0, 0)),
            pl.BlockSpec((D_MODEL, tn), lambda i, j: (0, j)),
            pl.BlockSpec((D_MODEL, tn), lambda i, j: (0, nj + j)),
            pl.BlockSpec((CONV_WIDTH, tn), lambda i, j: (0, j)),
            pl.BlockSpec((CONV_WIDTH, tn), lambda i, j: (0, nj + j)),
            pl.BlockSpec((1, tn), lambda i, j: (0, j)),
            pl.BlockSpec((1, tn), lambda i, j: (0, nj + j)),
            pl.BlockSpec((tn, D_MODEL), lambda i, j: (j, 0)),
            pl.BlockSpec((1, D_MODEL), lambda i, j: (0, 0)),
        ],
        out_specs=pl.BlockSpec((tm, D_MODEL), lambda i, j: (i, 0)),
        out_shape=jax.ShapeDtypeStruct((T, D_MODEL), f32),
        scratch_shapes=[
            pltpu.VMEM((tm + FFN_HALO, D_MODEL), bf16),
        ] + [pltpu.VMEM((tm + FFN_HALO, 2 * MXU_COLS), f32)] * (tn // MXU_COLS),
        compiler_params=_params(("parallel", "arbitrary")),
        name="ffn",
    )(x1, x1, g_pre, w_up, w_up, conv_w, conv_w, conv_b, conv_b, w_down, g_post)


def _rotary_tables(seq):
    d = RET_DK
    inv = 1.0 / (ROPE_BASE ** (jnp.arange(0, d, 2, dtype=f32) / d))
    ang = jnp.arange(seq, dtype=jnp.int32).astype(f32)[:, None] * inv[None, :]
    cos, sin = jnp.cos(ang), jnp.sin(ang)
    tab = jnp.concatenate([cos, cos, -sin, sin], axis=-1)
    return jnp.stack([tab, tab * (d ** -0.5)])


def kernel(x, g_pre_mix, w_in, w_ret_o, w_swa_o, w_out, swa_sinks, g_post_mix, g_pre_ffn, w_up, conv_w, conv_b,
           w_down, g_post_ffn):
    B, S, D = x.shape
    depth = w_in.shape[0]
    assert D == D_MODEL and w_in.shape[1:] == (D_MODEL, IN_WIDTH) and w_up.shape[1:] == (D_MODEL, 2 * D_FF)
    assert S % RET_CHUNK == 0 and S % SWA_BLOCK == 0
    assert S % min(INPROJ_TM, S) == 0 and S % min(FFN_TM, S) == 0 and (B * S) % min(MIX_TM, B * S) == 0
    rot = _rotary_tables(S)
    x2 = x.reshape(B * S, D)
    for l in range(depth):
        proj = _inproj(x2, g_pre_mix[l][None], w_in[l].astype(bf16), rot, S)
        ret = _retention(proj, B, S)
        swa = _swa(proj, swa_sinks[l].astype(f32), B, S)
        x2 = _mix(ret, swa, proj, w_ret_o[l].astype(bf16), w_swa_o[l].astype(bf16), w_out[l].astype(bf16), x2,
                  g_post_mix[l][None])
        x2 = _ffn(x2, g_pre_ffn[l][None], w_up[l].astype(bf16), conv_w[l], conv_b[l][None], w_down[l].astype(bf16),
                  g_post_ffn[l][None], S)
    return x2.reshape(B, S, D)
```

```python
import functools

import jax
import jax.numpy as jnp
from jax import lax
from jax.experimental import pallas as pl
from jax.experimental.pallas import tpu as pltpu

D_MODEL = 2048
RET_HEADS = 8
RET_DK = 128
RET_DV = 256
RET_CHUNK = 256
ROPE_BASE = 10000.0
SWA_Q_HEADS = 16
SWA_KV_HEADS = 4
SWA_HEAD_DIM = 64
SWA_BLOCK = 128
D_FF = 5632
CONV_WIDTH = 3
RMS_EPS = 1e-6

RET_QK_W = RET_HEADS * RET_DK
RET_V_W = RET_HEADS * RET_DV
SWA_Q_W = SWA_Q_HEADS * SWA_HEAD_DIM
SWA_KV_W = SWA_KV_HEADS * SWA_HEAD_DIM

COL_RQ = 0
COL_RK = COL_RQ + RET_QK_W
COL_RV = COL_RK + RET_QK_W
COL_RG = COL_RV + RET_V_W
COL_SQ = COL_RG + RET_V_W
COL_SK = COL_SQ + SWA_Q_W
COL_SV = COL_SK + SWA_KV_W
COL_GR = COL_SV + SWA_KV_W
COL_GS = COL_GR + D_MODEL
IN_WIDTH = COL_GS + D_MODEL

V7X_VMEM_LIMIT_BYTES = 60 * 1024 * 1024
LANES = 128
BF16_SUBLANES = 16
MXU_COLS = 256

NORM_ROWS = 32
NORM_UNROLL = 4
INPROJ_TM = 1024
INPROJ_TN = 1024
INPROJ_ROW_SPLIT = 4
MIX_TM = 512
MIX_TN = 1024
MIX_GATE_W = 512
MIX_STEPS = D_MODEL // MIX_TN
MIX_ROW_SPLIT = 2
FFN_TM = 1024
FFN_TN = 512
FFN_HALO = BF16_SUBLANES
FFN_ROW_SPLIT = 4
RET_HEADS_PER_STEP = 2
SWA_KV_PER_STEP = 2
SWA_UNROLL = 8

f32 = jnp.float32
bf16 = jnp.bfloat16


def _params(semantics):
    return pltpu.CompilerParams(dimension_semantics=semantics, vmem_limit_bytes=V7X_VMEM_LIMIT_BYTES)


def _rmsnorm(x, g):
    ms = jnp.mean(x * x, axis=-1, keepdims=True)
    return x * lax.rsqrt(ms + RMS_EPS) * g


def _norm_rows_to(x_ref, g_ref, dst_ref, dst_off, rows):
    g = g_ref[...]
    group = NORM_ROWS * NORM_UNROLL

    def step(c, carry):
        base = pl.multiple_of(c * group, group)
        for k in range(NORM_UNROLL):
            r = base + k * NORM_ROWS
            dst_ref[pl.ds(dst_off + r, NORM_ROWS), :] = (
                _rmsnorm(x_ref[pl.ds(r, NORM_ROWS), :], g).astype(dst_ref.dtype))
        return carry

    lax.fori_loop(0, rows // group, step, 0)


def _residual_norm_rows(x_ref, y_ref, g_ref, o_ref, rows):
    g = g_ref[...]
    group = NORM_ROWS * NORM_UNROLL

    def step(c, carry):
        base = pl.multiple_of(c * group, group)
        rows_k = [pl.ds(base + k * NORM_ROWS, NORM_ROWS) for k in range(NORM_UNROLL)]
        scales = []
        for r in rows_k:
            y = y_ref[r, :]
            scales.append(lax.rsqrt(jnp.mean(y * y, axis=-1, keepdims=True) + RMS_EPS))
        for r, s in zip(rows_k, scales):
            o_ref[r, :] = x_ref[r, :] + y_ref[r, :] * s * g
        return carry

    lax.fori_loop(0, rows // group, step, 0)


def _inproj_tile_kinds():
    starts = ((COL_RQ, "rot"), (COL_RK, "rot"), (COL_RV, "cast"), (COL_RG, "silu"), (COL_SQ, "cast"),
              (COL_GR, "sig"))
    chunk_kinds = [[k for c0, k in starts if c0 <= col][-1] for col in range(0, IN_WIDTH, MXU_COLS)]
    per_tile = INPROJ_TN // MXU_COLS
    return [tuple(chunk_kinds[t:t + per_tile]) for t in range(0, len(chunk_kinds), per_tile)]


_INPROJ_TILE_KINDS = _inproj_tile_kinds()
_T_RK = COL_RK // INPROJ_TN
assert COL_RK % INPROJ_TN == 0 and COL_RV % INPROJ_TN == 0


def _inproj_kernel(x_ref, g_ref, w_ref, rot_ref, o_ref, h_scr):
    j = pl.program_id(1)
    tm = x_ref.shape[0]

    @pl.when(j == 0)
    def _():
        _norm_rows_to(x_ref, g_ref, h_scr, 0, tm)

    def rotary(acc, rows):
        cos = rot_ref[rows, 0:RET_DK]
        sin = rot_ref[rows, RET_DK:2 * RET_DK]
        heads = []
        for hh in range(acc.shape[1] // RET_DK):
            xh = acc[:, hh * RET_DK:(hh + 1) * RET_DK]
            heads.append(xh * cos + pltpu.roll(xh, RET_DK // 2, axis=1) * sin)
        return jnp.concatenate(heads, axis=1)

    epilogues = {
        "rot": rotary,
        "cast": lambda acc, rows: acc,
        "silu": lambda acc, rows: acc * jax.nn.sigmoid(acc),
        "sig": lambda acc, rows: jax.nn.sigmoid(acc),
    }

    def tile(kinds):
        hm = tm // INPROJ_ROW_SPLIT
        for c, kind in enumerate(kinds):
            cs = slice(c * MXU_COLS, (c + 1) * MXU_COLS)
            w = w_ref[:, cs]
            for m in range(INPROJ_ROW_SPLIT):
                rows = slice(m * hm, (m + 1) * hm)
                acc = jnp.dot(h_scr[rows, :], w, preferred_element_type=f32)
                o_ref[rows, cs] = epilogues[kind](acc, rows).astype(o_ref.dtype)

    for kinds in sorted(set(_INPROJ_TILE_KINDS)):
        tiles = [t for t, k in enumerate(_INPROJ_TILE_KINDS) if k == kinds]
        cond = functools.reduce(lambda a, b: a | b, [j == t for t in tiles])
        pl.when(cond)(functools.partial(tile, kinds))


def _inproj(x2, g, w_in, rot, seq):
    T = x2.shape[0]
    tm, tn = INPROJ_TM, INPROJ_TN
    tm = min(tm, seq)
    spt = seq // tm
    grid = (T // tm, len(_INPROJ_TILE_KINDS))
    return pl.pallas_call(
        _inproj_kernel,
        grid=grid,
        in_specs=[
            pl.BlockSpec((tm, D_MODEL), lambda i, j: (i, 0)),
            pl.BlockSpec((1, D_MODEL), lambda i, j: (0, 0)),
            pl.BlockSpec((D_MODEL, tn), lambda i, j: (0, j)),
            pl.BlockSpec((None, tm, 2 * RET_DK), lambda i, j: (jnp.where(j >= _T_RK, 1, 0), i % spt, 0)),
        ],
        out_specs=pl.BlockSpec((tm, tn), lambda i, j: (i, j)),
        out_shape=jax.ShapeDtypeStruct((T, IN_WIDTH), bf16),
        scratch_shapes=[pltpu.VMEM((tm, D_MODEL), bf16)],
        compiler_params=_params(("parallel", "arbitrary")),
        name="inproj",
    )(x2, g, w_in, rot)


def _retention_kernel(q_ref, k_ref, v_ref, gate_ref, dmat_ref, xi_ref, zeta_ref, cdec_ref, o_ref, kv_scr, st_scr):
    S = q_ref.shape[0]
    C = RET_CHUNK
    N = S // C
    heads = range(RET_HEADS_PER_STEP)
    qk = [slice(hh * RET_DK, (hh + 1) * RET_DK) for hh in heads]
    vv = [slice(hh * RET_DV, (hh + 1) * RET_DV) for hh in heads]
    q3 = [q_ref[:, qk[hh]].reshape(N, C, RET_DK) for hh in heads]
    k3 = [k_ref[:, qk[hh]].reshape(N, C, RET_DK) for hh in heads]
    v3 = [v_ref[:, vv[hh]].reshape(N, C, RET_DV) for hh in heads]

    for hh in heads:
        kz = (k3[hh].astype(f32) * zeta_ref[hh][None]).astype(bf16)
        kv_scr[hh] = jnp.einsum("ndk,nkv->ndv", jnp.swapaxes(kz, 1, 2), v3[hh], preferred_element_type=f32)

    cdec = [cdec_ref[hh] for hh in heads]

    def scan(n, states):
        for hh in heads:
            st_scr[hh, n] = states[hh].astype(bf16)
        return tuple(states[hh] * cdec[hh] + kv_scr[hh, n] for hh in heads)

    lax.fori_loop(0, N, scan, tuple(jnp.zeros((RET_DK, RET_DV), f32) for _ in heads))

    for hh in heads:
        sc = jnp.einsum("nqd,nkd->nqk", q3[hh], k3[hh], preferred_element_type=f32) * dmat_ref[hh][None]
        inner = jnp.einsum("nqk,nkv->nqv", sc.astype(bf16), v3[hh], preferred_element_type=f32)
        cross = jnp.einsum("nqd,ndv->nqv", q3[hh], st_scr[hh], preferred_element_type=f32) * xi_ref[hh][None]
        o = inner + cross
        o = o * lax.rsqrt(jnp.mean(o * o, axis=-1, keepdims=True) + RMS_EPS)
        o = gate_ref[:, vv[hh]].astype(f32).reshape(N, C, RET_DV) * o
        o_ref[:, vv[hh]] = o.reshape(S, RET_DV).astype(o_ref.dtype)


def _retention_tables():
    C = RET_CHUNK
    log_gamma = jnp.log(1.0 - 2.0 ** (-5.0 - jnp.arange(RET_HEADS, dtype=f32)))
    idx = jnp.arange(C, dtype=f32)
    rel = idx[:, None] - idx[None, :]
    dmat = jnp.where(rel[None] >= 0, jnp.exp(log_gamma[:, None, None] * jnp.maximum(rel, 0.0)[None]), 0.0)
    xi = jnp.exp(log_gamma[:, None] * (idx + 1.0))
    zeta = jnp.exp(log_gamma[:, None] * (C - 1.0 - idx))
    cdec = jnp.exp(log_gamma * C)
    xi_b = jnp.broadcast_to(xi[:, :, None], (RET_HEADS, C, RET_DV))
    zeta_b = jnp.broadcast_to(zeta[:, :, None], (RET_HEADS, C, RET_DK))
    cdec_b = jnp.broadcast_to(cdec[:, None, None], (RET_HEADS, 1, RET_DV))
    return dmat.astype(f32), xi_b.astype(f32), zeta_b.astype(f32), cdec_b.astype(f32)


def _retention(proj, batch, seq):
    T = proj.shape[0]
    C = RET_CHUNK
    N = seq // C
    dmat, xi_b, zeta_b, cdec_b = _retention_tables()
    hp = RET_HEADS_PER_STEP
    qk_w, v_w = hp * RET_DK, hp * RET_DV
    return pl.pallas_call(
        _retention_kernel,
        grid=(batch, RET_HEADS // hp),
        in_specs=[
            pl.BlockSpec((seq, qk_w), lambda b, h: (b, COL_RQ // qk_w + h)),
            pl.BlockSpec((seq, qk_w), lambda b, h: (b, COL_RK // qk_w + h)),
            pl.BlockSpec((seq, v_w), lambda b, h: (b, COL_RV // v_w + h)),
            pl.BlockSpec((seq, v_w), lambda b, h: (b, COL_RG // v_w + h)),
            pl.BlockSpec((hp, C, C), lambda b, h: (h, 0, 0)),
            pl.BlockSpec((hp, C, RET_DV), lambda b, h: (h, 0, 0)),
            pl.BlockSpec((hp, C, RET_DK), lambda b, h: (h, 0, 0)),
            pl.BlockSpec((hp, 1, RET_DV), lambda b, h: (h, 0, 0)),
        ],
        out_specs=pl.BlockSpec((seq, v_w), lambda b, h: (b, h)),
        out_shape=jax.ShapeDtypeStruct((T, RET_V_W), bf16),
        scratch_shapes=[pltpu.VMEM((hp, N, RET_DK, RET_DV), f32), pltpu.VMEM((hp, N, RET_DK, RET_DV), bf16)],
        compiler_params=_params(("parallel", "parallel")),
        name="retention",
    )(proj, proj, proj, proj, dmat, xi_b, zeta_b, cdec_b)


KV_PER_STEP = SWA_KV_PER_STEP
Q_PER_KV = SWA_Q_HEADS // SWA_KV_HEADS
SWA_Q_STEP_W = KV_PER_STEP * Q_PER_KV * SWA_HEAD_DIM
assert KV_PER_STEP * SWA_HEAD_DIM == LANES and Q_PER_KV * SWA_HEAD_DIM == 2 * LANES


def _swa_kernel(sink_ref, bias_ref, q_ref, k_ref, v_ref, o_ref, klo, khi, vlo, vhi):
    S = q_ref.shape[0]
    C = SWA_BLOCK
    gp = pl.program_id(1)
    half = SWA_HEAD_DIM

    lane_s = lax.broadcasted_iota(jnp.int32, (S, LANES), 1)
    low_s = lane_s < half

    def prep(src_ref, lo_scr, hi_scr, scale, with_ones):
        w = src_ref[...].astype(f32) * scale
        r = pltpu.roll(w, half, axis=1)
        zeros = jnp.zeros((C, lo_scr.shape[2]), bf16)
        for t in range(KV_PER_STEP):
            lo_scr[t, 0:C, :] = zeros
            hi_scr[t, 0:C, :] = zeros
        lo_scr[0, C:C + S, 0:LANES] = jnp.where(low_s, w, 0.0).astype(bf16)
        hi_scr[0, C:C + S, 0:LANES] = jnp.where(low_s, 0.0, r).astype(bf16)
        lo_scr[1, C:C + S, 0:LANES] = jnp.where(low_s, r, 0.0).astype(bf16)
        hi_scr[1, C:C + S, 0:LANES] = jnp.where(low_s, 0.0, w).astype(bf16)
        if with_ones:
            for t in range(KV_PER_STEP):
                lo_scr[t, C:C + S, LANES:2 * LANES] = jnp.where(low_s, 1.0, 0.0).astype(bf16)
                hi_scr[t, C:C + S, LANES:2 * LANES] = jnp.where(low_s, 0.0, 1.0).astype(bf16)

    prep(k_ref, klo, khi, SWA_HEAD_DIM ** -0.5, False)
    prep(v_ref, vlo, vhi, 1.0, True)

    rows = 2 * C
    win = 2 * C
    first_pair = lax.broadcasted_iota(jnp.int32, (rows, 1), 0) < C
    low_o = lax.broadcasted_iota(jnp.int32, (rows, LANES), 1) < half
    nt = (((1,), (1,)), ((), ()))

    def softmax_parts(s, sink, bias):
        s = s + bias
        m = jnp.maximum(jnp.max(s, axis=-1, keepdims=True), sink)
        return jnp.exp(s - m).astype(bf16), jnp.exp(sink - m)

    def block(n, carry):
        r0 = pl.multiple_of(n * C, C)
        bias = bias_ref[jnp.minimum(n, 1)]
        for t in range(KV_PER_STEP):
            c0 = t * Q_PER_KV * SWA_HEAD_DIM
            hbase = gp * (KV_PER_STEP * Q_PER_KV) + t * Q_PER_KV
            qs = jnp.concatenate([q_ref[pl.ds(r0, C), c0:c0 + LANES],
                                  q_ref[pl.ds(r0, C), c0 + LANES:c0 + 2 * LANES]], axis=0)
            s_e = lax.dot_general(qs, klo[t, pl.ds(r0, win), :], nt, preferred_element_type=f32)
            s_o = lax.dot_general(qs, khi[t, pl.ds(r0, win), :], nt, preferred_element_type=f32)
            sink_e = jnp.where(first_pair, sink_ref[hbase + 0], sink_ref[hbase + 2])
            sink_o = jnp.where(first_pair, sink_ref[hbase + 1], sink_ref[hbase + 3])
            p_e, z_e = softmax_parts(s_e, sink_e, bias)
            p_o, z_o = softmax_parts(s_o, sink_o, bias)
            pv = (jnp.dot(p_e, vlo[t, pl.ds(r0, win), :], preferred_element_type=f32)
                  + jnp.dot(p_o, vhi[t, pl.ds(r0, win), :], preferred_element_type=f32))
            den = pv[:, LANES:2 * LANES] + jnp.where(low_o, z_e, z_o)
            o = pv[:, 0:LANES] / den
            o_ref[pl.ds(r0, C), c0:c0 + LANES] = o[0:C].astype(o_ref.dtype)
            o_ref[pl.ds(r0, C), c0 + LANES:c0 + 2 * LANES] = o[C:2 * C].astype(o_ref.dtype)
        return carry

    lax.fori_loop(0, S // C, block, 0, unroll=SWA_UNROLL)


def _swa_bias():
    C = SWA_BLOCK
    row_i = jnp.arange(2 * C)[:, None] % C
    col_j = jnp.arange(2 * C)[None, :]
    band = (col_j > row_i) & (col_j <= row_i + C)
    first = band & (col_j >= C)
    return jnp.where(jnp.stack([first, band]), 0.0, -jnp.inf).astype(f32)


def _swa(proj, sinks, batch, seq):
    T = proj.shape[0]
    steps = SWA_KV_HEADS // KV_PER_STEP
    k_scr = pltpu.VMEM((KV_PER_STEP, seq + SWA_BLOCK, LANES), bf16)
    v_scr = pltpu.VMEM((KV_PER_STEP, seq + SWA_BLOCK, 2 * LANES), bf16)
    bias = _swa_bias()
    return pl.pallas_call(
        _swa_kernel,
        grid=(batch, steps),
        in_specs=[
            pl.BlockSpec(memory_space=pltpu.SMEM),
            pl.BlockSpec(bias.shape, lambda b, g: (0, 0, 0)),
            pl.BlockSpec((seq, SWA_Q_STEP_W), lambda b, g: (b, COL_SQ // SWA_Q_STEP_W + g)),
            pl.BlockSpec((seq, LANES), lambda b, g: (b, COL_SK // LANES + g)),
            pl.BlockSpec((seq, LANES), lambda b, g: (b, COL_SV // LANES + g)),
        ],
        out_specs=pl.BlockSpec((seq, SWA_Q_STEP_W), lambda b, g: (b, g)),
        out_shape=jax.ShapeDtypeStruct((T, SWA_Q_W), bf16),
        scratch_shapes=[k_scr, k_scr, v_scr, v_scr],
        compiler_params=_params(("parallel", "parallel")),
        name="swa",
    )(sinks, bias, proj, proj, proj)


def _mix_kernel(ret_ref, swa_ref, *refs):
    n_gate = MIX_TN // MIX_GATE_W
    gr_refs, gs_refs = refs[:n_gate], refs[n_gate:2 * n_gate]
    wr_ref, ws_ref, wo_ref, x_ref, g_ref, o_ref = refs[2 * n_gate:]
    j = pl.program_id(1)
    nj = MIX_STEPS
    hm = o_ref.shape[0] // MIX_ROW_SPLIT

    def step(first, last):
        for m in range(MIX_ROW_SPLIT):
            rows = slice(m * hm, (m + 1) * hm)
            part = None
            for c in range(n_gate):
                cs = slice(c * MIX_GATE_W, (c + 1) * MIX_GATE_W)
                ret_out = jnp.dot(ret_ref[rows, :], wr_ref[:, cs], preferred_element_type=f32)
                swa_out = jnp.dot(swa_ref[rows, :], ws_ref[:, cs], preferred_element_type=f32)
                mixed = gr_refs[c][rows, :].astype(f32) * ret_out + gs_refs[c][rows, :].astype(f32) * swa_out
                d = jnp.dot(mixed.astype(bf16), wo_ref[cs, :], preferred_element_type=f32)
                part = d if part is None else part + d
            acc = part if first else o_ref[rows, :] + part
            if last:
                acc = x_ref[rows, :] + _rmsnorm(acc, g_ref[...])
            o_ref[rows, :] = acc

    pl.when(j == 0)(functools.partial(step, True, nj == 1))
    if nj > 2:
        pl.when((j > 0) & (j < nj - 1))(functools.partial(step, False, False))
    if nj > 1:
        pl.when(j == nj - 1)(functools.partial(step, False, True))


def _mix(ret, swa, proj, w_ret_o, w_swa_o, w_out, x2, g):
    T = x2.shape[0]
    tm, tn = min(MIX_TM, T), MIX_TN
    nj = D_MODEL // tn
    gw = MIX_GATE_W
    n_gate = tn // gw

    def gate_specs(col0):
        return [pl.BlockSpec((tm, gw), functools.partial(lambda i, j, c: (i, col0 // gw + j * n_gate + c), c=c))
                for c in range(n_gate)]

    return pl.pallas_call(
        _mix_kernel,
        grid=(T // tm, nj),
        in_specs=[
            pl.BlockSpec((tm, RET_V_W), lambda i, j: (i, 0)),
            pl.BlockSpec((tm, SWA_Q_W), lambda i, j: (i, 0)),
            *gate_specs(COL_GR),
            *gate_specs(COL_GS),
            pl.BlockSpec((RET_V_W, tn), lambda i, j: (0, j)),
            pl.BlockSpec((SWA_Q_W, tn), lambda i, j: (0, j)),
            pl.BlockSpec((tn, D_MODEL), lambda i, j: (j, 0)),
            pl.BlockSpec((tm, D_MODEL), lambda i, j: (i, 0)),
            pl.BlockSpec((1, D_MODEL), lambda i, j: (0, 0)),
        ],
        out_specs=pl.BlockSpec((tm, D_MODEL), lambda i, j: (i, 0)),
        out_shape=jax.ShapeDtypeStruct((T, D_MODEL), f32),
        compiler_params=_params(("parallel", "arbitrary")),
        name="mix",
    )(ret, swa, *([proj] * (2 * n_gate)), w_ret_o, w_swa_o, w_out, x2, g)


def _gelu_tanh(x):
    return x * (0.5 * (1.0 + jnp.tanh(0.7978845608028654 * (x + 0.044715 * (x * x * x)))))


def _ffn_kernel(x_ref, halo_ref, gpre_ref, wv_ref, wg_ref, cwv_ref, cwg_ref, cbv_ref, cbg_ref, wd_ref, gpost_ref,
                o_ref, h_scr, *u_scrs, tiles_per_seq):
    i = pl.program_id(0)
    j = pl.program_id(1)
    tm = x_ref.shape[0]
    tn = wv_ref.shape[1]
    H = FFN_HALO

    @pl.when(j == 0)
    def _():
        halo = jnp.where(i % tiles_per_seq == 0, 0.0, _rmsnorm(halo_ref[...], gpre_ref[...]))
        h_scr[0:H, :] = halo.astype(bf16)
        _norm_rows_to(x_ref, gpre_ref, h_scr, H, tm)
        o_ref[...] = jnp.zeros_like(o_ref)

    def conv(u_scr, row0, nrows, u_cols, w_cols, cw_ref, cb_ref):
        y = cb_ref[:, w_cols]
        for kk in range(CONV_WIDTH):
            y = y + u_scr[pl.ds(H - (CONV_WIDTH - 1) + kk + row0, nrows), u_cols] * cw_ref[kk:kk + 1, w_cols]
        return y

    def step(last):
        h = h_scr[...]
        hm = tm // FFN_ROW_SPLIT
        parts = [None] * FFN_ROW_SPLIT
        uv = slice(0, MXU_COLS)
        ug = slice(MXU_COLS, 2 * MXU_COLS)
        n_chunks = len(u_scrs)
        for c, u_scr in enumerate(u_scrs):
            cs = slice(c * MXU_COLS, (c + 1) * MXU_COLS)
            if c < n_chunks - 1:
                u_scr[:, uv] = jnp.dot(h, wv_ref[:, cs], preferred_element_type=f32)
                u_scr[:, ug] = jnp.dot(h, wg_ref[:, cs], preferred_element_type=f32)
            else:
                wv, wg = wv_ref[:, cs], wg_ref[:, cs]
                r0 = 0
                for m in range(FFN_ROW_SPLIT):
                    r1 = H + (m + 1) * hm
                    u_scr[r0:r1, uv] = jnp.dot(h_scr[r0:r1, :], wv, preferred_element_type=f32)
                    u_scr[r0:r1, ug] = jnp.dot(h_scr[r0:r1, :], wg, preferred_element_type=f32)
                    r0 = r1
            for m in range(FFN_ROW_SPLIT):
                val = conv(u_scr, m * hm, hm, uv, cs, cwv_ref, cbv_ref)
                gate = conv(u_scr, m * hm, hm, ug, cs, cwg_ref, cbg_ref)
                a = (_gelu_tanh(gate) * val).astype(bf16)
                d = jnp.dot(a, wd_ref[cs, :], preferred_element_type=f32)
                parts[m] = d if parts[m] is None else parts[m] + d
        for m in range(FFN_ROW_SPLIT):
            rows = slice(m * hm, (m + 1) * hm)
            acc = o_ref[rows, :] + parts[m]
            if last:
                acc = x_ref[rows, :] + _rmsnorm(acc, gpost_ref[...])
            o_ref[rows, :] = acc

    pl.when(j < pl.num_programs(1) - 1)(functools.partial(step, False))
    pl.when(j == pl.num_programs(1) - 1)(functools.partial(step, True))


def _ffn(x1, g_pre, w_up, conv_w, conv_b, w_down, g_post, seq):
    T = x1.shape[0]
    tm, tn = min(FFN_TM, seq), FFN_TN
    nj = D_FF // tn
    hb = tm // FFN_HALO
    kern = functools.partial(_ffn_kernel, tiles_per_seq=seq // tm)
    return pl.pallas_call(
        kern,
        grid=(T // tm, nj),
        in_specs=[
            pl.BlockSpec((tm, D_MODEL), lambda i, j: (i, 0)),
            pl.BlockSpec((FFN_HALO, D_MODEL), lambda i, j: (jnp.maximum(i * hb - 1, 0), 0)),
            pl.BlockSpec((1, D_MODEL), lambda i, j: (0, 0)),
            pl.BlockSpec((D_MODEL, tn), lambda i, j: (0, j)),
            pl.BlockSpec((D_MODEL, tn), lambda i, j: (0, nj + j)),
            pl.BlockSpec((CONV_WIDTH, tn), lambda i, j: (0, j)),
            pl.BlockSpec((CONV_WIDTH, tn), lambda i, j: (0, nj + j)),
            pl.BlockSpec((1, tn), lambda i, j: (0, j)),
            pl.BlockSpec((1, tn), lambda i, j: (0, nj + j)),
            pl.BlockSpec((tn, D_MODEL), lambda i, j: (j, 0)),
            pl.BlockSpec((1, D_MODEL), lambda i, j: (0, 0)),
        ],
        out_specs=pl.BlockSpec((tm, D_MODEL), lambda i, j: (i, 0)),
        out_shape=jax.ShapeDtypeStruct((T, D_MODEL), f32),
        scratch_shapes=[
            pltpu.VMEM((tm + FFN_HALO, D_MODEL), bf16),
        ] + [pltpu.VMEM((tm + FFN_HALO, 2 * MXU_COLS), f32)] * (tn // MXU_COLS),
        compiler_params=_params(("parallel", "arbitrary")),
        name="ffn",
    )(x1, x1, g_pre, w_up, w_up, conv_w, conv_w, conv_b, conv_b, w_down, g_post)


def _rotary_tables(seq):
    d = RET_DK
    inv = 1.0 / (ROPE_BASE ** (jnp.arange(0, d, 2, dtype=f32) / d))
    ang = jnp.arange(seq, dtype=jnp.int32).astype(f32)[:, None] * inv[None, :]
    cos, sin = jnp.cos(ang), jnp.sin(ang)
    tab = jnp.concatenate([cos, cos, -sin, sin], axis=-1)
    return jnp.stack([tab, tab * (d ** -0.5)])


def kernel(x, g_pre_mix, w_in, w_ret_o, w_swa_o, w_out, swa_sinks, g_post_mix, g_pre_ffn, w_up, conv_w, conv_b,
           w_down, g_post_ffn):
    B, S, D = x.shape
    depth = w_in.shape[0]
    assert D == D_MODEL and w_in.shape[1:] == (D_MODEL, IN_WIDTH) and w_up.shape[1:] == (D_MODEL, 2 * D_FF)
    assert S % RET_CHUNK == 0 and S % SWA_BLOCK == 0
    assert S % min(INPROJ_TM, S) == 0 and S % min(FFN_TM, S) == 0 and (B * S) % min(MIX_TM, B * S) == 0
    rot = _rotary_tables(S)
    x2 = x.reshape(B * S, D)
    for l in range(depth):
        proj = _inproj(x2, g_pre_mix[l][None], w_in[l].astype(bf16), rot, S)
        ret = _retention(proj, B, S)
        swa = _swa(proj, swa_sinks[l].astype(f32), B, S)
        x2 = _mix(ret, swa, proj, w_ret_o[l].astype(bf16), w_swa_o[l].astype(bf16), w_out[l].astype(bf16), x2,
                  g_post_mix[l][None])
        x2 = _ffn(x2, g_pre_ffn[l][None], w_up[l].astype(bf16), conv_w[l], conv_b[l][None], w_down[l].astype(bf16),
                  g_post_ffn[l][None], S)
    return x2.reshape(B, S, D)
```

```python
import functools

import jax
import jax.numpy as jnp
from jax import lax
from jax.experimental import pallas as pl
from jax.experimental.pallas import tpu as pltpu

D_MODEL = 2048
RET_HEADS = 8
RET_DK = 128
RET_DV = 256
RET_CHUNK = 256
ROPE_BASE = 10000.0
SWA_Q_HEADS = 16
SWA_KV_HEADS = 4
SWA_HEAD_DIM = 64
SWA_BLOCK = 128
D_FF = 5632
CONV_WIDTH = 3
RMS_EPS = 1e-6

RET_QK_W = RET_HEADS * RET_DK
RET_V_W = RET_HEADS * RET_DV
SWA_Q_W = SWA_Q_HEADS * SWA_HEAD_DIM
SWA_KV_W = SWA_KV_HEADS * SWA_HEAD_DIM

COL_RQ = 0
COL_RK = COL_RQ + RET_QK_W
COL_RV = COL_RK + RET_QK_W
COL_RG = COL_RV + RET_V_W
COL_SQ = COL_RG + RET_V_W
COL_SK = COL_SQ + SWA_Q_W
COL_SV = COL_SK + SWA_KV_W
COL_GR = COL_SV + SWA_KV_W
COL_GS = COL_GR + D_MODEL
IN_WIDTH = COL_GS + D_MODEL

V7X_VMEM_LIMIT_BYTES = 60 * 1024 * 1024
LANES = 128
BF16_SUBLANES = 16
MXU_COLS = 256

NORM_ROWS = 32
NORM_UNROLL = 4
INPROJ_TM = 1024
INPROJ_TN = 1024
INPROJ_ROW_SPLIT = 4
MIX_TM = 512
MIX_TN = 1024
MIX_GATE_W = 512
MIX_STEPS = D_MODEL // MIX_TN
MIX_ROW_SPLIT = 2
FFN_TM = 1024
FFN_TN = 512
FFN_HALO = BF16_SUBLANES
FFN_ROW_SPLIT = 4
RET_HEADS_PER_STEP = 2
SWA_KV_PER_STEP = 2
SWA_UNROLL = 8

f32 = jnp.float32
bf16 = jnp.bfloat16


def _params(semantics):
    return pltpu.CompilerParams(dimension_semantics=semantics, vmem_limit_bytes=V7X_VMEM_LIMIT_BYTES)


def _rmsnorm(x, g):
    ms = jnp.mean(x * x, axis=-1, keepdims=True)
    return x * lax.rsqrt(ms + RMS_EPS) * g


def _norm_rows_to(x_ref, g_ref, dst_ref, dst_off, rows):
    g = g_ref[...]
    group = NORM_ROWS * NORM_UNROLL

    def step(c, carry):
        base = pl.multiple_of(c * group, group)
        for k in range(NORM_UNROLL):
            r = base + k * NORM_ROWS
            dst_ref[pl.ds(dst_off + r, NORM_ROWS), :] = (
                _rmsnorm(x_ref[pl.ds(r, NORM_ROWS), :], g).astype(dst_ref.dtype))
        return carry

    lax.fori_loop(0, rows // group, step, 0)


def _inproj_tile_kinds():
    starts = ((COL_RQ, "rot"), (COL_RK, "rot"), (COL_RV, "cast"), (COL_RG, "silu"), (COL_SQ, "cast"),
              (COL_GR, "sig"))
    chunk_kinds = [[k for c0, k in starts if c0 <= col][-1] for col in range(0, IN_WIDTH, MXU_COLS)]
    per_tile = INPROJ_TN // MXU_COLS
    return [tuple(chunk_kinds[t:t + per_tile]) for t in range(0, len(chunk_kinds), per_tile)]


_INPROJ_TILE_KINDS = _inproj_tile_kinds()
_T_RK = COL_RK // INPROJ_TN
assert COL_RK % INPROJ_TN == 0 and COL_RV % INPROJ_TN == 0


def _inproj_kernel(x_ref, g_ref, w_ref, rot_ref, o_ref, h_scr):
    j = pl.program_id(1)
    tm = x_ref.shape[0]

    @pl.when(j == 0)
    def _():
        _norm_rows_to(x_ref, g_ref, h_scr, 0, tm)

    def rotary(acc, rows):
        cos = rot_ref[rows, 0:RET_DK]
        sin = rot_ref[rows, RET_DK:2 * RET_DK]
        heads = []
        for hh in range(acc.shape[1] // RET_DK):
            xh = acc[:, hh * RET_DK:(hh + 1) * RET_DK]
            heads.append(xh * cos + pltpu.roll(xh, RET_DK // 2, axis=1) * sin)
        return jnp.concatenate(heads, axis=1)

    epilogues = {
        "rot": rotary,
        "cast": lambda acc, rows: acc,
        "silu": lambda acc, rows: acc * jax.nn.sigmoid(acc),
        "sig": lambda acc, rows: jax.nn.sigmoid(acc),
    }

    def tile(kinds):
        hm = tm // INPROJ_ROW_SPLIT
        for c, kind in enumerate(kinds):
            cs = slice(c * MXU_COLS, (c + 1) * MXU_COLS)
            w = w_ref[:, cs]
            for m in range(INPROJ_ROW_SPLIT):
                rows = slice(m * hm, (m + 1) * hm)
                acc = jnp.dot(h_scr[rows, :], w, preferred_element_type=f32)
                o_ref[rows, cs] = epilogues[kind](acc, rows).astype(o_ref.dtype)

    for kinds in sorted(set(_INPROJ_TILE_KINDS)):
        tiles = [t for t, k in enumerate(_INPROJ_TILE_KINDS) if k == kinds]
        cond = functools.reduce(lambda a, b: a | b, [j == t for t in tiles])
        pl.when(cond)(functools.partial(tile, kinds))


def _inproj(x2, g, w_in, rot, seq):
    T = x2.shape[0]
    tm, tn = INPROJ_TM, INPROJ_TN
    tm = min(tm, seq)
    spt = seq // tm
    grid = (T // tm, len(_INPROJ_TILE_KINDS))
    return pl.pallas_call(
        _inproj_kernel,
        grid=grid,
        in_specs=[
            pl.BlockSpec((tm, D_MODEL), lambda i, j: (i, 0)),
            pl.BlockSpec((1, D_MODEL), lambda i, j: (0, 0)),
            pl.BlockSpec((D_MODEL, tn), lambda i, j: (0, j)),
            pl.BlockSpec((None, tm, 2 * RET_DK), lambda i, j: (jnp.where(j >= _T_RK, 1, 0), i % spt, 0)),
        ],
        out_specs=pl.BlockSpec((tm, tn), lambda i, j: (i, j)),
        out_shape=jax.ShapeDtypeStruct((T, IN_WIDTH), bf16),
        scratch_shapes=[pltpu.VMEM((tm, D_MODEL), bf16)],
        compiler_params=_params(("parallel", "arbitrary")),
        name="inproj",
    )(x2, g, w_in, rot)


def _retention_kernel(q_ref, k_ref, v_ref, gate_ref, dmat_ref, xi_ref, zeta_ref, cdec_ref, o_ref, kv_scr, st_scr):
    S = q_ref.shape[0]
    C = RET_CHUNK
    N = S // C
    heads = range(RET_HEADS_PER_STEP)
    qk = [slice(hh * RET_DK, (hh + 1) * RET_DK) for hh in heads]
    vv = [slice(hh * RET_DV, (hh + 1) * RET_DV) for hh in heads]
    q3 = [q_ref[:, qk[hh]].reshape(N, C, RET_DK) for hh in heads]
    k3 = [k_ref[:, qk[hh]].reshape(N, C, RET_DK) for hh in heads]
    v3 = [v_ref[:, vv[hh]].reshape(N, C, RET_DV) for hh in heads]

    for hh in heads:
        kz = (k3[hh].astype(f32) * zeta_ref[hh][None]).astype(bf16)
        kv_scr[hh] = jnp.einsum("ndk,nkv->ndv", jnp.swapaxes(kz, 1, 2), v3[hh], preferred_element_type=f32)

    cdec = [cdec_ref[hh] for hh in heads]

    def scan(n, states):
        for hh in heads:
            st_scr[hh, n] = states[hh].astype(bf16)
        return tuple(states[hh] * cdec[hh] + kv_scr[hh, n] for hh in heads)

    lax.fori_loop(0, N, scan, tuple(jnp.zeros((RET_DK, RET_DV), f32) for _ in heads))

    for hh in heads:
        sc = jnp.einsum("nqd,nkd->nqk", q3[hh], k3[hh], preferred_element_type=f32) * dmat_ref[hh][None]
        inner = jnp.einsum("nqk,nkv->nqv", sc.astype(bf16), v3[hh], preferred_element_type=f32)
        cross = jnp.einsum("nqd,ndv->nqv", q3[hh], st_scr[hh], preferred_element_type=f32) * xi_ref[hh][None]
        o = inner + cross
        o = o * lax.rsqrt(jnp.mean(o * o, axis=-1, keepdims=True) + RMS_EPS)
        o = gate_ref[:, vv[hh]].astype(f32).reshape(N, C, RET_DV) * o
        o_ref[:, vv[hh]] = o.reshape(S, RET_DV).astype(o_ref.dtype)


def _retention_tables():
    C = RET_CHUNK
    log_gamma = jnp.log(1.0 - 2.0 ** (-5.0 - jnp.arange(RET_HEADS, dtype=f32)))
    idx = jnp.arange(C, dtype=f32)
    rel = idx[:, None] - idx[None, :]
    dmat = jnp.where(rel[None] >= 0, jnp.exp(log_gamma[:, None, None] * jnp.maximum(rel, 0.0)[None]), 0.0)
    xi = jnp.exp(log_gamma[:, None] * (idx + 1.0))
    zeta = jnp.exp(log_gamma[:, None] * (C - 1.0 - idx))
    cdec = jnp.exp(log_gamma * C)
    xi_b = jnp.broadcast_to(xi[:, :, None], (RET_HEADS, C, RET_DV))
    zeta_b = jnp.broadcast_to(zeta[:, :, None], (RET_HEADS, C, RET_DK))
    cdec_b = jnp.broadcast_to(cdec[:, None, None], (RET_HEADS, 1, RET_DV))
    return dmat.astype(f32), xi_b.astype(f32), zeta_b.astype(f32), cdec_b.astype(f32)


def _retention(proj, batch, seq):
    T = proj.shape[0]
    C = RET_CHUNK
    N = seq // C
    dmat, xi_b, zeta_b, cdec_b = _retention_tables()
    hp = RET_HEADS_PER_STEP
    qk_w, v_w = hp * RET_DK, hp * RET_DV
    return pl.pallas_call(
        _retention_kernel,
        grid=(batch, RET_HEADS // hp),
        in_specs=[
            pl.BlockSpec((seq, qk_w), lambda b, h: (b, COL_RQ // qk_w + h)),
            pl.BlockSpec((seq, qk_w), lambda b, h: (b, COL_RK // qk_w + h)),
            pl.BlockSpec((seq, v_w), lambda b, h: (b, COL_RV // v_w + h)),
            pl.BlockSpec((seq, v_w), lambda b, h: (b, COL_RG // v_w + h)),
            pl.BlockSpec((hp, C, C), lambda b, h: (h, 0, 0)),
            pl.BlockSpec((hp, C, RET_DV), lambda b, h: (h, 0, 0)),
            pl.BlockSpec((hp, C, RET_DK), lambda b, h: (h, 0, 0)),
            pl.BlockSpec((hp, 1, RET_DV), lambda b, h: (h, 0, 0)),
        ],
        out_specs=pl.BlockSpec((seq, v_w), lambda b, h: (b, h)),
        out_shape=jax.ShapeDtypeStruct((T, RET_V_W), bf16),
        scratch_shapes=[pltpu.VMEM((hp, N, RET_DK, RET_DV), f32), pltpu.VMEM((hp, N, RET_DK, RET_DV), bf16)],
        compiler_params=_params(("parallel", "parallel")),
        name="retention",
    )(proj, proj, proj, proj, dmat, xi_b, zeta_b, cdec_b)


KV_PER_STEP = SWA_KV_PER_STEP
Q_PER_KV = SWA_Q_HEADS // SWA_KV_HEADS
SWA_Q_STEP_W = KV_PER_STEP * Q_PER_KV * SWA_HEAD_DIM
assert KV_PER_STEP * SWA_HEAD_DIM == LANES and Q_PER_KV * SWA_HEAD_DIM == 2 * LANES


def _swa_kernel(sink_ref, bias_ref, q_ref, k_ref, v_ref, o_ref, klo, khi, vlo, vhi):
    S = q_ref.shape[0]
    C = SWA_BLOCK
    gp = pl.program_id(1)
    half = SWA_HEAD_DIM

    lane_s = lax.broadcasted_iota(jnp.int32, (S, LANES), 1)
    low_s = lane_s < half

    def prep(src_ref, lo_scr, hi_scr, scale, with_ones):
        w = src_ref[...].astype(f32) * scale
        r = pltpu.roll(w, half, axis=1)
        zeros = jnp.zeros((C, lo_scr.shape[2]), bf16)
        for t in range(KV_PER_STEP):
            lo_scr[t, 0:C, :] = zeros
            hi_scr[t, 0:C, :] = zeros
        lo_scr[0, C:C + S, 0:LANES] = jnp.where(low_s, w, 0.0).astype(bf16)
        hi_scr[0, C:C + S, 0:LANES] = jnp.where(low_s, 0.0, r).astype(bf16)
        lo_scr[1, C:C + S, 0:LANES] = jnp.where(low_s, r, 0.0).astype(bf16)
        hi_scr[1, C:C + S, 0:LANES] = jnp.where(low_s, 0.0, w).astype(bf16)
        if with_ones:
            for t in range(KV_PER_STEP):
                lo_scr[t, C:C + S, LANES:2 * LANES] = jnp.where(low_s, 1.0, 0.0).astype(bf16)
                hi_scr[t, C:C + S, LANES:2 * LANES] = jnp.where(low_s, 0.0, 1.0).astype(bf16)

    prep(k_ref, klo, khi, SWA_HEAD_DIM ** -0.5, False)
    prep(v_ref, vlo, vhi, 1.0, True)

    rows = 2 * C
    win = 2 * C
    first_pair = lax.broadcasted_iota(jnp.int32, (rows, 1), 0) < C
    low_o = lax.broadcasted_iota(jnp.int32, (rows, LANES), 1) < half
    nt = (((1,), (1,)), ((), ()))

    def softmax_parts(s, sink, bias):
        s = s + bias
        m = jnp.maximum(jnp.max(s, axis=-1, keepdims=True), sink)
        return jnp.exp(s - m).astype(bf16), jnp.exp(sink - m)

    def block(n, carry):
        r0 = pl.multiple_of(n * C, C)
        bias = bias_ref[jnp.minimum(n, 1)]
        for t in range(KV_PER_STEP):
            c0 = t * Q_PER_KV * SWA_HEAD_DIM
            hbase = gp * (KV_PER_STEP * Q_PER_KV) + t * Q_PER_KV
            qs = jnp.concatenate([q_ref[pl.ds(r0, C), c0:c0 + LANES],
                                  q_ref[pl.ds(r0, C), c0 + LANES:c0 + 2 * LANES]], axis=0)
            s_e = lax.dot_general(qs, klo[t, pl.ds(r0, win), :], nt, preferred_element_type=f32)
            s_o = lax.dot_general(qs, khi[t, pl.ds(r0, win), :], nt, preferred_element_type=f32)
            sink_e = jnp.where(first_pair, sink_ref[hbase + 0], sink_ref[hbase + 2])
            sink_o = jnp.where(first_pair, sink_ref[hbase + 1], sink_ref[hbase + 3])
            p_e, z_e = softmax_parts(s_e, sink_e, bias)
            p_o, z_o = softmax_parts(s_o, sink_o, bias)
            pv = (jnp.dot(p_e, vlo[t, pl.ds(r0, win), :], preferred_element_type=f32)
                  + jnp.dot(p_o, vhi[t, pl.ds(r0, win), :], preferred_element_type=f32))
            den = pv[:, LANES:2 * LANES] + jnp.where(low_o, z_e, z_o)
            o = pv[:, 0:LANES] / den
            o_ref[pl.ds(r0, C), c0:c0 + LANES] = o[0:C].astype(o_ref.dtype)
            o_ref[pl.ds(r0, C), c0 + LANES:c0 + 2 * LANES] = o[C:2 * C].astype(o_ref.dtype)
        return carry

    lax.fori_loop(0, S // C, block, 0, unroll=SWA_UNROLL)


def _swa_bias():
    C = SWA_BLOCK
    row_i = jnp.arange(2 * C)[:, None] % C
    col_j = jnp.arange(2 * C)[None, :]
    band = (col_j > row_i) & (col_j <= row_i + C)
    first = band & (col_j >= C)
    return jnp.where(jnp.stack([first, band]), 0.0, -jnp.inf).astype(f32)


def _swa(proj, sinks, batch, seq):
    T = proj.shape[0]
    steps = SWA_KV_HEADS // KV_PER_STEP
    k_scr = pltpu.VMEM((KV_PER_STEP, seq + SWA_BLOCK, LANES), bf16)
    v_scr = pltpu.VMEM((KV_PER_STEP, seq + SWA_BLOCK, 2 * LANES), bf16)
    bias = _swa_bias()
    return pl.pallas_call(
        _swa_kernel,
        grid=(batch, steps),
        in_specs=[
            pl.BlockSpec(memory_space=pltpu.SMEM),
            pl.BlockSpec(bias.shape, lambda b, g: (0, 0, 0)),
            pl.BlockSpec((seq, SWA_Q_STEP_W), lambda b, g: (b, COL_SQ // SWA_Q_STEP_W + g)),
            pl.BlockSpec((seq, LANES), lambda b, g: (b, COL_SK // LANES + g)),
            pl.BlockSpec((seq, LANES), lambda b, g: (b, COL_SV // LANES + g)),
        ],
        out_specs=pl.BlockSpec((seq, SWA_Q_STEP_W), lambda b, g: (b, g)),
        out_shape=jax.ShapeDtypeStruct((T, SWA_Q_W), bf16),
        scratch_shapes=[k_scr, k_scr, v_scr, v_scr],
        compiler_params=_params(("parallel", "parallel")),
        name="swa",
    )(sinks, bias, proj, proj, proj)


def _mix_kernel(ret_ref, swa_ref, *refs):
    n_gate = MIX_TN // MIX_GATE_W
    gr_refs, gs_refs = refs[:n_gate], refs[n_gate:2 * n_gate]
    wr_ref, ws_ref, wo_ref, x_ref, g_ref, o_ref = refs[2 * n_gate:]
    j = pl.program_id(1)
    nj = MIX_STEPS
    hm = o_ref.shape[0] // MIX_ROW_SPLIT

    def step(first, last):
        for m in range(MIX_ROW_SPLIT):
            rows = slice(m * hm, (m + 1) * hm)
            part = None
            for c in range(n_gate):
                cs = slice(c * MIX_GATE_W, (c + 1) * MIX_GATE_W)
                ret_out = jnp.dot(ret_ref[rows, :], wr_ref[:, cs], preferred_element_type=f32)
                swa_out = jnp.dot(swa_ref[rows, :], ws_ref[:, cs], preferred_element_type=f32)
                mixed = gr_refs[c][rows, :].astype(f32) * ret_out + gs_refs[c][rows, :].astype(f32) * swa_out
                d = jnp.dot(mixed.astype(bf16), wo_ref[cs, :], preferred_element_type=f32)
                part = d if part is None else part + d
            acc = part if first else o_ref[rows, :] + part
            if last:
                acc = x_ref[rows, :] + _rmsnorm(acc, g_ref[...])
            o_ref[rows, :] = acc

    pl.when(j == 0)(functools.partial(step, True, nj == 1))
    if nj > 2:
        pl.when((j > 0) & (j < nj - 1))(functools.partial(step, False, False))
    if nj > 1:
        pl.when(j == nj - 1)(functools.partial(step, False, True))


def _mix(ret, swa, proj, w_ret_o, w_swa_o, w_out, x2, g):
    T = x2.shape[0]
    tm, tn = min(MIX_TM, T), MIX_TN
    nj = D_MODEL // tn
    gw = MIX_GATE_W
    n_gate = tn // gw

    def gate_specs(col0):
        return [pl.BlockSpec((tm, gw), functools.partial(lambda i, j, c: (i, col0 // gw + j * n_gate + c), c=c))
                for c in range(n_gate)]

    return pl.pallas_call(
        _mix_kernel,
        grid=(T // tm, nj),
        in_specs=[
            pl.BlockSpec((tm, RET_V_W), lambda i, j: (i, 0)),
            pl.BlockSpec((tm, SWA_Q_W), lambda i, j: (i, 0)),
            *gate_specs(COL_GR),
            *gate_specs(COL_GS),
            pl.BlockSpec((RET_V_W, tn), lambda i, j: (0, j)),
            pl.BlockSpec((SWA_Q_W, tn), lambda i, j: (0, j)),
            pl.BlockSpec((tn, D_MODEL), lambda i, j: (j, 0)),
            pl.BlockSpec((tm, D_MODEL), lambda i, j: (i, 0)),
            pl.BlockSpec((1, D_MODEL), lambda i, j: (0, 0)),
        ],
        out_specs=pl.BlockSpec((tm, D_MODEL), lambda i, j: (i, 0)),
        out_shape=jax.ShapeDtypeStruct((T, D_MODEL), f32),
        compiler_params=_params(("parallel", "arbitrary")),
        name="mix",
    )(ret, swa, *([proj] * (2 * n_gate)), w_ret_o, w_swa_o, w_out, x2, g)


def _gelu_tanh(x):
    return x * (0.5 * (1.0 + jnp.tanh(0.7978845608028654 * (x + 0.044715 * (x * x * x)))))


def _ffn_kernel(x_ref, halo_ref, gpre_ref, wv_ref, wg_ref, cwv_ref, cwg_ref, cbv_ref, cbg_ref, wd_ref, gpost_ref,
                o_ref, h_scr, *u_scrs, tiles_per_seq):
    i = pl.program_id(0)
    j = pl.program_id(1)
    tm = x_ref.shape[0]
    tn = wv_ref.shape[1]
    H = FFN_HALO

    @pl.when(j == 0)
    def _():
        halo = jnp.where(i % tiles_per_seq == 0, 0.0, _rmsnorm(halo_ref[...], gpre_ref[...]))
        h_scr[0:H, :] = halo.astype(bf16)
        _norm_rows_to(x_ref, gpre_ref, h_scr, H, tm)
        o_ref[...] = jnp.zeros_like(o_ref)

    def conv(u_scr, row0, nrows, u_cols, w_cols, cw_ref, cb_ref):
        y = cb_ref[:, w_cols]
        for kk in range(CONV_WIDTH):
            y = y + u_scr[pl.ds(H - (CONV_WIDTH - 1) + kk + row0, nrows), u_cols] * cw_ref[kk:kk + 1, w_cols]
        return y

    def step(last):
        h = h_scr[...]
        hm = tm // FFN_ROW_SPLIT
        parts = [None] * FFN_ROW_SPLIT
        uv = slice(0, MXU_COLS)
        ug = slice(MXU_COLS, 2 * MXU_COLS)
        n_chunks = len(u_scrs)
        for c, u_scr in enumerate(u_scrs):
            cs = slice(c * MXU_COLS, (c + 1) * MXU_COLS)
            if c < n_chunks - 1:
                u_scr[:, uv] = jnp.dot(h, wv_ref[:, cs], preferred_element_type=f32)
                u_scr[:, ug] = jnp.dot(h, wg_ref[:, cs], preferred_element_type=f32)
            else:
                wv, wg = wv_ref[:, cs], wg_ref[:, cs]
                r0 = 0
                for m in range(FFN_ROW_SPLIT):
                    r1 = H + (m + 1) * hm
                    u_scr[r0:r1, uv] = jnp.dot(h_scr[r0:r1, :], wv, preferred_element_type=f32)
                    u_scr[r0:r1, ug] = jnp.dot(h_scr[r0:r1, :], wg, preferred_element_type=f32)
                    r0 = r1
            for m in range(FFN_ROW_SPLIT):
                val = conv(u_scr, m * hm, hm, uv, cs, cwv_ref, cbv_ref)
                gate = conv(u_scr, m * hm, hm, ug, cs, cwg_ref, cbg_ref)
                a = (_gelu_tanh(gate) * val).astype(bf16)
                d = jnp.dot(a, wd_ref[cs, :], preferred_element_type=f32)
                parts[m] = d if parts[m] is None else parts[m] + d
        for m in range(FFN_ROW_SPLIT):
            rows = slice(m * hm, (m + 1) * hm)
            acc = o_ref[rows, :] + parts[m]
            if last:
                acc = x_ref[rows, :] + _rmsnorm(acc, gpost_ref[...])
            o_ref[rows, :] = acc

    pl.when(j < pl.num_programs(1) - 1)(functools.partial(step, False))
    pl.when(j == pl.num_programs(1) - 1)(functools.partial(step, True))


def _ffn(x1, g_pre, w_up, conv_w, conv_b, w_down, g_post, seq):
    T = x1.shape[0]
    tm, tn = min(FFN_TM, seq), FFN_TN
    nj = D_FF // tn
    hb = tm // FFN_HALO
    kern = functools.partial(_ffn_kernel, tiles_per_seq=seq // tm)
    return pl.pallas_call(
        kern,
        grid=(T // tm, nj),
        in_specs=[
            pl.BlockSpec((tm, D_MODEL), lambda i, j: (i, 0)),
            pl.BlockSpec((FFN_HALO, D_MODEL), lambda i, j: (jnp.maximum(i * hb - 1, 0), 0)),
            pl.BlockSpec((1, D_MODEL), lambda i, j: (0, 0)),
            pl.BlockSpec((D_MODEL, tn), lambda i, j: (0, j)),
            pl.BlockSpec((D_MODEL, tn), lambda i, j: (0, nj + j)),
            pl.BlockSpec((CONV_WIDTH, tn), lambda i, j: (0, j)),
            pl.BlockSpec((CONV_WIDTH, tn), lambda i, j: (0, nj + j)),
            pl.BlockSpec((1, tn), lambda i, j: (0, j)),
            pl.BlockSpec((1, tn), lambda i, j: (0, nj + j)),
            pl.BlockSpec((tn, D_MODEL), lambda i, j: (j, 0)),
            pl.BlockSpec((1, D_MODEL), lambda i, j: (0, 0)),
        ],
        out_specs=pl.BlockSpec((tm, D_MODEL), lambda i, j: (i, 0)),
        out_shape=jax.ShapeDtypeStruct((T, D_MODEL), f32),
        scratch_shapes=[
            pltpu.VMEM((tm + FFN_HALO, D_MODEL), bf16),
        ] + [pltpu.VMEM((tm + FFN_HALO, 2 * MXU_COLS), f32)] * (tn // MXU_COLS),
        compiler_params=_params(("parallel", "arbitrary")),
        name="ffn",
    )(x1, x1, g_pre, w_up, w_up, conv_w, conv_w, conv_b, conv_b, w_down, g_post)


def _rotary_tables(seq):
    d = RET_DK
    inv = 1.0 / (ROPE_BASE ** (jnp.arange(0, d, 2, dtype=f32) / d))
    ang = jnp.arange(seq, dtype=jnp.int32).astype(f32)[:, None] * inv[None, :]
    cos, sin = jnp.cos(ang), jnp.sin(ang)
    tab = jnp.concatenate([cos, cos, -sin, sin], axis=-1)
    return jnp.stack([tab, tab * (d ** -0.5)])


def kernel(x, g_pre_mix, w_in, w_ret_o, w_swa_o, w_out, swa_sinks, g_post_mix, g_pre_ffn, w_up, conv_w, conv_b,
           w_down, g_post_ffn):
    B, S, D = x.shape
    depth = w_in.shape[0]
    assert D == D_MODEL and w_in.shape[1:] == (D_MODEL, IN_WIDTH) and w_up.shape[1:] == (D_MODEL, 2 * D_FF)
    assert S % RET_CHUNK == 0 and S % SWA_BLOCK == 0
    assert S % min(INPROJ_TM, S) == 0 and S % min(FFN_TM, S) == 0 and (B * S) % min(MIX_TM, B * S) == 0
    rot = _rotary_tables(S)
    x2 = x.reshape(B * S, D)
    for l in range(depth):
        proj = _inproj(x2, g_pre_mix[l][None], w_in[l].astype(bf16), rot, S)
        ret = _retention(proj, B, S)
        swa = _swa(proj, swa_sinks[l].astype(f32), B, S)
        x2 = _mix(ret, swa, proj, w_ret_o[l].astype(bf16), w_swa_o[l].astype(bf16), w_out[l].astype(bf16), x2,
                  g_post_mix[l][None])
        x2 = _ffn(x2, g_pre_ffn[l][None], w_up[l].astype(bf16), conv_w[l], conv_b[l][None], w_down[l].astype(bf16),
                  g_post_ffn[l][None], S)
    return x2.reshape(B, S, D)
```

```python
import functools

import jax
import jax.numpy as jnp
from jax import lax
from jax.experimental import pallas as pl
from jax.experimental.pallas import tpu as pltpu

D_MODEL = 2048
RET_HEADS = 8
RET_DK = 128
RET_DV = 256
RET_CHUNK = 256
ROPE_BASE = 10000.0
SWA_Q_HEADS = 16
SWA_KV_HEADS = 4
SWA_HEAD_DIM = 64
SWA_BLOCK = 128
D_FF = 5632
CONV_WIDTH = 3
RMS_EPS = 1e-6

RET_QK_W = RET_HEADS * RET_DK
RET_V_W = RET_HEADS * RET_DV
SWA_Q_W = SWA_Q_HEADS * SWA_HEAD_DIM
SWA_KV_W = SWA_KV_HEADS * SWA_HEAD_DIM

COL_RQ = 0
COL_RK = COL_RQ + RET_QK_W
COL_RV = COL_RK + RET_QK_W
COL_RG = COL_RV + RET_V_W
COL_SQ = COL_RG + RET_V_W
COL_SK = COL_SQ + SWA_Q_W
COL_SV = COL_SK + SWA_KV_W
COL_GR = COL_SV + SWA_KV_W
COL_GS = COL_GR + D_MODEL
IN_WIDTH = COL_GS + D_MODEL

V7X_VMEM_LIMIT_BYTES = 60 * 1024 * 1024
LANES = 128
BF16_SUBLANES = 16
MXU_COLS = 256

NORM_ROWS = 32
NORM_UNROLL = 4
INPROJ_TM = 1024
INPROJ_TN = 1024
INPROJ_ROW_SPLIT = 4
MIX_TM = 512
MIX_TN = 2048
MIX_GATE_W = 512
MIX_STEPS = D_MODEL // MIX_TN
MIX_ROW_SPLIT = 2
FFN_TM = 1024
FFN_TN = 512
FFN_HALO = BF16_SUBLANES
FFN_ROW_SPLIT = 4
RET_HEADS_PER_STEP = 2
SWA_KV_PER_STEP = 2
SWA_UNROLL = 8

f32 = jnp.float32
bf16 = jnp.bfloat16


def _params(semantics):
    return pltpu.CompilerParams(dimension_semantics=semantics, vmem_limit_bytes=V7X_VMEM_LIMIT_BYTES)


def _rmsnorm(x, g):
    ms = jnp.mean(x * x, axis=-1, keepdims=True)
    return x * lax.rsqrt(ms + RMS_EPS) * g


def _norm_rows_to(x_ref, g_ref, dst_ref, dst_off, rows):
    g = g_ref[...]
    group = NORM_ROWS * NORM_UNROLL

    def step(c, carry):
        base = pl.multiple_of(c * group, group)
        for k in range(NORM_UNROLL):
            r = base + k * NORM_ROWS
            dst_ref[pl.ds(dst_off + r, NORM_ROWS), :] = (
                _rmsnorm(x_ref[pl.ds(r, NORM_ROWS), :], g).astype(dst_ref.dtype))
        return carry

    lax.fori_loop(0, rows // group, step, 0)


def _inproj_tile_kinds():
    starts = ((COL_RQ, "rot"), (COL_RK, "rot"), (COL_RV, "cast"), (COL_RG, "silu"), (COL_SQ, "cast"),
              (COL_GR, "sig"))
    chunk_kinds = [[k for c0, k in starts if c0 <= col][-1] for col in range(0, IN_WIDTH, MXU_COLS)]
    per_tile = INPROJ_TN // MXU_COLS
    return [tuple(chunk_kinds[t:t + per_tile]) for t in range(0, len(chunk_kinds), per_tile)]


_INPROJ_TILE_KINDS = _inproj_tile_kinds()
_T_RK = COL_RK // INPROJ_TN
assert COL_RK % INPROJ_TN == 0 and COL_RV % INPROJ_TN == 0


def _inproj_kernel(x_ref, g_ref, w_ref, rot_ref, o_ref, h_scr):
    j = pl.program_id(1)
    tm = x_ref.shape[0]

    @pl.when(j == 0)
    def _():
        _norm_rows_to(x_ref, g_ref, h_scr, 0, tm)

    def rotary(acc, rows):
        cos = rot_ref[rows, 0:RET_DK]
        sin = rot_ref[rows, RET_DK:2 * RET_DK]
        heads = []
        for hh in range(acc.shape[1] // RET_DK):
            xh = acc[:, hh * RET_DK:(hh + 1) * RET_DK]
            heads.append(xh * cos + pltpu.roll(xh, RET_DK // 2, axis=1) * sin)
        return jnp.concatenate(heads, axis=1)

    epilogues = {
        "rot": rotary,
        "cast": lambda acc, rows: acc,
        "silu": lambda acc, rows: acc * jax.nn.sigmoid(acc),
        "sig": lambda acc, rows: jax.nn.sigmoid(acc),
    }

    def tile(kinds):
        hm = tm // INPROJ_ROW_SPLIT
        for c, kind in enumerate(kinds):
            cs = slice(c * MXU_COLS, (c + 1) * MXU_COLS)
            w = w_ref[:, cs]
            for m in range(INPROJ_ROW_SPLIT):
                rows = slice(m * hm, (m + 1) * hm)
                acc = jnp.dot(h_scr[rows, :], w, preferred_element_type=f32)
                o_ref[rows, cs] = epilogues[kind](acc, rows).astype(o_ref.dtype)

    for kinds in sorted(set(_INPROJ_TILE_KINDS)):
        tiles = [t for t, k in enumerate(_INPROJ_TILE_KINDS) if k == kinds]
        cond = functools.reduce(lambda a, b: a | b, [j == t for t in tiles])
        pl.when(cond)(functools.partial(tile, kinds))


def _inproj(x2, g, w_in, rot, seq):
    T = x2.shape[0]
    tm, tn = INPROJ_TM, INPROJ_TN
    tm = min(tm, seq)
    spt = seq // tm
    grid = (T // tm, len(_INPROJ_TILE_KINDS))
    return pl.pallas_call(
        _inproj_kernel,
        grid=grid,
        in_specs=[
            pl.BlockSpec((tm, D_MODEL), lambda i, j: (i, 0)),
            pl.BlockSpec((1, D_MODEL), lambda i, j: (0, 0)),
            pl.BlockSpec((D_MODEL, tn), lambda i, j: (0, j)),
            pl.BlockSpec((None, tm, 2 * RET_DK), lambda i, j: (jnp.where(j >= _T_RK, 1, 0), i % spt, 0)),
        ],
        out_specs=pl.BlockSpec((tm, tn), lambda i, j: (i, j)),
        out_shape=jax.ShapeDtypeStruct((T, IN_WIDTH), bf16),
        scratch_shapes=[pltpu.VMEM((tm, D_MODEL), bf16)],
        compiler_params=_params(("parallel", "arbitrary")),
        name="inproj",
    )(x2, g, w_in, rot)


def _retention_kernel(q_ref, k_ref, v_ref, gate_ref, dmat_ref, xi_ref, zeta_ref, cdec_ref, o_ref, kv_scr, st_scr):
    S = q_ref.shape[0]
    C = RET_CHUNK
    N = S // C
    heads = range(RET_HEADS_PER_STEP)
    qk = [slice(hh * RET_DK, (hh + 1) * RET_DK) for hh in heads]
    vv = [slice(hh * RET_DV, (hh + 1) * RET_DV) for hh in heads]
    q3 = [q_ref[:, qk[hh]].reshape(N, C, RET_DK) for hh in heads]
    k3 = [k_ref[:, qk[hh]].reshape(N, C, RET_DK) for hh in heads]
    v3 = [v_ref[:, vv[hh]].reshape(N, C, RET_DV) for hh in heads]

    for hh in heads:
        kz = (k3[hh].astype(f32) * zeta_ref[hh][None]).astype(bf16)
        kv_scr[hh] = jnp.einsum("ndk,nkv->ndv", jnp.swapaxes(kz, 1, 2), v3[hh], preferred_element_type=f32)

    cdec = [cdec_ref[hh] for hh in heads]

    def scan(n, states):
        for hh in heads:
            st_scr[hh, n] = states[hh].astype(bf16)
        return tuple(states[hh] * cdec[hh] + kv_scr[hh, n] for hh in heads)

    lax.fori_loop(0, N, scan, tuple(jnp.zeros((RET_DK, RET_DV), f32) for _ in heads))

    for hh in heads:
        sc = jnp.einsum("nqd,nkd->nqk", q3[hh], k3[hh], preferred_element_type=f32) * dmat_ref[hh][None]
        inner = jnp.einsum("nqk,nkv->nqv", sc.astype(bf16), v3[hh], preferred_element_type=f32)
        cross = jnp.einsum("nqd,ndv->nqv", q3[hh], st_scr[hh], preferred_element_type=f32) * xi_ref[hh][None]
        o = inner + cross
        o = o * lax.rsqrt(jnp.mean(o * o, axis=-1, keepdims=True) + RMS_EPS)
        o = gate_ref[:, vv[hh]].astype(f32).reshape(N, C, RET_DV) * o
        o_ref[:, vv[hh]] = o.reshape(S, RET_DV).astype(o_ref.dtype)


def _retention_tables():
    C = RET_CHUNK
    log_gamma = jnp.log(1.0 - 2.0 ** (-5.0 - jnp.arange(RET_HEADS, dtype=f32)))
    idx = jnp.arange(C, dtype=f32)
    rel = idx[:, None] - idx[None, :]
    dmat = jnp.where(rel[None] >= 0, jnp.exp(log_gamma[:, None, None] * jnp.maximum(rel, 0.0)[None]), 0.0)
    xi = jnp.exp(log_gamma[:, None] * (idx + 1.0))
    zeta = jnp.exp(log_gamma[:, None] * (C - 1.0 - idx))
    cdec = jnp.exp(log_gamma * C)
    xi_b = jnp.broadcast_to(xi[:, :, None], (RET_HEADS, C, RET_DV))
    zeta_b = jnp.broadcast_to(zeta[:, :, None], (RET_HEADS, C, RET_DK))
    cdec_b = jnp.broadcast_to(cdec[:, None, None], (RET_HEADS, 1, RET_DV))
    return dmat.astype(f32), xi_b.astype(f32), zeta_b.astype(f32), cdec_b.astype(f32)


def _retention(proj, batch, seq):
    T = proj.shape[0]
    C = RET_CHUNK
    N = seq // C
    dmat, xi_b, zeta_b, cdec_b = _retention_tables()
    hp = RET_HEADS_PER_STEP
    qk_w, v_w = hp * RET_DK, hp * RET_DV
    return pl.pallas_call(
        _retention_kernel,
        grid=(batch, RET_HEADS // hp),
        in_specs=[
            pl.BlockSpec((seq, qk_w), lambda b, h: (b, COL_RQ // qk_w + h)),
            pl.BlockSpec((seq, qk_w), lambda b, h: (b, COL_RK // qk_w + h)),
            pl.BlockSpec((seq, v_w), lambda b, h: (b, COL_RV // v_w + h)),
            pl.BlockSpec((seq, v_w), lambda b, h: (b, COL_RG // v_w + h)),
            pl.BlockSpec((hp, C, C), lambda b, h: (h, 0, 0)),
            pl.BlockSpec((hp, C, RET_DV), lambda b, h: (h, 0, 0)),
            pl.BlockSpec((hp, C, RET_DK), lambda b, h: (h, 0, 0)),
            pl.BlockSpec((hp, 1, RET_DV), lambda b, h: (h, 0, 0)),
        ],
        out_specs=pl.BlockSpec((seq, v_w), lambda b, h: (b, h)),
        out_shape=jax.ShapeDtypeStruct((T, RET_V_W), bf16),
        scratch_shapes=[pltpu.VMEM((hp, N, RET_DK, RET_DV), f32), pltpu.VMEM((hp, N, RET_DK, RET_DV), bf16)],
        compiler_params=_params(("parallel", "parallel")),
        name="retention",
    )(proj, proj, proj, proj, dmat, xi_b, zeta_b, cdec_b)


KV_PER_STEP = SWA_KV_PER_STEP
Q_PER_KV = SWA_Q_HEADS // SWA_KV_HEADS
SWA_Q_STEP_W = KV_PER_STEP * Q_PER_KV * SWA_HEAD_DIM
assert KV_PER_STEP * SWA_HEAD_DIM == LANES and Q_PER_KV * SWA_HEAD_DIM == 2 * LANES


def _swa_kernel(sink_ref, bias_ref, q_ref, k_ref, v_ref, o_ref, klo, khi, vlo, vhi):
    S = q_ref.shape[0]
    C = SWA_BLOCK
    gp = pl.program_id(1)
    half = SWA_HEAD_DIM

    lane_s = lax.broadcasted_iota(jnp.int32, (S, LANES), 1)
    low_s = lane_s < half

    def prep(src_ref, lo_scr, hi_scr, scale, with_ones):
        w = src_ref[...].astype(f32) * scale
        r = pltpu.roll(w, half, axis=1)
        zeros = jnp.zeros((C, lo_scr.shape[2]), bf16)
        for t in range(KV_PER_STEP):
            lo_scr[t, 0:C, :] = zeros
            hi_scr[t, 0:C, :] = zeros
        lo_scr[0, C:C + S, 0:LANES] = jnp.where(low_s, w, 0.0).astype(bf16)
        hi_scr[0, C:C + S, 0:LANES] = jnp.where(low_s, 0.0, r).astype(bf16)
        lo_scr[1, C:C + S, 0:LANES] = jnp.where(low_s, r, 0.0).astype(bf16)
        hi_scr[1, C:C + S, 0:LANES] = jnp.where(low_s, 0.0, w).astype(bf16)
        if with_ones:
            for t in range(KV_PER_STEP):
                lo_scr[t, C:C + S, LANES:2 * LANES] = jnp.where(low_s, 1.0, 0.0).astype(bf16)
                hi_scr[t, C:C + S, LANES:2 * LANES] = jnp.where(low_s, 0.0, 1.0).astype(bf16)

    prep(k_ref, klo, khi, SWA_HEAD_DIM ** -0.5, False)
    prep(v_ref, vlo, vhi, 1.0, True)

    rows = 2 * C
    win = 2 * C
    first_pair = lax.broadcasted_iota(jnp.int32, (rows, 1), 0) < C
    low_o = lax.broadcasted_iota(jnp.int32, (rows, LANES), 1) < half
    nt = (((1,), (1,)), ((), ()))

    def softmax_parts(s, sink, bias):
        s = s + bias
        m = jnp.maximum(jnp.max(s, axis=-1, keepdims=True), sink)
        return jnp.exp(s - m).astype(bf16), jnp.exp(sink - m)

    def block(n, carry):
        r0 = pl.multiple_of(n * C, C)
        bias = bias_ref[jnp.minimum(n, 1)]
        for t in range(KV_PER_STEP):
            c0 = t * Q_PER_KV * SWA_HEAD_DIM
            hbase = gp * (KV_PER_STEP * Q_PER_KV) + t * Q_PER_KV
            qs = jnp.concatenate([q_ref[pl.ds(r0, C), c0:c0 + LANES],
                                  q_ref[pl.ds(r0, C), c0 + LANES:c0 + 2 * LANES]], axis=0)
            s_e = lax.dot_general(qs, klo[t, pl.ds(r0, win), :], nt, preferred_element_type=f32)
            s_o = lax.dot_general(qs, khi[t, pl.ds(r0, win), :], nt, preferred_element_type=f32)
            sink_e = jnp.where(first_pair, sink_ref[hbase + 0], sink_ref[hbase + 2])
            sink_o = jnp.where(first_pair, sink_ref[hbase + 1], sink_ref[hbase + 3])
            p_e, z_e = softmax_parts(s_e, sink_e, bias)
            p_o, z_o = softmax_parts(s_o, sink_o, bias)
            pv = (jnp.dot(p_e, vlo[t, pl.ds(r0, win), :], preferred_element_type=f32)
                  + jnp.dot(p_o, vhi[t, pl.ds(r0, win), :], preferred_element_type=f32))
            den = pv[:, LANES:2 * LANES] + jnp.where(low_o, z_e, z_o)
            o = pv[:, 0:LANES] / den
            o_ref[pl.ds(r0, C), c0:c0 + LANES] = o[0:C].astype(o_ref.dtype)
            o_ref[pl.ds(r0, C), c0 + LANES:c0 + 2 * LANES] = o[C:2 * C].astype(o_ref.dtype)
        return carry

    lax.fori_loop(0, S // C, block, 0, unroll=SWA_UNROLL)


def _swa_bias():
    C = SWA_BLOCK
    row_i = jnp.arange(2 * C)[:, None] % C
    col_j = jnp.arange(2 * C)[None, :]
    band = (col_j > row_i) & (col_j <= row_i + C)
    first = band & (col_j >= C)
    return jnp.where(jnp.stack([first, band]), 0.0, -jnp.inf).astype(f32)


def _swa(proj, sinks, batch, seq):
    T = proj.shape[0]
    steps = SWA_KV_HEADS // KV_PER_STEP
    k_scr = pltpu.VMEM((KV_PER_STEP, seq + SWA_BLOCK, LANES), bf16)
    v_scr = pltpu.VMEM((KV_PER_STEP, seq + SWA_BLOCK, 2 * LANES), bf16)
    bias = _swa_bias()
    return pl.pallas_call(
        _swa_kernel,
        grid=(batch, steps),
        in_specs=[
            pl.BlockSpec(memory_space=pltpu.SMEM),
            pl.BlockSpec(bias.shape, lambda b, g: (0, 0, 0)),
            pl.BlockSpec((seq, SWA_Q_STEP_W), lambda b, g: (b, COL_SQ // SWA_Q_STEP_W + g)),
            pl.BlockSpec((seq, LANES), lambda b, g: (b, COL_SK // LANES + g)),
            pl.BlockSpec((seq, LANES), lambda b, g: (b, COL_SV // LANES + g)),
        ],
        out_specs=pl.BlockSpec((seq, SWA_Q_STEP_W), lambda b, g: (b, g)),
        out_shape=jax.ShapeDtypeStruct((T, SWA_Q_W), bf16),
        scratch_shapes=[k_scr, k_scr, v_scr, v_scr],
        compiler_params=_params(("parallel", "parallel")),
        name="swa",
    )(sinks, bias, proj, proj, proj)


def _mix_kernel(ret_ref, swa_ref, *refs):
    n_gate = MIX_TN // MIX_GATE_W
    gr_refs, gs_refs = refs[:n_gate], refs[n_gate:2 * n_gate]
    wr_ref, ws_ref, wo_ref, x_ref, g_ref, o_ref = refs[2 * n_gate:]
    j = pl.program_id(1)
    nj = MIX_STEPS
    hm = o_ref.shape[0] // MIX_ROW_SPLIT

    def step(first, last):
        for m in range(MIX_ROW_SPLIT):
            rows = slice(m * hm, (m + 1) * hm)
            part = None
            for c in range(n_gate):
                cs = slice(c * MIX_GATE_W, (c + 1) * MIX_GATE_W)
                ret_out = jnp.dot(ret_ref[rows, :], wr_ref[:, cs], preferred_element_type=f32)
                swa_out = jnp.dot(swa_ref[rows, :], ws_ref[:, cs], preferred_element_type=f32)
                mixed = gr_refs[c][rows, :].astype(f32) * ret_out + gs_refs[c][rows, :].astype(f32) * swa_out
                d = jnp.dot(mixed.astype(bf16), wo_ref[cs, :], preferred_element_type=f32)
                part = d if part is None else part + d
            acc = part if first else o_ref[rows, :] + part
            if last:
                acc = x_ref[rows, :] + _rmsnorm(acc, g_ref[...])
            o_ref[rows, :] = acc

    pl.when(j == 0)(functools.partial(step, True, nj == 1))
    if nj > 2:
        pl.when((j > 0) & (j < nj - 1))(functools.partial(step, False, False))
    if nj > 1:
        pl.when(j == nj - 1)(functools.partial(step, False, True))


def _mix(ret, swa, proj, w_ret_o, w_swa_o, w_out, x2, g):
    T = x2.shape[0]
    tm, tn = min(MIX_TM, T), MIX_TN
    nj = D_MODEL // tn
    gw = MIX_GATE_W
    n_gate = tn // gw

    resident = dict(pipeline_mode=pl.Buffered(1)) if nj == 1 else {}

    def gate_specs(col0):
        return [pl.BlockSpec((tm, gw), functools.partial(lambda i, j, c: (i, col0 // gw + j * n_gate + c), c=c))
                for c in range(n_gate)]

    return pl.pallas_call(
        _mix_kernel,
        grid=(T // tm, nj),
        in_specs=[
            pl.BlockSpec((tm, RET_V_W), lambda i, j: (i, 0)),
            pl.BlockSpec((tm, SWA_Q_W), lambda i, j: (i, 0)),
            *gate_specs(COL_GR),
            *gate_specs(COL_GS),
            pl.BlockSpec((RET_V_W, tn), lambda i, j: (0, j), **resident),
            pl.BlockSpec((SWA_Q_W, tn), lambda i, j: (0, j), **resident),
            pl.BlockSpec((tn, D_MODEL), lambda i, j: (j, 0), **resident),
            pl.BlockSpec((tm, D_MODEL), lambda i, j: (i, 0)),
            pl.BlockSpec((1, D_MODEL), lambda i, j: (0, 0)),
        ],
        out_specs=pl.BlockSpec((tm, D_MODEL), lambda i, j: (i, 0)),
        out_shape=jax.ShapeDtypeStruct((T, D_MODEL), f32),
        compiler_params=_params(("parallel", "arbitrary")),
        name="mix",
    )(ret, swa, *([proj] * (2 * n_gate)), w_ret_o, w_swa_o, w_out, x2, g)


def _gelu_tanh(x):
    return x * (0.5 * (1.0 + jnp.tanh(0.7978845608028654 * (x + 0.044715 * (x * x * x)))))


def _ffn_kernel(x_ref, halo_ref, gpre_ref, wv_ref, wg_ref, cwv_ref, cwg_ref, cbv_ref, cbg_ref, wd_ref, gpost_ref,
                o_ref, h_scr, *u_scrs, tiles_per_seq):
    i = pl.program_id(0)
    j = pl.program_id(1)
    tm = x_ref.shape[0]
    tn = wv_ref.shape[1]
    H = FFN_HALO

    @pl.when(j == 0)
    def _():
        halo = jnp.where(i % tiles_per_seq == 0, 0.0, _rmsnorm(halo_ref[...], gpre_ref[...]))
        h_scr[0:H, :] = halo.astype(bf16)
        _norm_rows_to(x_ref, gpre_ref, h_scr, H, tm)
        o_ref[...] = jnp.zeros_like(o_ref)

    def conv(u_scr, row0, nrows, u_cols, w_cols, cw_ref, cb_ref):
        y = cb_ref[:, w_cols]
        for kk in range(CONV_WIDTH):
            y = y + u_scr[pl.ds(H - (CONV_WIDTH - 1) + kk + row0, nrows), u_cols] * cw_ref[kk:kk + 1, w_cols]
        return y

    def step(last):
        h = h_scr[...]
        hm = tm // FFN_ROW_SPLIT
        parts = [None] * FFN_ROW_SPLIT
        uv = slice(0, MXU_COLS)
        ug = slice(MXU_COLS, 2 * MXU_COLS)
        n_chunks = len(u_scrs)
        for c, u_scr in enumerate(u_scrs):
            cs = slice(c * MXU_COLS, (c + 1) * MXU_COLS)
            if c < n_chunks - 1:
                u_scr[:, uv] = jnp.dot(h, wv_ref[:, cs], preferred_element_type=f32)
                u_scr[:, ug] = jnp.dot(h, wg_ref[:, cs], preferred_element_type=f32)
            else:
                wv, wg = wv_ref[:, cs], wg_ref[:, cs]
                r0 = 0
                for m in range(FFN_ROW_SPLIT):
                    r1 = H + (m + 1) * hm
                    u_scr[r0:r1, uv] = jnp.dot(h_scr[r0:r1, :], wv, preferred_element_type=f32)
                    u_scr[r0:r1, ug] = jnp.dot(h_scr[r0:r1, :], wg, preferred_element_type=f32)
                    r0 = r1
            for m in range(FFN_ROW_SPLIT):
                val = conv(u_scr, m * hm, hm, uv, cs, cwv_ref, cbv_ref)
                gate = conv(u_scr, m * hm, hm, ug, cs, cwg_ref, cbg_ref)
                a = (_gelu_tanh(gate) * val).astype(bf16)
                d = jnp.dot(a, wd_ref[cs, :], preferred_element_type=f32)
                parts[m] = d if parts[m] is None else parts[m] + d
        for m in range(FFN_ROW_SPLIT):
            rows = slice(m * hm, (m + 1) * hm)
            acc = o_ref[rows, :] + parts[m]
            if last:
                acc = x_ref[rows, :] + _rmsnorm(acc, gpost_ref[...])
            o_ref[rows, :] = acc

    pl.when(j < pl.num_programs(1) - 1)(functools.partial(step, False))
    pl.when(j == pl.num_programs(1) - 1)(functools.partial(step, True))


def _ffn(x1, g_pre, w_up, conv_w, conv_b, w_down, g_post, seq):
    T = x1.shape[0]
    tm, tn = min(FFN_TM, seq), FFN_TN
    nj = D_FF // tn
    hb = tm // FFN_HALO
    kern = functools.partial(_ffn_kernel, tiles_per_seq=seq // tm)
    return pl.pallas_call(
        kern,
        grid=(T // tm, nj),
        in_specs=[
            pl.BlockSpec((tm, D_MODEL), lambda i, j: (i, 0)),
            pl.BlockSpec((FFN_HALO, D_MODEL), lambda i, j: (jnp.maximum(i * hb - 1, 0), 0)),
            pl.BlockSpec((1, D_MODEL), lambda i, j: (0, 0)),
            pl.BlockSpec((D_MODEL, tn), lambda i, j: (0, j)),
            pl.BlockSpec((D_MODEL, tn), lambda i, j: (0, nj + j)),
            pl.BlockSpec((CONV_WIDTH, tn), lambda i, j: (0, j)),
            pl.BlockSpec((CONV_WIDTH, tn), lambda i, j: (0, nj + j)),
            pl.BlockSpec((1, tn), lambda i, j: (0, j)),
            pl.BlockSpec((1, tn), lambda i, j: (0, nj + j)),
            pl.BlockSpec((tn, D_MODEL), lambda i, j: (j, 0)),
            pl.BlockSpec((1, D_MODEL), lambda i, j: (0, 0)),
        ],
        out_specs=pl.BlockSpec((tm, D_MODEL), lambda i, j: (i, 0)),
        out_shape=jax.ShapeDtypeStruct((T, D_MODEL), f32),
        scratch_shapes=[
            pltpu.VMEM((tm + FFN_HALO, D_MODEL), bf16),
        ] + [pltpu.VMEM((tm + FFN_HALO, 2 * MXU_COLS), f32)] * (tn // MXU_COLS),
        compiler_params=_params(("parallel", "arbitrary")),
        name="ffn",
    )(x1, x1, g_pre, w_up, w_up, conv_w, conv_w, conv_b, conv_b, w_down, g_post)


def _rotary_tables(seq):
    d = RET_DK
    inv = 1.0 / (ROPE_BASE ** (jnp.arange(0, d, 2, dtype=f32) / d))
    ang = jnp.arange(seq, dtype=jnp.int32).astype(f32)[:, None] * inv[None, :]
    cos, sin = jnp.cos(ang), jnp.sin(ang)
    tab = jnp.concatenate([cos, cos, -sin, sin], axis=-1)
    return jnp.stack([tab, tab * (d ** -0.5)])


def kernel(x, g_pre_mix, w_in, w_ret_o, w_swa_o, w_out, swa_sinks, g_post_mix, g_pre_ffn, w_up, conv_w, conv_b,
           w_down, g_post_ffn):
    B, S, D = x.shape
    depth = w_in.shape[0]
    assert D == D_MODEL and w_in.shape[1:] == (D_MODEL, IN_WIDTH) and w_up.shape[1:] == (D_MODEL, 2 * D_FF)
    assert S % RET_CHUNK == 0 and S % SWA_BLOCK == 0
    assert S % min(INPROJ_TM, S) == 0 and S % min(FFN_TM, S) == 0 and (B * S) % min(MIX_TM, B * S) == 0
    rot = _rotary_tables(S)
    x2 = x.reshape(B * S, D)
    for l in range(depth):
        proj = _inproj(x2, g_pre_mix[l][None], w_in[l].astype(bf16), rot, S)
        ret = _retention(proj, B, S)
        swa = _swa(proj, swa_sinks[l].astype(f32), B, S)
        x2 = _mix(ret, swa, proj, w_ret_o[l].astype(bf16), w_swa_o[l].astype(bf16), w_out[l].astype(bf16), x2,
                  g_post_mix[l][None])
        x2 = _ffn(x2, g_pre_ffn[l][None], w_up[l].astype(bf16), conv_w[l], conv_b[l][None], w_down[l].astype(bf16),
                  g_post_ffn[l][None], S)
    return x2.reshape(B, S, D)
```
